```python
import math
import jax
import jax.numpy as jnp
from jax import lax

D_MODEL = 1024
BATCH = 16
SEQ = 2048
DEPTH = 2
DEC_BATCH = 32
DEC_SEQ = 16
PAST_LEN = 1024

F32 = jnp.float32
CHUNK = 64
Q_BLOCK = 128
H_A = 4
DK_A = 64
DV_A = 128
RET_THETA = 10000.0
H_B = 4
DK_B = 64
DV_B = 2 * DK_B
ROT_B = DK_B // 4
ROPE_THETA = 500000.0
DIFF_LAYER = 0
DIFF_LAMBDA_INIT = 0.8 - 0.6 * math.exp(-0.3 * DIFF_LAYER)
H_C = 8
DK_C = 128
DV_C = 128
C_W = H_C * DV_C
CONV_W = 4
N_MEM = 256
X_HEADS = 4
X_HD = D_MODEL // X_HEADS
D_FF = 2816
N_EXPERTS = 8
TOP_K = 2
D_FF_E = 3584
DN_ALPHA = (2 * DEPTH) ** 0.25
DN_BETA = (8 * DEPTH) ** -0.25
LN_EPS = 1e-5
A_QK = H_A * DK_A
A_V = H_A * DV_A
B_QK = H_B * 2 * DK_B
B_V = H_B * DV_B
SPLIT0 = (A_QK, 2 * A_QK, 2 * A_QK + A_V, 2 * A_QK + 2 * A_V, 2 * A_QK + 2 * A_V + B_QK, 2 * A_QK + 2 * A_V + 2 * B_QK)
PROJ0 = 2 * A_QK + 2 * A_V + 2 * B_QK + B_V
SPLIT1 = (3 * C_W, 4 * C_W, 4 * C_W + H_C)
PROJ1 = 4 * C_W + 2 * H_C

kernel_name = 'streaming_hybrid_retention_diffattn_gdn_step'


def layer_norm(x, g, b):
    xf = x.astype(F32)
    mu = jnp.mean(xf, axis=-1, keepdims=True)
    var = jnp.mean(jnp.square(xf - mu), axis=-1, keepdims=True)
    return ((xf - mu) * lax.rsqrt(var + LN_EPS) * g.astype(F32) + b.astype(F32)).astype(x.dtype)


def head_group_norm(x, g):
    xf = x.astype(F32)
    mu = jnp.mean(xf, axis=-1, keepdims=True)
    var = jnp.mean(jnp.square(xf - mu), axis=-1, keepdims=True)
    return (xf - mu) * lax.rsqrt(var + LN_EPS) * g.astype(F32)


def rms_norm(x, g):
    xf = x.astype(F32)
    return xf * lax.rsqrt(jnp.mean(jnp.square(xf), axis=-1, keepdims=True) + LN_EPS) * g.astype(F32)


def l2_normalize(x):
    xf = x.astype(F32)
    return xf * lax.rsqrt(jnp.sum(jnp.square(xf), axis=-1, keepdims=True) + 1e-6)


def deepnorm(x, fx, g, b):
    return layer_norm(DN_ALPHA * x + fx.astype(x.dtype), g, b)


def rope_angles(pos, dim, theta):
    inv = theta ** (-jnp.arange(0, dim, 2, dtype=F32) / dim)
    ang = pos.astype(F32)[:, None] * inv[None, :]
    return jnp.cos(ang), jnp.sin(ang)


def rotate(x, cos, sin):
    x1, x2 = jnp.split(x.astype(F32), 2, axis=-1)
    return jnp.concatenate([x1 * cos - x2 * sin, x2 * cos + x1 * sin], axis=-1).astype(x.dtype)


def run_chunks(step, state, *xs):
    L = xs[0].shape[1]
    if L <= CHUNK:
        return step(state, *xs)
    nc = L // CHUNK
    blocks = tuple(t.reshape((t.shape[0], nc, CHUNK) + t.shape[2:]).swapaxes(0, 1) for t in xs)
    state, out = lax.scan(lambda s, c: step(s, *c), state, blocks)
    out = out.swapaxes(0, 1)
    return state, out.reshape((out.shape[0], L) + out.shape[3:])


def retention_chunk(S, q, k, v):
    L = q.shape[1]
    lg = jnp.log1p(-jnp.exp2(-5.0 - jnp.arange(H_A, dtype=F32)))
    t = jnp.arange(L, dtype=F32)
    rel = t[:, None] - t[None, :]
    causal = rel >= 0
    dmat = jnp.where(causal, jnp.exp(lg[:, None, None] * jnp.where(causal, rel, 0.0)), 0.0)
    qf, kf, vf = (a.astype(F32) for a in (q, k, v))
    scores = jnp.einsum('blhd,bmhd->bhlm', qf, kf) * dmat
    o = jnp.einsum('bhlm,bmhe->blhe', scores, vf)
    o = o + jnp.einsum('blhd,bhde->blhe', qf, S) * jnp.exp(lg * (t[:, None] + 1.0))[:, :, None]
    kd = kf * jnp.exp(lg * (L - 1.0 - t[:, None]))[:, :, None]
    S = S * jnp.exp(lg * L)[None, :, None, None] + jnp.einsum('blhd,blhe->bhde', kd, vf)
    return S, o


def gdn_chunk(S, q, k, v, glog, beta):
    L = q.shape[1]
    q, k, v = (t.astype(F32).transpose(0, 2, 1, 3) for t in (q, k, v))
    g = jnp.cumsum(glog, axis=1).transpose(0, 2, 1)
    bt = beta.transpose(0, 2, 1)[..., None]
    incl = jnp.tril(jnp.ones((L, L), bool))
    strict = jnp.tril(jnp.ones((L, L), bool), -1)
    decay = jnp.exp(jnp.where(incl, g[..., :, None] - g[..., None, :], -jnp.inf))
    kb = k * bt
    a_low = jnp.where(strict, jnp.einsum('bhld,bhmd->bhlm', kb, k) * decay, 0.0)
    rhs = jnp.concatenate([v * bt, kb * jnp.exp(g)[..., None]], axis=-1)
    sol = lax.linalg.triangular_solve(a_low, rhs, left_side=True, lower=True, unit_diagonal=True)
    u = sol[..., :DV_C] - jnp.einsum('bhld,bhde->bhle', sol[..., DV_C:], S)
    o = (jnp.einsum('bhld,bhde->bhle', q * jnp.exp(g)[..., None], S)
         + jnp.einsum('bhlm,bhme->bhle', jnp.einsum('bhld,bhmd->bhlm', q, k) * decay, u))
    g_last = g[..., -1:]
    S = S * jnp.exp(g_last)[..., None] + jnp.einsum('bhld,bhle->bhde', k * jnp.exp(g_last - g)[..., None], u)
    return S, o.transpose(0, 2, 1, 3)


def diff_attn_core(q, k, v, mask, lam):
    s = jnp.einsum('bqhcd,bkhcd->bhcqk', q, k).astype(F32) * (DK_B ** -0.5)
    s = jnp.where(mask, s, jnp.finfo(F32).min)
    pr = jax.nn.softmax(s, axis=-1)
    a = pr[:, :, 0] - lam * pr[:, :, 1]
    return jnp.einsum('bhqk,bkhe->bqhe', a.astype(v.dtype), v)


def diff_attn_prompt(q, k, v, lam):
    B, S = q.shape[:2]
    nb = S // Q_BLOCK
    kchunk = jnp.arange(S) // CHUNK
    qblocks = q.reshape((B, nb, Q_BLOCK) + q.shape[2:]).swapaxes(0, 1)

    def one_block(args):
        qi, i = args
        qchunk = (i * Q_BLOCK + jnp.arange(Q_BLOCK)) // CHUNK
        return diff_attn_core(qi, k, v, kchunk[None, :] <= qchunk[:, None], lam)

    out = lax.map(one_block, (qblocks, jnp.arange(nb)))
    return out.swapaxes(0, 1).reshape(B, S, H_B, DV_B)


def even_token_mixer(x, pos, ret_state, past_k, past_v, p):
    B, L, _ = x.shape
    qa, ka, va, ga, qb, kb, vb = jnp.split(x @ p['w_in0'], SPLIT0, axis=-1)
    ca, sa = rope_angles(pos, DK_A, RET_THETA)
    ca, sa = ca[:, None, :], sa[:, None, :]
    qa = rotate(qa.reshape(B, L, H_A, DK_A), ca, sa)
    ka = rotate(ka.reshape(B, L, H_A, DK_A), ca, sa) * (DK_A ** -0.5)
    ret_state, oa = run_chunks(retention_chunk, ret_state, qa, ka, va.reshape(B, L, H_A, DV_A))
    oa = head_group_norm(oa, p['ret_norm_g'].reshape(H_A, DV_A)).reshape(B, L, A_V) * jax.nn.silu(ga.astype(F32))
    cb, sb = rope_angles(pos, ROT_B, ROPE_THETA)
    cb, sb = cb[:, None, None, :], sb[:, None, None, :]
    qb = qb.reshape(B, L, H_B, 2, DK_B)
    kb = kb.reshape(B, L, H_B, 2, DK_B)
    qb = jnp.concatenate([rotate(qb[..., :ROT_B], cb, sb), qb[..., ROT_B:]], axis=-1)
    kb = jnp.concatenate([rotate(kb[..., :ROT_B], cb, sb), kb[..., ROT_B:]], axis=-1)
    vb = vb.reshape(B, L, H_B, DV_B)
    lp = p['diff_lambda'].astype(F32)
    lam = jnp.exp(jnp.sum(lp[0] * lp[1])) - jnp.exp(jnp.sum(lp[2] * lp[3])) + DIFF_LAMBDA_INIT
    if past_k is None:
        ob = diff_attn_prompt(qb, kb, vb, lam)
    else:
        P = past_k.shape[1]
        k_all = jnp.concatenate([past_k.reshape(B, P, H_B, 2, DK_B).astype(kb.dtype), kb], axis=1)
        v_all = jnp.concatenate([past_v.astype(vb.dtype), vb], axis=1)
        kpos = jnp.arange(P + L)
        mask = (kpos // CHUNK)[None, :] <= (pos // CHUNK)[:, None]
        ob = diff_attn_core(qb, k_all, v_all, mask, lam)
    ob = rms_norm(ob, p['diff_norm_g']) * (1.0 - DIFF_LAMBDA_INIT)
    o = jnp.concatenate([oa, ob.reshape(B, L, B_V)], axis=-1).astype(x.dtype)
    return o @ p['w_out0'], ret_state, kb.reshape(B, L, H_B, 2 * DK_B), vb


def causal_dwconv(xpad, w):
    return lax.conv_general_dilated(xpad, w[:, None, :].astype(xpad.dtype), window_strides=(1,), padding='VALID',
                                    dimension_numbers=('NWC', 'WIO', 'NWC'), feature_group_count=xpad.shape[-1])


def odd_token_mixer(x, conv_state, gdn_state, p):
    B, L, _ = x.shape
    qkv, z, a, b = jnp.split(x @ p['w_in1'], SPLIT1, axis=-1)
    xpad = jnp.concatenate([conv_state.astype(qkv.dtype), qkv], axis=1)
    new_conv = xpad[:, xpad.shape[1] - (CONV_W - 1):]
    q, k, v = jnp.split(jax.nn.silu(causal_dwconv(xpad, p['conv_w'])), 3, axis=-1)
    q = l2_normalize(q.reshape(B, L, H_C, DK_C)) * (DK_C ** -0.5)
    k = l2_normalize(k.reshape(B, L, H_C, DK_C))
    v = v.reshape(B, L, H_C, DV_C)
    beta = jax.nn.sigmoid(b.astype(F32))
    glog = -jnp.exp(p['a_log'].astype(F32)) * jax.nn.softplus(a.astype(F32) + p['dt_bias'].astype(F32))
    gdn_state, o = run_chunks(gdn_chunk, gdn_state, q, k, v, glog, beta)
    o = rms_norm(o, p['gdn_norm_g']) * jax.nn.silu(z.reshape(B, L, H_C, DV_C).astype(F32))
    return o.reshape(B, L, C_W).astype(x.dtype) @ p['w_out1'], new_conv, gdn_state


def memory_cross_attn(x, mk, mv, w_q, w_o):
    B, L, _ = x.shape
    q = (x @ w_q).reshape(B, L, X_HEADS, X_HD)
    s = jnp.einsum('blhd,bmhd->bhlm', q, mk.astype(q.dtype)).astype(F32) * (X_HD ** -0.5)
    pr = jax.nn.softmax(s, axis=-1)
    o = jnp.einsum('bhlm,bmhd->blhd', pr.astype(x.dtype), mv.astype(x.dtype))
    return o.reshape(B, L, D_MODEL) @ w_o


def dense_swiglu(x, w_gu, w_d):
    g, u = jnp.split(x @ w_gu, 2, axis=-1)
    return (jax.nn.silu(g) * u) @ w_d


def moe_swiglu(x, w_router, w_gu, w_d):
    shp = x.shape
    xt = x.reshape(-1, shp[-1])
    logits = (xt @ w_router).astype(F32)
    top_val, top_idx = lax.top_k(logits, TOP_K)
    gates = jax.nn.softmax(top_val, axis=-1)
    combine = jnp.einsum('tk,tke->te', gates, jax.nn.one_hot(top_idx, N_EXPERTS, dtype=F32)).astype(x.dtype)
    y = jnp.zeros_like(xt)
    for e in range(N_EXPERTS):
        g, u = jnp.split(xt @ w_gu[e], 2, axis=-1)
        y = y + combine[:, e:e + 1] * ((jax.nn.silu(g) * u) @ w_d[e])
    return y.reshape(shp)


def trunk(x, pos, mem_k, mem_v, ret_state, past_k, past_v, conv_state, gdn_state, p):
    for layer in range(DEPTH):
        if layer % 2 == 0:
            f, ret_state, k_rows, v_rows = even_token_mixer(x, pos, ret_state, past_k, past_v, p)
        else:
            f, conv_state, gdn_state = odd_token_mixer(x, conv_state, gdn_state, p)
        x = deepnorm(x, f, p['ln_g'][layer, 0], p['ln_b'][layer, 0])
        xa = memory_cross_attn(x, mem_k[layer], mem_v[layer], p['w_xq'][layer], p['w_xo'][layer])
        x = deepnorm(x, xa, p['ln_g'][layer, 1], p['ln_b'][layer, 1])
        if layer % 2 == 0:
            ff = dense_swiglu(x, p['w_ffn_gu'], p['w_ffn_d'])
        else:
            ff = moe_swiglu(x, p['w_router'], p['w_moe_gu'], p['w_moe_d'])
        x = deepnorm(x, ff, p['ln_g'][layer, 2], p['ln_b'][layer, 2])
    return x, k_rows, v_rows, ret_state, conv_state, gdn_state


def setup_inputs(seed: int = 0) -> dict:
    key = jax.random.key(seed)
    keys = iter(jax.random.split(key, 40))

    def nrm(shape, scale=1.0):
        return jax.random.normal(next(keys), shape, F32) * scale

    dt = jnp.exp(jax.random.uniform(next(keys), (H_C,), F32, math.log(1e-3), math.log(1e-1)))
    a_log = jnp.log(jax.random.uniform(next(keys), (H_C,), F32, 1.0, 16.0))
    return {
        'x_prompt': nrm((BATCH, SEQ, D_MODEL)),
        'x_sample': nrm((DEC_BATCH, DEC_SEQ, D_MODEL)),
        'cache_diff_k': nrm((DEC_BATCH, PAST_LEN, H_B, 2 * DK_B)),
        'cache_diff_v': nrm((DEC_BATCH, PAST_LEN, H_B, DV_B)),
        'state_ret': nrm((DEC_BATCH, H_A, DK_A, DV_A), 0.5),
        'state_gdn_conv': nrm((DEC_BATCH, CONV_W - 1, 3 * C_W)),
        'state_gdn': nrm((DEC_BATCH, H_C, DK_C, DV_C), 0.1),
        'cache_mem_k': nrm((DEPTH, DEC_BATCH, N_MEM, X_HEADS, X_HD)),
        'cache_mem_v': nrm((DEPTH, DEC_BATCH, N_MEM, X_HEADS, X_HD)),
        'mem_prompt': nrm((BATCH, N_MEM, D_MODEL)),
        'w_in0': nrm((D_MODEL, PROJ0), D_MODEL ** -0.5),
        'ret_norm_g': 1.0 + nrm((A_V,), 0.02),
        'diff_lambda': nrm((4, DK_B), 0.1),
        'diff_norm_g': 1.0 + nrm((DV_B,), 0.02),
        'w_out0': nrm((A_V + B_V, D_MODEL), DN_BETA * (A_V + B_V) ** -0.5),
        'w_in1': nrm((D_MODEL, PROJ1), D_MODEL ** -0.5),
        'conv_w': nrm((CONV_W, 3 * C_W), CONV_W ** -0.5),
        'a_log': a_log,
        'dt_bias': dt + jnp.log(-jnp.expm1(-dt)),
        'gdn_norm_g': 1.0 + nrm((DV_C,), 0.02),
        'w_out1': nrm((C_W, D_MODEL), DN_BETA * C_W ** -0.5),
        'w_xq': nrm((DEPTH, D_MODEL, D_MODEL), D_MODEL ** -0.5),
        'w_xkv': nrm((DEPTH, D_MODEL, 2 * D_MODEL), D_MODEL ** -0.5),
        'w_xo': nrm((DEPTH, D_MODEL, D_MODEL), DN_BETA * D_MODEL ** -0.5),
        'w_ffn_gu': nrm((D_MODEL, 2 * D_FF), D_MODEL ** -0.5),
        'w_ffn_d': nrm((D_FF, D_MODEL), DN_BETA * D_FF ** -0.5),
        'w_router': nrm((D_MODEL, N_EXPERTS), D_MODEL ** -0.5),
        'w_moe_gu': nrm((N_EXPERTS, D_MODEL, 2 * D_FF_E), D_MODEL ** -0.5),
        'w_moe_d': nrm((N_EXPERTS, D_FF_E, D_MODEL), DN_BETA * D_FF_E ** -0.5),
        'ln_g': 1.0 + nrm((DEPTH, 3, D_MODEL), 0.02),
        'ln_b': nrm((DEPTH, 3, D_MODEL), 0.02),
    }


def reference(x_prompt, x_sample, cache_diff_k, cache_diff_v, state_ret, state_gdn_conv, state_gdn,
              cache_mem_k, cache_mem_v, mem_prompt, w_in0, ret_norm_g, diff_lambda, diff_norm_g, w_out0,
              w_in1, conv_w, a_log, dt_bias, gdn_norm_g, w_out1, w_xq, w_xkv, w_xo, w_ffn_gu, w_ffn_d,
              w_router, w_moe_gu, w_moe_d, ln_g, ln_b):
    p = dict(w_in0=w_in0, ret_norm_g=ret_norm_g, diff_lambda=diff_lambda, diff_norm_g=diff_norm_g,
             w_out0=w_out0, w_in1=w_in1, conv_w=conv_w, a_log=a_log, dt_bias=dt_bias,
             gdn_norm_g=gdn_norm_g, w_out1=w_out1, w_xq=w_xq, w_xo=w_xo, w_ffn_gu=w_ffn_gu,
             w_ffn_d=w_ffn_d, w_router=w_router, w_moe_gu=w_moe_gu, w_moe_d=w_moe_d, ln_g=ln_g, ln_b=ln_b)
    B, S = x_prompt.shape[0], x_prompt.shape[1]
    mk_p, mv_p = jnp.split(jnp.einsum('bmd,lde->lbme', mem_prompt, w_xkv), 2, axis=-1)
    mk_p = mk_p.reshape(DEPTH, B, N_MEM, X_HEADS, X_HD)
    mv_p = mv_p.reshape(DEPTH, B, N_MEM, X_HEADS, X_HD)
    y_p, dk_p, dv_p, ret_p, conv_p, gdn_p = trunk(
        x_prompt, jnp.arange(S), mk_p, mv_p, jnp.zeros((B, H_A, DK_A, DV_A), F32), None, None,
        jnp.zeros((B, CONV_W - 1, 3 * C_W), x_prompt.dtype), jnp.zeros((B, H_C, DK_C, DV_C), F32), p)
    P, L = cache_diff_k.shape[1], x_sample.shape[1]
    y_s, dk_s, dv_s, ret_s, conv_s, gdn_s = trunk(
        x_sample, P + jnp.arange(L), cache_mem_k, cache_mem_v, state_ret.astype(F32), cache_diff_k,
        cache_diff_v, state_gdn_conv, state_gdn.astype(F32), p)
    return (y_p, y_s, mk_p, mv_p, dk_p, dv_p, ret_p, conv_p, gdn_p, dk_s, dv_s, ret_s, conv_s, gdn_s)
```

```python
import functools
import math

import jax
import jax.numpy as jnp
from jax import lax
from jax.experimental import pallas as pl
from jax.experimental.pallas import tpu as pltpu

F32 = jnp.float32
BF16 = jnp.bfloat16
I32 = jnp.int32

D_MODEL = 1024
DEPTH = 2
CHUNK = 64
H_A, DK_A, DV_A = 4, 64, 128
RET_THETA = 10000.0
H_B, DK_B = 4, 64
DV_B = 2 * DK_B
ROT_B = DK_B // 4
ROPE_THETA = 500000.0
DIFF_LAMBDA_INIT = 0.8 - 0.6 * math.exp(-0.3 * 0)
H_C, DK_C, DV_C = 8, 128, 128
C_W = H_C * DV_C
CONV_W = 4
N_MEM = 256
X_HEADS = 4
X_HD = D_MODEL // X_HEADS
D_FF = 2816
N_EXPERTS = 8
D_FF_E = 3584
DN_ALPHA = (2 * DEPTH) ** 0.25
LN_EPS = 1e-5
A_QK = H_A * DK_A
A_V = H_A * DV_A
B_QK = H_B * 2 * DK_B
B_V = H_B * DV_B
PROJ0 = 2 * A_QK + 2 * A_V + 2 * B_QK + B_V

VMEM_LIMIT_V7X = 52 * 1024 * 1024
NEG_BIG = -1e30

MOE_TOK_TILE = 512
MOE_SUB = 64
MOE_FF_BLK = 512


def _cparams(sem):
    return pltpu.CompilerParams(dimension_semantics=sem, vmem_limit_bytes=VMEM_LIMIT_V7X)


def _dot(a, b):
    return jnp.dot(a, b, preferred_element_type=F32)


def _dot_nt(a, b):
    return lax.dot_general(a, b, (((1,), (1,)), ((), ())), preferred_element_type=F32)


def _dot_tn(a, b):
    return lax.dot_general(a, b, (((0,), (0,)), ((), ())), preferred_element_type=F32)


def _silu(x):
    return x * (1.0 / (1.0 + jnp.exp(-x)))


def _layer_norm_rows(y, g, b):
    mu = jnp.mean(y, axis=-1, keepdims=True)
    d = y - mu
    var = jnp.mean(d * d, axis=-1, keepdims=True)
    return d * lax.rsqrt(var + LN_EPS) * g + b


def _mm_kernel(x_ref, w_ref, o_ref, *, scale):
    acc = _dot(x_ref[...].astype(BF16), w_ref[0])
    if scale != 1.0:
        acc = acc * scale
    o_ref[0] = acc.astype(o_ref.dtype)


def _matmul(x, w, out_dtype, tm, tn, scale=1.0, name="matmul"):
    M, K = x.shape
    G, _, N = w.shape
    tm = min(tm, M)
    tn = min(tn, N)
    assert M % tm == 0 and N % tn == 0
    return pl.pallas_call(
        functools.partial(_mm_kernel, scale=scale),
        out_shape=jax.ShapeDtypeStruct((G, M, N), out_dtype),
        grid=(G, M // tm, N // tn),
        in_specs=[pl.BlockSpec((tm, K), lambda g, i, j: (i, 0)),
                  pl.BlockSpec((1, K, tn), lambda g, i, j: (g, 0, j))],
        out_specs=pl.BlockSpec((1, tm, tn), lambda g, i, j: (g, i, j)),
        compiler_params=_cparams(("parallel", "parallel", "arbitrary")),
        name=name,
    )(x, w)


def _mm_dn_kernel(*refs, n_in):
    xs = refs[:n_in]
    ws = refs[n_in:2 * n_in]
    r_ref, g_ref, b_ref, o_ref = refs[2 * n_in:]
    acc = DN_ALPHA * r_ref[...]
    for x_ref, w_ref in zip(xs, ws):
        acc = acc + _dot(x_ref[...].astype(BF16), w_ref[...])
    o_ref[...] = _layer_norm_rows(acc, g_ref[...], b_ref[...])


def _matmul_deepnorm(xs, ws, resid, g, b, tm, name="matmul_deepnorm"):
    M = resid.shape[0]
    tm = min(tm, M)
    assert M % tm == 0
    n_in = len(xs)
    in_specs = [pl.BlockSpec((tm, x.shape[1]), lambda i: (i, 0)) for x in xs]
    in_specs += [pl.BlockSpec(w.shape, lambda i: (0, 0)) for w in ws]
    in_specs += [pl.BlockSpec((tm, D_MODEL), lambda i: (i, 0)),
                 pl.BlockSpec((1, D_MODEL), lambda i: (0, 0)),
                 pl.BlockSpec((1, D_MODEL), lambda i: (0, 0))]
    return pl.pallas_call(
        functools.partial(_mm_dn_kernel, n_in=n_in),
        out_shape=jax.ShapeDtypeStruct((M, D_MODEL), F32),
        grid=(M // tm,),
        in_specs=in_specs,
        out_specs=pl.BlockSpec((tm, D_MODEL), lambda i: (i, 0)),
        compiler_params=_cparams(("parallel",)),
        name=name,
    )(*xs, *ws, resid, g.reshape(1, D_MODEL), b.reshape(1, D_MODEL))


def _swap_halves(x, group, half):
    n = x.shape[-1]
    lane = lax.broadcasted_iota(I32, x.shape, x.ndim - 1) % group
    up = pltpu.roll(x, n - half, x.ndim - 1)
    dn = pltpu.roll(x, half, x.ndim - 1)
    return jnp.where(lane < half, up, dn)


def _proj0_kernel(x_ref, w_ref, ca_ref, sa_ref, cq_ref, sq_ref, ck_ref, sk_ref,
                  qka_ref, va_ref, ga_ref, qb_ref, kb_ref, vb_ref):
    j = pl.program_id(1)
    acc = _dot(x_ref[...].astype(BF16), w_ref[...])

    @pl.when(j == 0)
    def _():
        y = acc * ca_ref[...] + _swap_halves(acc, DK_A, DK_A // 2) * sa_ref[...]
        qka_ref[...] = y.astype(qka_ref.dtype)

    @pl.when(j == 1)
    def _():
        va_ref[...] = acc.astype(va_ref.dtype)

    @pl.when(j == 2)
    def _():
        ga_ref[...] = acc

    @pl.when(j == 3)
    def _():
        y = acc * cq_ref[...] + _swap_halves(acc, DK_B, ROT_B // 2) * sq_ref[...]
        qb_ref[...] = y.astype(qb_ref.dtype)

    @pl.when(j == 4)
    def _():
        kb_ref[...] = acc * ck_ref[...] + _swap_halves(acc, DK_B, ROT_B // 2) * sk_ref[...]

    @pl.when(j == 5)
    def _():
        vb_ref[...] = acc


def _rope_tables(pos):
    pos = pos.astype(F32)
    inv = RET_THETA ** (-jnp.arange(0, DK_A, 2, dtype=F32) / DK_A)
    ang = pos[:, None] * inv[None, :]
    c, s = jnp.cos(ang), jnp.sin(ang)
    c64 = jnp.concatenate([c, c], axis=-1)
    s64 = jnp.concatenate([-s, s], axis=-1)
    kscale = DK_A ** -0.5
    ca = jnp.concatenate([jnp.tile(c64, (1, H_A)), jnp.tile(c64, (1, H_A)) * kscale], axis=-1)
    sa = jnp.concatenate([jnp.tile(s64, (1, H_A)), jnp.tile(s64, (1, H_A)) * kscale], axis=-1)
    invb = ROPE_THETA ** (-jnp.arange(0, ROT_B, 2, dtype=F32) / ROT_B)
    angb = pos[:, None] * invb[None, :]
    cb, sb = jnp.cos(angb), jnp.sin(angb)
    rest = DK_B - ROT_B
    c64b = jnp.concatenate([cb, cb, jnp.ones((pos.shape[0], rest), F32)], axis=-1)
    s64b = jnp.concatenate([-sb, sb, jnp.zeros((pos.shape[0], rest), F32)], axis=-1)
    ck = jnp.tile(c64b, (1, 2 * H_B))
    sk = jnp.tile(s64b, (1, 2 * H_B))
    qscale = DK_B ** -0.5
    return ca, sa, ck * qscale, sk * qscale, ck, sk


def _proj0(x, w_in0_bf, tables, tm):
    T = x.shape[0]
    P = tables[0].shape[0]
    tm = min(tm, T, P)
    assert T % tm == 0 and P % tm == 0
    npb = P // tm
    blk = 512
    assert PROJ0 == 6 * blk
    row = lambda i, j: (i, 0)
    tab = pl.BlockSpec((tm, blk), lambda i, j: (i % npb, 0))
    outs = [jax.ShapeDtypeStruct((T, blk), dt) for dt in (BF16, BF16, F32, BF16, F32, F32)]
    return pl.pallas_call(
        _proj0_kernel,
        out_shape=outs,
        grid=(T // tm, 6),
        in_specs=[pl.BlockSpec((tm, D_MODEL), row),
                  pl.BlockSpec((D_MODEL, blk), lambda i, j: (0, j))] + [tab] * 6,
        out_specs=[pl.BlockSpec((tm, blk), row)] * 6,
        compiler_params=_cparams(("parallel", "arbitrary")),
        name="proj0_rotary",
    )(x, w_in0_bf, *tables)


def _retention_kernel(qk_ref, v_ref, g_ref, dmat_ref, qdec_ref, kdec_ref, sdec_ref, s0_ref, ng_ref,
                      o_ref, sout_ref, s_scr, *, lc, l_real):
    c = pl.program_id(1)

    @pl.when(c == 0)
    def _():
        s_scr[...] = s0_ref[0]

    if l_real < lc:
        live = lax.broadcasted_iota(I32, (lc, 1), 0) < l_real
    for h in range(H_A):
        q = qk_ref[:, h * DK_A:(h + 1) * DK_A]
        k = qk_ref[:, A_QK + h * DK_A:A_QK + (h + 1) * DK_A]
        v = v_ref[:, h * DV_A:(h + 1) * DV_A]
        if l_real < lc:
            k = jnp.where(live, k, jnp.zeros_like(k))
            v = jnp.where(live, v, jnp.zeros_like(v))
        s = s_scr[h]
        scores = _dot_nt(q, k) * dmat_ref[h]
        o = _dot(scores.astype(BF16), v) + _dot(q, s.astype(BF16)) * qdec_ref[h]
        kd = (k.astype(F32) * kdec_ref[h]).astype(BF16)
        s_scr[h] = s * sdec_ref[h] + _dot_tn(kd, v)
        mu = jnp.mean(o, axis=-1, keepdims=True)
        d = o - mu
        var = jnp.mean(d * d, axis=-1, keepdims=True)
        gate = g_ref[:, h * DV_A:(h + 1) * DV_A]
        y = d * lax.rsqrt(var + LN_EPS) * ng_ref[:, h * DV_A:(h + 1) * DV_A] * _silu(gate)
        o_ref[:, h * DV_A:(h + 1) * DV_A] = y.astype(o_ref.dtype)

    @pl.when(c == pl.num_programs(1) - 1)
    def _():
        sout_ref[0] = s_scr[...]


def _retention_tables(lc, l_real):
    lg = jnp.log1p(-jnp.exp2(-5.0 - jnp.arange(H_A, dtype=F32)))
    t = jnp.arange(lc, dtype=F32)
    rel = t[:, None] - t[None, :]
    causal = rel >= 0
    dmat = jnp.where(causal, jnp.exp(lg[:, None, None] * jnp.where(causal, rel, 0.0)), 0.0)
    qdec = jnp.exp(lg[:, None] * (t[None, :] + 1.0))
    kdec = jnp.exp(lg[:, None] * jnp.maximum(l_real - 1.0 - t[None, :], 0.0))
    sdec = jnp.exp(lg * l_real)
    return (dmat,
            jnp.broadcast_to(qdec[:, :, None], (H_A, lc, DV_A)),
            jnp.broadcast_to(kdec[:, :, None], (H_A, lc, DK_A)),
            jnp.broadcast_to(sdec[:, None, None], (H_A, DK_A, DV_A)))


def _retention(qka, va, ga, state0, ret_norm_g, n_batch, lc, l_real):
    T = qka.shape[0]
    nc = T // (n_batch * lc)
    assert nc * n_batch * lc == T and (nc == 1 or l_real == lc)
    dmat, qdec, kdec, sdec = _retention_tables(lc, l_real)
    row = lambda b, c: (b * nc + c, 0)
    const3 = lambda b, c: (0, 0, 0)
    return pl.pallas_call(
        functools.partial(_retention_kernel, lc=lc, l_real=l_real),
        out_shape=[jax.ShapeDtypeStruct((T, A_V), BF16),
                   jax.ShapeDtypeStruct((n_batch, H_A, DK_A, DV_A), F32)],
        grid=(n_batch, nc),
        in_specs=[pl.BlockSpec((lc, 2 * A_QK), row),
                  pl.BlockSpec((lc, A_V), row),
                  pl.BlockSpec((lc, A_V), row),
                  pl.BlockSpec((H_A, lc, lc), const3),
                  pl.BlockSpec((H_A, lc, DV_A), const3),
                  pl.BlockSpec((H_A, lc, DK_A), const3),
                  pl.BlockSpec((H_A, DK_A, DV_A), const3),
                  pl.BlockSpec((1, H_A, DK_A, DV_A), lambda b, c: (b, 0, 0, 0)),
                  pl.BlockSpec((1, A_V), lambda b, c: (0, 0))],
        out_specs=[pl.BlockSpec((lc, A_V), row),
                   pl.BlockSpec((1, H_A, DK_A, DV_A), lambda b, c: (b, 0, 0, 0))],
        scratch_shapes=[pltpu.VMEM((H_A, DK_A, DV_A), F32)],
        compiler_params=_cparams(("parallel", "arbitrary")),
        name="retention",
    )(qka, va, ga, dmat, qdec, kdec, sdec, state0, ret_norm_g.reshape(1, A_V))


def _diff_finish(acc1, l1, acc2, l2, lam, ng):
    o = acc1 * (1.0 / l1) - lam * (acc2 * (1.0 / l2))
    ms = jnp.mean(o * o, axis=-1, keepdims=True)
    return o * lax.rsqrt(ms + LN_EPS) * ng * (1.0 - DIFF_LAMBDA_INIT)


def _diff_prompt_kernel(lam_ref, q_ref, k_ref, v_ref, ng_ref, o_ref, kbf, vbf, *, tq):
    qi = pl.program_id(2)

    @pl.when(qi == 0)
    def _():
        kbf[...] = k_ref[...].astype(BF16)
        vbf[...] = v_ref[...].astype(BF16)

    q1 = q_ref[:, :DK_B]
    q2 = q_ref[:, DK_B:]

    def block(kblk, carry, masked):
        m1, l1, a1, m2, l2, a2 = carry
        off = pl.multiple_of(kblk * tq, tq)
        kk = kbf[pl.ds(off, tq), :]
        vv = vbf[pl.ds(off, tq), :]
        s1 = _dot_nt(q1, kk[:, :DK_B])
        s2 = _dot_nt(q2, kk[:, DK_B:])
        if masked:
            rc = lax.broadcasted_iota(I32, (tq, tq), 0) // CHUNK
            cc = lax.broadcasted_iota(I32, (tq, tq), 1) // CHUNK
            vis = cc <= rc
            s1 = jnp.where(vis, s1, NEG_BIG)
            s2 = jnp.where(vis, s2, NEG_BIG)
        out = []
        for s, m, l, a in ((s1, m1, l1, a1), (s2, m2, l2, a2)):
            mn = jnp.maximum(m, jnp.max(s, axis=-1, keepdims=True))
            alpha = jnp.exp(m - mn)
            p = jnp.exp(s - mn)
            l = alpha * l + jnp.sum(p, axis=-1, keepdims=True)
            a = alpha * a + _dot(p.astype(BF16), vv)
            out += [mn, l, a]
        return tuple(out)

    init = (jnp.full((tq, 1), NEG_BIG, F32), jnp.zeros((tq, 1), F32), jnp.zeros((tq, DV_B), F32)) * 2
    carry = lax.fori_loop(0, qi, lambda kb, c: block(kb, c, False), init)
    m1, l1, a1, m2, l2, a2 = block(qi, carry, True)
    o_ref[...] = _diff_finish(a1, l1, a2, l2, lam_ref[0], ng_ref[...]).astype(o_ref.dtype)


def _diff_prompt(qb, kb, vb, lam, diff_norm_g, n_batch, seq, tq):
    T = qb.shape[0]
    assert seq % tq == 0 and tq % CHUNK == 0
    nq = seq // tq
    return pl.pallas_call(
        functools.partial(_diff_prompt_kernel, tq=tq),
        out_shape=jax.ShapeDtypeStruct((T, B_V), BF16),
        grid_spec=pltpu.PrefetchScalarGridSpec(
            num_scalar_prefetch=1,
            grid=(n_batch, H_B, nq),
            in_specs=[pl.BlockSpec((tq, DV_B), lambda b, h, i, lam: (b * nq + i, h)),
                      pl.BlockSpec((seq, DV_B), lambda b, h, i, lam: (b, h)),
                      pl.BlockSpec((seq, DV_B), lambda b, h, i, lam: (b, h)),
                      pl.BlockSpec((1, DV_B), lambda b, h, i, lam: (0, 0))],
            out_specs=pl.BlockSpec((tq, DV_B), lambda b, h, i, lam: (b * nq + i, h)),
            scratch_shapes=[pltpu.VMEM((seq, DV_B), BF16), pltpu.VMEM((seq, DV_B), BF16)]),
        compiler_params=_cparams(("parallel", "parallel", "arbitrary")),
        name="diff_attn_prompt",
    )(lam, qb, kb, vb, diff_norm_g.reshape(1, DV_B))


def _diff_step_kernel(lam_ref, q_ref, kp_ref, vp_ref, kn_ref, vn_ref, ng_ref, o_ref, *, past, ln):
    q = q_ref[...]
    kp = kp_ref[...].astype(BF16)
    vp = vp_ref[...].astype(BF16)
    kn = kn_ref[...].astype(BF16)
    vn = vn_ref[...].astype(BF16)
    qchunk = (past + lax.broadcasted_iota(I32, (ln, 1), 0)) // CHUNK
    vis_p = (lax.broadcasted_iota(I32, (ln, past), 1) // CHUNK) <= qchunk
    vis_n = ((past + lax.broadcasted_iota(I32, (ln, ln), 1)) // CHUNK) <= qchunk
    accs = []
    for c in range(2):
        sl = slice(c * DK_B, (c + 1) * DK_B)
        sp = jnp.where(vis_p, _dot_nt(q[:, sl], kp[:, sl]), NEG_BIG)
        sn = jnp.where(vis_n, _dot_nt(q[:, sl], kn[:, sl]), NEG_BIG)
        m = jnp.maximum(jnp.max(sp, axis=-1, keepdims=True), jnp.max(sn, axis=-1, keepdims=True))
        pp = jnp.exp(sp - m)
        pn = jnp.exp(sn - m)
        l = jnp.sum(pp, axis=-1, keepdims=True) + jnp.sum(pn, axis=-1, keepdims=True)
        accs += [_dot(pp.astype(BF16), vp) + _dot(pn.astype(BF16), vn), l]
    o_ref[...] = _diff_finish(accs[0], accs[1], accs[2], accs[3], lam_ref[0], ng_ref[...]).astype(o_ref.dtype)


def _diff_step(qb, kb, vb, past_k, past_v, lam, diff_norm_g, n_batch, ln):
    past = past_k.shape[0] // n_batch
    blk = lambda b, h, lam: (b, h)
    return pl.pallas_call(
        functools.partial(_diff_step_kernel, past=past, ln=ln),
        out_shape=jax.ShapeDtypeStruct((n_batch * ln, B_V), BF16),
        grid_spec=pltpu.PrefetchScalarGridSpec(
            num_scalar_prefetch=1,
            grid=(n_batch, H_B),
            in_specs=[pl.BlockSpec((ln, DV_B), blk),
                      pl.BlockSpec((past, DV_B), blk),
                      pl.BlockSpec((past, DV_B), blk),
                      pl.BlockSpec((ln, DV_B), blk),
                      pl.BlockSpec((ln, DV_B), blk),
                      pl.BlockSpec((1, DV_B), lambda b, h, lam: (0, 0))],
            out_specs=pl.BlockSpec((ln, DV_B), blk)),
        compiler_params=_cparams(("parallel", "parallel")),
        name="diff_attn_step",
    )(lam, qb, past_k, past_v, kb, vb, diff_norm_g.reshape(1, DV_B))


def _xattn_kernel(q_ref, mk_ref, mv_ref, o_ref):
    for h in range(X_HEADS):
        sl = slice(h * X_HD, (h + 1) * X_HD)
        s = _dot_nt(q_ref[:, sl], mk_ref[:, sl].astype(BF16))
        p = jnp.exp(s - jnp.max(s, axis=-1, keepdims=True))
        l = jnp.sum(p, axis=-1, keepdims=True)
        o = _dot(p.astype(BF16), mv_ref[:, sl].astype(BF16)) * (1.0 / l)
        o_ref[:, sl] = o.astype(o_ref.dtype)


def _xattn(q, mk, mv, n_batch, tq):
    T = q.shape[0]
    per = T // n_batch
    tq = min(tq, per)
    nt = per // tq
    return pl.pallas_call(
        _xattn_kernel,
        out_shape=jax.ShapeDtypeStruct((T, D_MODEL), BF16),
        grid=(n_batch, nt),
        in_specs=[pl.BlockSpec((tq, D_MODEL), lambda b, t: (b * nt + t, 0)),
                  pl.BlockSpec((N_MEM, D_MODEL), lambda b, t: (b, 0)),
                  pl.BlockSpec((N_MEM, D_MODEL), lambda b, t: (b, 0))],
        out_specs=pl.BlockSpec((tq, D_MODEL), lambda b, t: (b * nt + t, 0)),
        compiler_params=_cparams(("parallel", "arbitrary")),
        name="mem_xattn",
    )(q, mk, mv)


def _swiglu_up_kernel(x_ref, wg_ref, wu_ref, o_ref):
    xb = x_ref[...].astype(BF16)
    g = _dot(xb, wg_ref[...])
    u = _dot(xb, wu_ref[...])
    o_ref[...] = (_silu(g) * u).astype(o_ref.dtype)


def _swiglu_up(x, w_gu_bf, tm, tn):
    T = x.shape[0]
    tm = min(tm, T)
    nj = D_FF // tn
    assert D_FF % tn == 0
    return pl.pallas_call(
        _swiglu_up_kernel,
        out_shape=jax.ShapeDtypeStruct((T, D_FF), BF16),
        grid=(T // tm, nj),
        in_specs=[pl.BlockSpec((tm, D_MODEL), lambda i, j: (i, 0)),
                  pl.BlockSpec((D_MODEL, tn), lambda i, j: (0, j)),
                  pl.BlockSpec((D_MODEL, tn), lambda i, j: (0, nj + j))],
        out_specs=pl.BlockSpec((tm, tn), lambda i, j: (i, j)),
        compiler_params=_cparams(("parallel", "arbitrary")),
        name="swiglu_up",
    )(x, w_gu_bf, w_gu_bf)


def _conv_kernel(x_ref, prev_ref, st_ref, w_ref, o_ref, buf, *, tm):
    t = pl.program_id(1)
    cb = pl.program_id(2)
    buf[0:8, :] = jnp.where(t == 0, st_ref[0], prev_ref[...])
    buf[8:8 + tm, :] = x_ref[...]
    y = buf[8:8 + tm, :] * w_ref[3:4, :]
    for i in range(CONV_W - 1):
        y = y + buf[5 + i:5 + i + tm, :] * w_ref[i:i + 1, :]
    y = _silu(y)

    @pl.when(cb < 2)
    def _():
        scale = jnp.where(cb == 0, DK_C ** -0.5, 1.0)
        for h in range(H_C):
            yh = y[:, h * DK_C:(h + 1) * DK_C]
            ss = jnp.sum(yh * yh, axis=-1, keepdims=True)
            o_ref[:, h * DK_C:(h + 1) * DK_C] = (yh * (lax.rsqrt(ss + 1e-6) * scale)).astype(o_ref.dtype)

    @pl.when(cb == 2)
    def _():
        o_ref[...] = y.astype(o_ref.dtype)


def _gdn_conv(proj1, conv_state8, conv_w, n_batch, tm):
    T = proj1.shape[0]
    per = T // n_batch
    tm = min(tm, per)
    nt = per // tm
    assert per % tm == 0 and tm % 8 == 0
    return pl.pallas_call(
        functools.partial(_conv_kernel, tm=tm),
        out_shape=jax.ShapeDtypeStruct((T, 3 * C_W), BF16),
        grid=(n_batch, nt, 3),
        in_specs=[pl.BlockSpec((tm, C_W), lambda b, t, c: (b * nt + t, c)),
                  pl.BlockSpec((8, C_W), lambda b, t, c: (jnp.maximum((b * nt + t) * (tm // 8) - 1, 0), c)),
                  pl.BlockSpec((1, 8, C_W), lambda b, t, c: (b, 0, c)),
                  pl.BlockSpec((CONV_W, C_W), lambda b, t, c: (0, c))],
        out_specs=pl.BlockSpec((tm, C_W), lambda b, t, c: (b * nt + t, c)),
        scratch_shapes=[pltpu.VMEM((8 + tm, C_W), F32)],
        compiler_params=_cparams(("parallel", "arbitrary", "arbitrary")),
        name="gdn_conv",
    )(proj1, proj1, conv_state8, conv_w)


def _gdn_kernel(q_ref, k_ref, v_ref, z_ref, ab_ref, alog_ref, dtb_ref, ng_ref, s0_ref,
                o_ref, sout_ref, s_scr, *, lc, l_real):
    c = pl.program_id(1)

    @pl.when(c == 0)
    def _():
        s_scr[...] = s0_ref[0]

    ri = lax.broadcasted_iota(I32, (lc, lc), 0)
    ci = lax.broadcasted_iota(I32, (lc, lc), 1)
    incl = ci <= ri
    strict = ci < ri
    eye = (ci == ri).astype(F32)
    ab = ab_ref[...]
    sp = jnp.maximum(ab + dtb_ref[...], 0.0) + jnp.log1p(jnp.exp(-jnp.abs(ab + dtb_ref[...])))
    glog = -jnp.exp(alog_ref[...]) * sp
    if l_real < lc:
        live = lax.broadcasted_iota(I32, (lc, 1), 0) < l_real
        glog = jnp.where(live, glog, 0.0)
    gcum = jnp.dot(incl.astype(F32), glog, preferred_element_type=F32, precision=lax.Precision.HIGHEST)
    gcum_t = gcum.T
    beta_all = 1.0 / (1.0 + jnp.exp(-ab))

    for h in range(H_C):
        sl = slice(h * DK_C, (h + 1) * DK_C)
        q = q_ref[:, sl]
        k = k_ref[:, sl]
        v = v_ref[:, sl].astype(F32)
        if l_real < lc:
            k = jnp.where(live, k, jnp.zeros_like(k))
            v = jnp.where(live, v, 0.0)
        kf = k.astype(F32)
        gcol = gcum[:, h:h + 1]
        grow = gcum_t[h:h + 1, :]
        beta = beta_all[:, H_C + h:H_C + h + 1]
        decay = jnp.exp(jnp.where(incl, gcol - grow, NEG_BIG))
        eg = jnp.exp(gcol)
        kb = kf * beta
        a_low = jnp.where(strict, _dot_nt(kb.astype(BF16), k) * decay, 0.0)
        tinv = eye
        s = 1
        while s < lc:
            off = ((ri // (2 * s)) == (ci // (2 * s))) & (((ri // s) % 2) == 1) & (((ci // s) % 2) == 0)
            a_off = jnp.where(off, a_low, 0.0).astype(BF16)
            tb = tinv.astype(BF16)
            tinv = tinv - _dot(tb, _dot(a_off, tb).astype(BF16))
            s *= 2
        rhs = jnp.concatenate([v * beta, kb * eg], axis=-1).astype(BF16)
        sol = _dot(tinv.astype(BF16), rhs)
        st = s_scr[h]
        stb = st.astype(BF16)
        u = sol[:, :DV_C] - _dot(sol[:, DV_C:].astype(BF16), stb)
        ub = u.astype(BF16)
        qk = _dot_nt(q, k) * decay
        o = _dot((q.astype(F32) * eg).astype(BF16), stb) + _dot(qk.astype(BF16), ub)
        g_last = gcol[lc - 1:lc, :]
        kdec = (kf * jnp.exp(g_last - gcol)).astype(BF16)
        s_scr[h] = st * jnp.exp(g_last) + _dot_tn(kdec, ub)
        ms = jnp.mean(o * o, axis=-1, keepdims=True)
        y = o * lax.rsqrt(ms + LN_EPS) * ng_ref[...] * _silu(z_ref[:, sl])
        o_ref[:, sl] = y.astype(o_ref.dtype)

    @pl.when(c == pl.num_programs(1) - 1)
    def _():
        sout_ref[0] = s_scr[...]


def _gdn(qkvn, proj1, ab, a_log, dt_bias, gdn_norm_g, state0, n_batch, lc, l_real):
    T = qkvn.shape[0]
    nc = T // (n_batch * lc)
    assert nc * n_batch * lc == T and (nc == 1 or l_real == lc)
    pad = lambda v: jnp.zeros((1, 128), F32).at[0, :H_C].set(v.astype(F32))
    return pl.pallas_call(
        functools.partial(_gdn_kernel, lc=lc, l_real=l_real),
        out_shape=[jax.ShapeDtypeStruct((T, C_W), BF16),
                   jax.ShapeDtypeStruct((n_batch, H_C, DK_C, DV_C), F32)],
        grid=(n_batch, nc),
        in_specs=[pl.BlockSpec((lc, C_W), lambda b, c: (b * nc + c, 0)),
                  pl.BlockSpec((lc, C_W), lambda b, c: (b * nc + c, 1)),
                  pl.BlockSpec((lc, C_W), lambda b, c: (b * nc + c, 2)),
                  pl.BlockSpec((lc, C_W), lambda b, c: (b * nc + c, 3)),
                  pl.BlockSpec((lc, 128), lambda b, c: (b * nc + c, 0)),
                  pl.BlockSpec((1, 128), lambda b, c: (0, 0)),
                  pl.BlockSpec((1, 128), lambda b, c: (0, 0)),
                  pl.BlockSpec((1, DV_C), lambda b, c: (0, 0)),
                  pl.BlockSpec((1, H_C, DK_C, DV_C), lambda b, c: (b, 0, 0, 0))],
        out_specs=[pl.BlockSpec((lc, C_W), lambda b, c: (b * nc + c, 0)),
                   pl.BlockSpec((1, H_C, DK_C, DV_C), lambda b, c: (b, 0, 0, 0))],
        scratch_shapes=[pltpu.VMEM((H_C, DK_C, DV_C), F32)],
        compiler_params=_cparams(("parallel", "arbitrary")),
        name="gated_delta",
    )(qkvn, qkvn, qkvn, proj1, ab, pad(a_log), pad(dt_bias), gdn_norm_g.reshape(1, DV_C), state0)


def _router_kernel(x_ref, wt_ref, ri_ref, rf_ref, cnt_ref):
    tt = x_ref.shape[0]
    logits = lax.dot_general(wt_ref[...], x_ref[...], (((1,), (1,)), ((), ())),
                             preferred_element_type=F32, precision=lax.Precision.HIGHEST)
    eid = lax.broadcasted_iota(I32, (N_EXPERTS, tt), 0).astype(F32)
    m1 = jnp.max(logits, axis=0, keepdims=True)
    e1 = jnp.min(jnp.where(logits == m1, eid, float(N_EXPERTS)), axis=0, keepdims=True)
    rest = jnp.where(eid == e1, -jnp.inf, logits)
    m2 = jnp.max(rest, axis=0, keepdims=True)
    e2 = jnp.min(jnp.where(rest == m2, eid, float(N_EXPERTS)), axis=0, keepdims=True)
    ev = jnp.exp(m2 - m1)
    g1 = 1.0 / (1.0 + ev)
    g2 = ev / (1.0 + ev)
    hit1 = eid == e1
    hit2 = eid == e2
    member = jnp.where(hit1 | hit2, 1.0, 0.0)
    before = (lax.broadcasted_iota(I32, (tt, tt), 0) < lax.broadcasted_iota(I32, (tt, tt), 1))
    rank = _dot(member.astype(BF16), jnp.where(before, 1.0, 0.0).astype(BF16))
    r1 = jnp.sum(jnp.where(hit1, rank, 0.0), axis=0, keepdims=True).astype(I32)
    r2 = jnp.sum(jnp.where(hit2, rank, 0.0), axis=0, keepdims=True).astype(I32)
    zi = jnp.zeros((4, tt), I32)
    ri_ref[0] = jnp.concatenate([e1.astype(I32), e2.astype(I32), r1, r2, zi], axis=0)
    rf_ref[0] = jnp.concatenate([g1, g2, jnp.zeros((6, tt), F32)], axis=0)
    cnt = jnp.sum(member, axis=1, keepdims=True).astype(I32)
    cnt_ref[0] = jnp.broadcast_to(cnt, (N_EXPERTS, 128))


def _router(x, w_router):
    T = x.shape[0]
    tt = MOE_TOK_TILE
    nt = T // tt
    assert T % tt == 0
    return pl.pallas_call(
        _router_kernel,
        out_shape=[jax.ShapeDtypeStruct((nt, 8, tt), I32),
                   jax.ShapeDtypeStruct((nt, 8, tt), F32),
                   jax.ShapeDtypeStruct((nt, N_EXPERTS, 128), I32)],
        grid=(nt,),
        in_specs=[pl.BlockSpec((tt, D_MODEL), lambda i: (i, 0)),
                  pl.BlockSpec((N_EXPERTS, D_MODEL), lambda i: (0, 0))],
        out_specs=[pl.BlockSpec((1, 8, tt), lambda i: (i, 0, 0)),
                   pl.BlockSpec((1, 8, tt), lambda i: (i, 0, 0)),
                   pl.BlockSpec((1, N_EXPERTS, 128), lambda i: (i, 0, 0))],
        compiler_params=_cparams(("parallel",)),
        name="moe_router",
    )(x, w_router.T)


def _moe_tables(cnt, n_tiles, group):
    E = N_EXPERTS
    nb_max = (2 * n_tiles * MOE_TOK_TILE) // MOE_SUB + n_tiles * E
    nb_tot = -(-(nb_max + E * (group - 1)) // group) * group
    nblk = (cnt + MOE_SUB - 1) // MOE_SUB
    nbe = jnp.sum(nblk, axis=0)
    nbe_pad = (nbe + group - 1) // group * group
    base = jnp.cumsum(nbe_pad) - nbe_pad
    off = jnp.cumsum(nblk, axis=0) - nblk
    dest0 = (base[None, :] + off).reshape(-1)
    flat = nblk.reshape(-1)
    ends = jnp.cumsum(flat)
    total = ends[-1]
    n = jnp.arange(nb_tot, dtype=I32)
    pair = jnp.minimum(jnp.searchsorted(ends, n, side="right"), n_tiles * E - 1).astype(I32)
    j = n - (ends - flat)[pair]
    valid = n < total
    npad = nbe_pad - nbe
    pends = jnp.cumsum(npad)
    m = n - total
    pe = jnp.minimum(jnp.searchsorted(pends, m, side="right"), E - 1).astype(I32)
    pdest = base[pe] + nbe[pe] + (m - (pends - npad)[pe])
    is_pad = (m >= 0) & (m < pends[-1])
    last_tile = n_tiles - 1
    blk_tile = jnp.where(valid, pair // E, last_tile).astype(I32)
    blk_e = jnp.where(valid, pair % E, -1).astype(I32)
    blk_j = jnp.where(valid, j, 0).astype(I32)
    blk_dest = jnp.where(valid, dest0[pair] + j, jnp.where(is_pad, pdest, nb_tot)).astype(I32)
    prev_tile = jnp.concatenate([jnp.full((1,), -1, I32), blk_tile[:-1]])
    next_tile = jnp.concatenate([blk_tile[1:], jnp.full((1,), -1, I32)])
    next_valid = jnp.concatenate([valid[1:], jnp.zeros((1,), bool)])
    first = (valid & (blk_tile != prev_tile)).astype(I32)
    last = (valid & ((blk_tile != next_tile) | ~next_valid)).astype(I32)
    n_rt = nb_tot // group
    r0 = jnp.arange(n_rt, dtype=I32) * group
    tot_pad = jnp.sum(nbe_pad)
    rt_valid = r0 < tot_pad
    rt_e = jnp.minimum(jnp.searchsorted(jnp.cumsum(nbe_pad), r0, side="right"), E - 1).astype(I32)
    last_e = rt_e[jnp.maximum((tot_pad // group) - 1, 0)]
    rt_e = jnp.where(rt_valid, rt_e, last_e).astype(I32)
    return dict(nb_tot=nb_tot, n_rt=n_rt, blk_tile=blk_tile, blk_e=blk_e, blk_j=blk_j, blk_dest=blk_dest,
                blk_valid=valid.astype(I32), first=first, last=last, rt_e=rt_e, rt_valid=rt_valid.astype(I32))


def _select_rows(ri_ref, e, j):
    tt = ri_ref.shape[-1]
    rr = j * MOE_SUB + lax.broadcasted_iota(I32, (MOE_SUB, tt), 0)
    ri = ri_ref[0]
    sel1 = (ri[0:1, :] == e) & (ri[2:3, :] == rr)
    sel2 = (ri[1:2, :] == e) & (ri[3:4, :] == rr)
    return sel1, sel2


def _moe_gather_kernel(tile_ref, e_ref, j_ref, dest_ref, x_ref, ri_ref, rf_ref, xg_ref, gate_ref, xbf):
    n = pl.program_id(0)
    changed = jnp.logical_or(n == 0, tile_ref[n] != tile_ref[jnp.maximum(n - 1, 0)])

    @pl.when(changed)
    def _():
        xbf[...] = x_ref[...].astype(BF16)

    sel1, sel2 = _select_rows(ri_ref, e_ref[n], j_ref[n])
    sel = jnp.where(sel1 | sel2, 1.0, 0.0).astype(BF16)
    xg_ref[...] = _dot(sel, xbf[...]).astype(xg_ref.dtype)
    rf = rf_ref[0]
    gate = jnp.where(sel1, rf[0:1, :], 0.0) + jnp.where(sel2, rf[1:2, :], 0.0)
    gate_ref[...] = jnp.sum(gate, axis=-1, keepdims=True)


def _moe_gather(x, ri, rf, tb):
    nb = tb["nb_tot"]
    tt = MOE_TOK_TILE
    return pl.pallas_call(
        _moe_gather_kernel,
        out_shape=[jax.ShapeDtypeStruct(((nb + 1) * MOE_SUB, D_MODEL), BF16),
                   jax.ShapeDtypeStruct(((nb + 1) * MOE_SUB, 1), F32)],
        grid_spec=pltpu.PrefetchScalarGridSpec(
            num_scalar_prefetch=4,
            grid=(nb,),
            in_specs=[pl.BlockSpec((tt, D_MODEL), lambda n, t, e, j, d: (t[n], 0)),
                      pl.BlockSpec((1, 8, tt), lambda n, t, e, j, d: (t[n], 0, 0)),
                      pl.BlockSpec((1, 8, tt), lambda n, t, e, j, d: (t[n], 0, 0))],
            out_specs=[pl.BlockSpec((MOE_SUB, D_MODEL), lambda n, t, e, j, d: (d[n], 0)),
                       pl.BlockSpec((MOE_SUB, 1), lambda n, t, e, j, d: (d[n], 0))],
            scratch_shapes=[pltpu.VMEM((tt, D_MODEL), BF16)]),
        compiler_params=_cparams(("arbitrary",)),
        name="moe_gather",
    )(tb["blk_tile"], tb["blk_e"], tb["blk_j"], tb["blk_dest"], x, ri, rf)


def _moe_ffn_kernel(e_ref, valid_ref, x_ref, wg_ref, wu_ref, wd_ref, gate_ref, y_ref, acc):
    r = pl.program_id(0)
    f = pl.program_id(1)
    nf = pl.num_programs(1)
    ok = valid_ref[r] == 1

    @pl.when(ok)
    def _():
        xb = x_ref[...]
        g = _dot(xb, wg_ref[0])
        u = _dot(xb, wu_ref[0])
        hmid = (_silu(g) * u).astype(BF16)
        part = _dot(hmid, wd_ref[0])

        @pl.when(f == 0)
        def _():
            acc[...] = part

        @pl.when(f > 0)
        def _():
            acc[...] += part

        @pl.when(f == nf - 1)
        def _():
            y_ref[...] = (acc[...] * gate_ref[...]).astype(y_ref.dtype)

    @pl.when(jnp.logical_and(jnp.logical_not(ok), f == nf - 1))
    def _():
        y_ref[...] = jnp.zeros_like(y_ref)


def _moe_ffn(xg, gates, w_gu_bf, w_d_bf, tb, rows):
    n_rt = tb["n_rt"]
    nf = D_FF_E // MOE_FF_BLK
    fb = MOE_FF_BLK

    def fsel(r, f, e, v):
        return jnp.where(v[r] == 1, f, nf - 1)

    return pl.pallas_call(
        _moe_ffn_kernel,
        out_shape=jax.ShapeDtypeStruct(xg.shape, BF16),
        grid_spec=pltpu.PrefetchScalarGridSpec(
            num_scalar_prefetch=2,
            grid=(n_rt, nf),
            in_specs=[pl.BlockSpec((rows, D_MODEL), lambda r, f, e, v: (r, 0)),
                      pl.BlockSpec((1, D_MODEL, fb), lambda r, f, e, v: (e[r], 0, fsel(r, f, e, v))),
                      pl.BlockSpec((1, D_MODEL, fb), lambda r, f, e, v: (e[r], 0, nf + fsel(r, f, e, v))),
                      pl.BlockSpec((1, fb, D_MODEL), lambda r, f, e, v: (e[r], fsel(r, f, e, v), 0)),
                      pl.BlockSpec((rows, 1), lambda r, f, e, v: (r, 0))],
            out_specs=pl.BlockSpec((rows, D_MODEL), lambda r, f, e, v: (r, 0)),
            scratch_shapes=[pltpu.VMEM((rows, D_MODEL), F32)]),
        compiler_params=_cparams(("arbitrary", "arbitrary")),
        name="moe_grouped_ffn",
    )(tb["rt_e"], tb["rt_valid"], xg, w_gu_bf, w_gu_bf, w_d_bf, gates)


def _moe_combine_kernel(tile_ref, e_ref, j_ref, dest_ref, valid_ref, first_ref, last_ref,
                        y_ref, ri_ref, x_ref, g_ref, b_ref, o_ref, acc):
    n = pl.program_id(0)

    @pl.when(valid_ref[n] == 1)
    def _():
        sel1, sel2 = _select_rows(ri_ref, e_ref[n], j_ref[n])
        sel = jnp.where(sel1 | sel2, 1.0, 0.0).astype(BF16)
        part = _dot_tn(sel, y_ref[...])

        @pl.when(first_ref[n] == 1)
        def _():
            acc[...] = part

        @pl.when(first_ref[n] == 0)
        def _():
            acc[...] += part

        @pl.when(last_ref[n] == 1)
        def _():
            o_ref[...] = _layer_norm_rows(DN_ALPHA * x_ref[...] + acc[...], g_ref[...], b_ref[...])


def _moe_combine(yg, ri, x, g, b, tb):
    nb = tb["nb_tot"]
    tt = MOE_TOK_TILE
    T = x.shape[0]
    im_t = lambda n, t, e, j, d, v, fi, la: (t[n], 0)
    return pl.pallas_call(
        _moe_combine_kernel,
        out_shape=jax.ShapeDtypeStruct((T, D_MODEL), F32),
        grid_spec=pltpu.PrefetchScalarGridSpec(
            num_scalar_prefetch=7,
            grid=(nb,),
            in_specs=[pl.BlockSpec((MOE_SUB, D_MODEL), lambda n, t, e, j, d, v, fi, la: (d[n], 0)),
                      pl.BlockSpec((1, 8, tt), lambda n, t, e, j, d, v, fi, la: (t[n], 0, 0)),
                      pl.BlockSpec((tt, D_MODEL), im_t),
                      pl.BlockSpec((1, D_MODEL), lambda n, t, e, j, d, v, fi, la: (0, 0)),
                      pl.BlockSpec((1, D_MODEL), lambda n, t, e, j, d, v, fi, la: (0, 0))],
            out_specs=pl.BlockSpec((tt, D_MODEL), im_t),
            scratch_shapes=[pltpu.VMEM((tt, D_MODEL), F32)]),
        compiler_params=_cparams(("arbitrary",)),
        name="moe_combine",
    )(tb["blk_tile"], tb["blk_e"], tb["blk_j"], tb["blk_dest"], tb["blk_valid"], tb["first"], tb["last"],
      yg, ri, x, g.reshape(1, D_MODEL), b.reshape(1, D_MODEL))


def _moe(x, w_router, w_gu_bf, w_d_bf, g, b, rows):
    T = x.shape[0]
    nt = T // MOE_TOK_TILE
    ri, rf, cnt = _router(x, w_router)
    tb = _moe_tables(cnt[:, :, 0], nt, rows // MOE_SUB)
    xg, gates = _moe_gather(x, ri, rf, tb)
    yg = _moe_ffn(xg, gates, w_gu_bf, w_d_bf, tb, rows)
    return _moe_combine(yg, ri, x, g, b, tb)


def _trunk(x, pos_rows, n_batch, mem_k, mem_v, ret_state, past_k, past_v, conv_state, gdn_state, p, cfg):
    T = x.shape[0]
    per = T // n_batch
    tm = cfg["tm"]
    ln_g, ln_b = p["ln_g"], p["ln_b"]

    tables = _rope_tables(pos_rows)
    qka, va, ga, qb, kb, vb = _proj0(x, p["w_in0"], tables, tm)
    oa, ret_new = _retention(qka, va, ga, ret_state, p["ret_norm_g"], n_batch, cfg["ret_lc"], cfg["ret_lc"])
    lp = p["diff_lambda"].astype(F32)
    lam = (jnp.exp(jnp.sum(lp[0] * lp[1])) - jnp.exp(jnp.sum(lp[2] * lp[3])) + DIFF_LAMBDA_INIT).reshape(1)
    if past_k is None:
        ob = _diff_prompt(qb, kb, vb, lam, p["diff_norm_g"], n_batch, per, cfg["diff_tq"])
    else:
        ob = _diff_step(qb, kb, vb, past_k, past_v, lam, p["diff_norm_g"], n_batch, per)
    x = _matmul_deepnorm([oa, ob], [p["w_out0"][:A_V], p["w_out0"][A_V:]], x, ln_g[0, 0], ln_b[0, 0], tm,
                         name="out0_deepnorm")
    q = _matmul(x, p["w_xq"][0:1], BF16, tm, 1024, scale=X_HD ** -0.5, name="xq0")[0]
    xa = _xattn(q, mem_k[0], mem_v[0], n_batch, cfg["x_tq"])
    x = _matmul_deepnorm([xa], [p["w_xo"][0]], x, ln_g[0, 1], ln_b[0, 1], tm, name="xo0_deepnorm")
    hmid = _swiglu_up(x, p["w_ffn_gu"], tm, cfg["ffn_tn"])
    x = _matmul_deepnorm([hmid], [p["w_ffn_d"]], x, ln_g[0, 2], ln_b[0, 2], tm, name="ffn_down_deepnorm")

    proj1 = _matmul(x, p["w_in1_main"][None], F32, tm, 1024, name="proj1")[0]
    ab = _matmul(x, p["w_in1_ab"][None], F32, tm, 128, name="proj1_gates")[0]
    qkv3 = proj1.reshape(n_batch, per, 4 * C_W)[:, :, :3 * C_W]
    conv_new = qkv3[:, per - (CONV_W - 1):, :]
    state8 = jnp.concatenate([jnp.zeros((n_batch, 8 - (CONV_W - 1), 3 * C_W), F32), conv_state.astype(F32)], axis=1)
    qkvn = _gdn_conv(proj1, state8, p["conv_w"], n_batch, cfg["conv_tm"])
    lc = cfg["gdn_lc"]
    if per < lc:
        padrows = lambda a: jnp.pad(a.reshape(n_batch, per, a.shape[-1]),
                                    ((0, 0), (0, lc - per), (0, 0))).reshape(n_batch * lc, a.shape[-1])
        og, gdn_new = _gdn(padrows(qkvn), padrows(proj1), padrows(ab), p["a_log"], p["dt_bias"],
                           p["gdn_norm_g"], gdn_state, n_batch, lc, per)
        og = og.reshape(n_batch, lc, C_W)[:, :per].reshape(T, C_W)
    else:
        og, gdn_new = _gdn(qkvn, proj1, ab, p["a_log"], p["dt_bias"], p["gdn_norm_g"], gdn_state,
                           n_batch, lc, lc)
    x = _matmul_deepnorm([og], [p["w_out1"]], x, ln_g[1, 0], ln_b[1, 0], tm, name="out1_deepnorm")
    q = _matmul(x, p["w_xq"][1:2], BF16, tm, 1024, scale=X_HD ** -0.5, name="xq1")[0]
    xa = _xattn(q, mem_k[1], mem_v[1], n_batch, cfg["x_tq"])
    x = _matmul_deepnorm([xa], [p["w_xo"][1]], x, ln_g[1, 1], ln_b[1, 1], tm, name="xo1_deepnorm")
    x = _moe(x, p["w_router"], p["w_moe_gu"], p["w_moe_d"], ln_g[1, 2], ln_b[1, 2], cfg["moe_rows"])
    return x, kb, vb, ret_new, conv_new, gdn_new


PROMPT_CFG = dict(tm=512, ret_lc=256, diff_tq=256, x_tq=512, ffn_tn=1408, conv_tm=256, gdn_lc=256, moe_rows=512)
STEP_CFG = dict(tm=512, ret_lc=16, diff_tq=16, x_tq=16, ffn_tn=1408, conv_tm=16, gdn_lc=128, moe_rows=128)


def kernel(x_prompt, x_sample, cache_diff_k, cache_diff_v, state_ret, state_gdn_conv, state_gdn, cache_mem_k, cache_mem_v, mem_prompt, w_in0, ret_norm_g, diff_lambda, diff_norm_g, w_out0, w_in1, conv_w, a_log, dt_bias, gdn_norm_g, w_out1, w_xq, w_xkv, w_xo, w_ffn_gu, w_ffn_d, w_router, w_moe_gu, w_moe_d, ln_g, ln_b):
    B, S, _ = x_prompt.shape
    DB, L, _ = x_sample.shape
    P = cache_diff_k.shape[1]
    bf = lambda w: w.astype(BF16)
    w_ab = jnp.zeros((D_MODEL, 128), F32).at[:, :2 * H_C].set(w_in1[:, 4 * C_W:])
    p = dict(w_in0=bf(w_in0), ret_norm_g=ret_norm_g, diff_lambda=diff_lambda, diff_norm_g=diff_norm_g,
             w_out0=bf(w_out0), w_in1_main=bf(w_in1[:, :4 * C_W]), w_in1_ab=bf(w_ab), conv_w=conv_w,
             a_log=a_log, dt_bias=dt_bias, gdn_norm_g=gdn_norm_g, w_out1=bf(w_out1), w_xq=bf(w_xq),
             w_xo=bf(w_xo), w_ffn_gu=bf(w_ffn_gu), w_ffn_d=bf(w_ffn_d), w_router=w_router,
             w_moe_gu=bf(w_moe_gu), w_moe_d=bf(w_moe_d), ln_g=ln_g, ln_b=ln_b)

    kv = _matmul(mem_prompt.reshape(B * N_MEM, D_MODEL), bf(w_xkv), F32, 512, 1024, name="mem_kv")
    mk_p = kv[:, :, :D_MODEL]
    mv_p = kv[:, :, D_MODEL:]

    y_p, dk_p, dv_p, ret_p, conv_p, gdn_p = _trunk(
        x_prompt.reshape(B * S, D_MODEL), jnp.arange(S), B, mk_p, mv_p,
        jnp.zeros((B, H_A, DK_A, DV_A), F32), None, None,
        jnp.zeros((B, CONV_W - 1, 3 * C_W), F32), jnp.zeros((B, H_C, DK_C, DV_C), F32), p, PROMPT_CFG)

    pos_s = jnp.tile(P + jnp.arange(L), DB)
    y_s, dk_s, dv_s, ret_s, conv_s, gdn_s = _trunk(
        x_sample.reshape(DB * L, D_MODEL), pos_s, DB,
        cache_mem_k.reshape(DEPTH, DB * N_MEM, D_MODEL), cache_mem_v.reshape(DEPTH, DB * N_MEM, D_MODEL),
        state_ret.astype(F32), cache_diff_k.reshape(DB * P, B_QK), cache_diff_v.reshape(DB * P, B_V),
        state_gdn_conv, state_gdn.astype(F32), p, STEP_CFG)

    shape5 = (DEPTH, B, N_MEM, X_HEADS, X_HD)
    return (y_p.reshape(B, S, D_MODEL), y_s.reshape(DB, L, D_MODEL),
            mk_p.reshape(shape5), mv_p.reshape(shape5),
            dk_p.reshape(B, S, H_B, 2 * DK_B), dv_p.reshape(B, S, H_B, DV_B),
            ret_p, conv_p, gdn_p,
            dk_s.reshape(DB, L, H_B, 2 * DK_B), dv_s.reshape(DB, L, H_B, DV_B),
            ret_s, conv_s, gdn_s)
```

```python
import functools
import math

import jax
import jax.numpy as jnp
from jax import lax
from jax.experimental import pallas as pl
from jax.experimental.pallas import tpu as pltpu

F32 = jnp.float32
BF16 = jnp.bfloat16
I32 = jnp.int32

D_MODEL = 1024
DEPTH = 2
CHUNK = 64
H_A, DK_A, DV_A = 4, 64, 128
RET_THETA = 10000.0
H_B, DK_B = 4, 64
DV_B = 2 * DK_B
ROT_B = DK_B // 4
ROPE_THETA = 500000.0
DIFF_LAMBDA_INIT = 0.8 - 0.6 * math.exp(-0.3 * 0)
H_C, DK_C, DV_C = 8, 128, 128
C_W = H_C * DV_C
CONV_W = 4
N_MEM = 256
X_HEADS = 4
X_HD = D_MODEL // X_HEADS
D_FF = 2816
N_EXPERTS = 8
D_FF_E = 3584
DN_ALPHA = (2 * DEPTH) ** 0.25
LN_EPS = 1e-5
A_QK = H_A * DK_A
A_V = H_A * DV_A
B_QK = H_B * 2 * DK_B
B_V = H_B * DV_B
PROJ0 = 2 * A_QK + 2 * A_V + 2 * B_QK + B_V

VMEM_LIMIT_V7X = 52 * 1024 * 1024
NEG_BIG = -1e30

MOE_TOK_TILE = 512
MOE_SUB = 64
MOE_FF_BLK = 512
MOE_ROW_SUB = 256
MOE_SLOTS = 2 * MOE_TOK_TILE // MOE_SUB + N_EXPERTS
GDN_HEAD_GROUP = 4


def _cparams(sem):
    return pltpu.CompilerParams(dimension_semantics=sem, vmem_limit_bytes=VMEM_LIMIT_V7X)


def _dot(a, b):
    return jnp.dot(a, b, preferred_element_type=F32)


def _dot_nt(a, b):
    return lax.dot_general(a, b, (((1,), (1,)), ((), ())), preferred_element_type=F32)


def _dot_tn(a, b):
    return lax.dot_general(a, b, (((0,), (0,)), ((), ())), preferred_element_type=F32)


def _silu(x):
    return x * (1.0 / (1.0 + jnp.exp(-x)))


def _layer_norm_rows(y, g, b):
    mu = jnp.mean(y, axis=-1, keepdims=True)
    d = y - mu
    var = jnp.mean(d * d, axis=-1, keepdims=True)
    return d * lax.rsqrt(var + LN_EPS) * g + b


def _mm_kernel(x_ref, w_ref, o_ref, *, scale):
    acc = _dot(x_ref[...].astype(BF16), w_ref[0])
    if scale != 1.0:
        acc = acc * scale
    o_ref[0] = acc.astype(o_ref.dtype)


def _matmul(x, w, out_dtype, tm, tn, scale=1.0, name="matmul"):
    M, K = x.shape
    G, _, N = w.shape
    tm = min(tm, M)
    tn = min(tn, N)
    assert M % tm == 0 and N % tn == 0
    return pl.pallas_call(
        functools.partial(_mm_kernel, scale=scale),
        out_shape=jax.ShapeDtypeStruct((G, M, N), out_dtype),
        grid=(G, M // tm, N // tn),
        in_specs=[pl.BlockSpec((tm, K), lambda g, i, j: (i, 0)),
                  pl.BlockSpec((1, K, tn), lambda g, i, j: (g, 0, j))],
        out_specs=pl.BlockSpec((1, tm, tn), lambda g, i, j: (g, i, j)),
        compiler_params=_cparams(("parallel", "parallel", "arbitrary")),
        name=name,
    )(x, w)


def _mm_dn_kernel(*refs, n_in):
    xs = refs[:n_in]
    ws = refs[n_in:2 * n_in]
    r_ref, g_ref, b_ref, o_ref, ob_ref = refs[2 * n_in:]
    acc = DN_ALPHA * r_ref[...]
    for x_ref, w_ref in zip(xs, ws):
        acc = acc + _dot(x_ref[...].astype(BF16), w_ref[...])
    y = _layer_norm_rows(acc, g_ref[...], b_ref[...])
    o_ref[...] = y
    ob_ref[...] = y.astype(BF16)


def _matmul_deepnorm(xs, ws, resid, g, b, tm, name="matmul_deepnorm"):
    M = resid.shape[0]
    tm = min(tm, M)
    assert M % tm == 0
    n_in = len(xs)
    in_specs = [pl.BlockSpec((tm, x.shape[1]), lambda i: (i, 0)) for x in xs]
    in_specs += [pl.BlockSpec(w.shape, lambda i: (0, 0)) for w in ws]
    in_specs += [pl.BlockSpec((tm, D_MODEL), lambda i: (i, 0)),
                 pl.BlockSpec((1, D_MODEL), lambda i: (0, 0)),
                 pl.BlockSpec((1, D_MODEL), lambda i: (0, 0))]
    return pl.pallas_call(
        functools.partial(_mm_dn_kernel, n_in=n_in),
        out_shape=[jax.ShapeDtypeStruct((M, D_MODEL), F32), jax.ShapeDtypeStruct((M, D_MODEL), BF16)],
        grid=(M // tm,),
        in_specs=in_specs,
        out_specs=[pl.BlockSpec((tm, D_MODEL), lambda i: (i, 0))] * 2,
        compiler_params=_cparams(("parallel",)),
        name=name,
    )(*xs, *ws, resid, g.reshape(1, D_MODEL), b.reshape(1, D_MODEL))


def _swap_halves(x, group, half):
    n = x.shape[-1]
    lane = lax.broadcasted_iota(I32, x.shape, x.ndim - 1) % group
    up = pltpu.roll(x, n - half, x.ndim - 1)
    dn = pltpu.roll(x, half, x.ndim - 1)
    return jnp.where(lane < half, up, dn)


def _proj_plain_kernel(x_ref, w_ref, o_ref):
    o_ref[...] = _dot(x_ref[...], w_ref[...]).astype(o_ref.dtype)


def _proj_rot_kernel(x_ref, w_ref, c_ref, s_ref, o_ref, *, group, half):
    acc = _dot(x_ref[...], w_ref[...])
    o_ref[...] = (acc * c_ref[...] + _swap_halves(acc, group, half) * s_ref[...]).astype(o_ref.dtype)


def _proj0_block(xb, w, out_dtype, tm, rot=None, name="proj0"):
    T = xb.shape[0]
    blk = w.shape[1]
    tm = min(tm, T)
    row = lambda i: (i, 0)
    in_specs = [pl.BlockSpec((tm, D_MODEL), row), pl.BlockSpec((D_MODEL, blk), lambda i: (0, 0))]
    args = [xb, w]
    if rot is None:
        body = _proj_plain_kernel
    else:
        cos, sin, group, half = rot
        P = cos.shape[0]
        tm = min(tm, P)
        assert P % tm == 0
        npb = P // tm
        in_specs[0] = pl.BlockSpec((tm, D_MODEL), row)
        in_specs += [pl.BlockSpec((tm, blk), lambda i: (i % npb, 0))] * 2
        args += [cos, sin]
        body = functools.partial(_proj_rot_kernel, group=group, half=half)
    assert T % tm == 0
    return pl.pallas_call(
        body,
        out_shape=jax.ShapeDtypeStruct((T, blk), out_dtype),
        grid=(T // tm,),
        in_specs=in_specs,
        out_specs=pl.BlockSpec((tm, blk), row),
        compiler_params=_cparams(("parallel",)),
        name=name,
    )(*args)


def _rope_tables(pos):
    pos = pos.astype(F32)
    inv = RET_THETA ** (-jnp.arange(0, DK_A, 2, dtype=F32) / DK_A)
    ang = pos[:, None] * inv[None, :]
    c, s = jnp.cos(ang), jnp.sin(ang)
    c64 = jnp.concatenate([c, c], axis=-1)
    s64 = jnp.concatenate([-s, s], axis=-1)
    kscale = DK_A ** -0.5
    ca = jnp.concatenate([jnp.tile(c64, (1, H_A)), jnp.tile(c64, (1, H_A)) * kscale], axis=-1)
    sa = jnp.concatenate([jnp.tile(s64, (1, H_A)), jnp.tile(s64, (1, H_A)) * kscale], axis=-1)
    invb = ROPE_THETA ** (-jnp.arange(0, ROT_B, 2, dtype=F32) / ROT_B)
    angb = pos[:, None] * invb[None, :]
    cb, sb = jnp.cos(angb), jnp.sin(angb)
    rest = DK_B - ROT_B
    c64b = jnp.concatenate([cb, cb, jnp.ones((pos.shape[0], rest), F32)], axis=-1)
    s64b = jnp.concatenate([-sb, sb, jnp.zeros((pos.shape[0], rest), F32)], axis=-1)
    ck = jnp.tile(c64b, (1, 2 * H_B))
    sk = jnp.tile(s64b, (1, 2 * H_B))
    qscale = DK_B ** -0.5
    return ca, sa, ck * qscale, sk * qscale, ck, sk


def _proj0(xb, w_in0_bf, tables, tm):
    ca, sa, cq, sq, ck, sk = tables
    blk = 512
    assert PROJ0 == 6 * blk
    w = [w_in0_bf[:, j * blk:(j + 1) * blk] for j in range(6)]
    rot_a = (DK_A, DK_A // 2)
    rot_b = (DK_B, ROT_B // 2)
    qka = _proj0_block(xb, w[0], BF16, tm, (ca, sa) + rot_a, name="proj0_qk_ret")
    va = _proj0_block(xb, w[1], BF16, tm, name="proj0_v_ret")
    ga = _proj0_block(xb, w[2], F32, tm, name="proj0_gate_ret")
    qb = _proj0_block(xb, w[3], BF16, tm, (cq, sq) + rot_b, name="proj0_q_diff")
    kb = _proj0_block(xb, w[4], F32, tm, (ck, sk) + rot_b, name="proj0_k_diff")
    vb = _proj0_block(xb, w[5], F32, tm, name="proj0_v_diff")
    return qka, va, ga, qb, kb, vb


def _retention_kernel(qk_ref, v_ref, g_ref, dmat_ref, qdec_ref, kdec_ref, sdec_ref, s0_ref, ng_ref,
                      o_ref, sout_ref, s_scr):
    c = pl.program_id(1)

    @pl.when(c == 0)
    def _():
        s_scr[...] = s0_ref[0]

    for h in range(H_A):
        q = qk_ref[:, h * DK_A:(h + 1) * DK_A]
        k = qk_ref[:, A_QK + h * DK_A:A_QK + (h + 1) * DK_A]
        v = v_ref[:, h * DV_A:(h + 1) * DV_A]
        s = s_scr[h]
        scores = _dot_nt(q, k) * dmat_ref[h]
        o = _dot(scores.astype(BF16), v) + _dot(q, s.astype(BF16)) * qdec_ref[h]
        kd = (k.astype(F32) * kdec_ref[h]).astype(BF16)
        s_scr[h] = s * sdec_ref[h] + _dot_tn(kd, v)
        mu = jnp.mean(o, axis=-1, keepdims=True)
        d = o - mu
        var = jnp.mean(d * d, axis=-1, keepdims=True)
        gate = g_ref[:, h * DV_A:(h + 1) * DV_A]
        y = d * lax.rsqrt(var + LN_EPS) * ng_ref[:, h * DV_A:(h + 1) * DV_A] * _silu(gate)
        o_ref[:, h * DV_A:(h + 1) * DV_A] = y.astype(o_ref.dtype)

    @pl.when(c == pl.num_programs(1) - 1)
    def _():
        sout_ref[0] = s_scr[...]


def _retention_tables(lc):
    lg = jnp.log1p(-jnp.exp2(-5.0 - jnp.arange(H_A, dtype=F32)))
    t = jnp.arange(lc, dtype=F32)
    rel = t[:, None] - t[None, :]
    causal = rel >= 0
    dmat = jnp.where(causal, jnp.exp(lg[:, None, None] * jnp.where(causal, rel, 0.0)), 0.0)
    qdec = jnp.exp(lg[:, None] * (t[None, :] + 1.0))
    kdec = jnp.exp(lg[:, None] * (lc - 1.0 - t[None, :]))
    sdec = jnp.exp(lg * lc)
    return (dmat,
            jnp.broadcast_to(qdec[:, :, None], (H_A, lc, DV_A)),
            jnp.broadcast_to(kdec[:, :, None], (H_A, lc, DK_A)),
            jnp.broadcast_to(sdec[:, None, None], (H_A, DK_A, DV_A)))


def _retention(qka, va, ga, state0, ret_norm_g, n_batch, lc):
    T = qka.shape[0]
    nc = T // (n_batch * lc)
    assert nc * n_batch * lc == T
    dmat, qdec, kdec, sdec = _retention_tables(lc)
    row = lambda b, c: (b * nc + c, 0)
    const3 = lambda b, c: (0, 0, 0)
    return pl.pallas_call(
        _retention_kernel,
        out_shape=[jax.ShapeDtypeStruct((T, A_V), BF16),
                   jax.ShapeDtypeStruct((n_batch, H_A, DK_A, DV_A), F32)],
        grid=(n_batch, nc),
        in_specs=[pl.BlockSpec((lc, 2 * A_QK), row),
                  pl.BlockSpec((lc, A_V), row),
                  pl.BlockSpec((lc, A_V), row),
                  pl.BlockSpec((H_A, lc, lc), const3),
                  pl.BlockSpec((H_A, lc, DV_A), const3),
                  pl.BlockSpec((H_A, lc, DK_A), const3),
                  pl.BlockSpec((H_A, DK_A, DV_A), const3),
                  pl.BlockSpec((1, H_A, DK_A, DV_A), lambda b, c: (b, 0, 0, 0)),
                  pl.BlockSpec((1, A_V), lambda b, c: (0, 0))],
        out_specs=[pl.BlockSpec((lc, A_V), row),
                   pl.BlockSpec((1, H_A, DK_A, DV_A), lambda b, c: (b, 0, 0, 0))],
        scratch_shapes=[pltpu.VMEM((H_A, DK_A, DV_A), F32)],
        compiler_params=_cparams(("parallel", "arbitrary")),
        name="retention",
    )(qka, va, ga, dmat, qdec, kdec, sdec, state0, ret_norm_g.reshape(1, A_V))


def _diff_finish(acc1, l1, acc2, l2, lam, ng):
    o = acc1 * (1.0 / l1) - lam * (acc2 * (1.0 / l2))
    ms = jnp.mean(o * o, axis=-1, keepdims=True)
    return o * lax.rsqrt(ms + LN_EPS) * ng * (1.0 - DIFF_LAMBDA_INIT)


def _diff_prompt_kernel(lam_ref, q_ref, k_ref, v_ref, ng_ref, o_ref, kbf, vbf, *, tq):
    qi = pl.program_id(2)

    @pl.when(qi == 0)
    def _():
        kbf[...] = k_ref[...].astype(BF16)
        vbf[...] = v_ref[...].astype(BF16)

    q1 = q_ref[:, :DK_B]
    q2 = q_ref[:, DK_B:]

    def block(kblk, carry, masked):
        m1, l1, a1, m2, l2, a2 = carry
        off = pl.multiple_of(kblk * tq, tq)
        kk = kbf[pl.ds(off, tq), :]
        vv = vbf[pl.ds(off, tq), :]
        s1 = _dot_nt(q1, kk[:, :DK_B])
        s2 = _dot_nt(q2, kk[:, DK_B:])
        if masked:
            rc = lax.broadcasted_iota(I32, (tq, tq), 0) // CHUNK
            cc = lax.broadcasted_iota(I32, (tq, tq), 1) // CHUNK
            vis = cc <= rc
            s1 = jnp.where(vis, s1, NEG_BIG)
            s2 = jnp.where(vis, s2, NEG_BIG)
        out = []
        for s, m, l, a in ((s1, m1, l1, a1), (s2, m2, l2, a2)):
            mn = jnp.maximum(m, jnp.max(s, axis=-1, keepdims=True))
            alpha = jnp.exp(m - mn)
            p = jnp.exp(s - mn)
            l = alpha * l + jnp.sum(p, axis=-1, keepdims=True)
            a = alpha * a + _dot(p.astype(BF16), vv)
            out += [mn, l, a]
        return tuple(out)

    init = (jnp.full((tq, 1), NEG_BIG, F32), jnp.zeros((tq, 1), F32), jnp.zeros((tq, DV_B), F32)) * 2
    carry = lax.fori_loop(0, qi, lambda kb, c: block(kb, c, False), init)
    m1, l1, a1, m2, l2, a2 = block(qi, carry, True)
    o_ref[...] = _diff_finish(a1, l1, a2, l2, lam_ref[0], ng_ref[...]).astype(o_ref.dtype)


def _diff_prompt(qb, kb, vb, lam, diff_norm_g, n_batch, seq, tq):
    T = qb.shape[0]
    assert seq % tq == 0 and tq % CHUNK == 0
    nq = seq // tq
    return pl.pallas_call(
        functools.partial(_diff_prompt_kernel, tq=tq),
        out_shape=jax.ShapeDtypeStruct((T, B_V), BF16),
        grid_spec=pltpu.PrefetchScalarGridSpec(
            num_scalar_prefetch=1,
            grid=(n_batch, H_B, nq),
            in_specs=[pl.BlockSpec((tq, DV_B), lambda b, h, i, lam: (b * nq + i, h)),
                      pl.BlockSpec((seq, DV_B), lambda b, h, i, lam: (b, h)),
                      pl.BlockSpec((seq, DV_B), lambda b, h, i, lam: (b, h)),
                      pl.BlockSpec((1, DV_B), lambda b, h, i, lam: (0, 0))],
            out_specs=pl.BlockSpec((tq, DV_B), lambda b, h, i, lam: (b * nq + i, h)),
            scratch_shapes=[pltpu.VMEM((seq, DV_B), BF16), pltpu.VMEM((seq, DV_B), BF16)]),
        compiler_params=_cparams(("parallel", "parallel", "arbitrary")),
        name="diff_attn_prompt",
    )(lam, qb, kb, vb, diff_norm_g.reshape(1, DV_B))


def _diff_step_kernel(lam_ref, q_ref, kp_ref, vp_ref, kn_ref, vn_ref, ng_ref, o_ref, *, past, ln):
    q = q_ref[...]
    kp = kp_ref[...].astype(BF16)
    vp = vp_ref[...].astype(BF16)
    kn = kn_ref[...].astype(BF16)
    vn = vn_ref[...].astype(BF16)
    qchunk = (past + lax.broadcasted_iota(I32, (ln, 1), 0)) // CHUNK
    vis_p = (lax.broadcasted_iota(I32, (ln, past), 1) // CHUNK) <= qchunk
    vis_n = ((past + lax.broadcasted_iota(I32, (ln, ln), 1)) // CHUNK) <= qchunk
    accs = []
    for c in range(2):
        sl = slice(c * DK_B, (c + 1) * DK_B)
        sp = jnp.where(vis_p, _dot_nt(q[:, sl], kp[:, sl]), NEG_BIG)
        sn = jnp.where(vis_n, _dot_nt(q[:, sl], kn[:, sl]), NEG_BIG)
        m = jnp.maximum(jnp.max(sp, axis=-1, keepdims=True), jnp.max(sn, axis=-1, keepdims=True))
        pp = jnp.exp(sp - m)
        pn = jnp.exp(sn - m)
        l = jnp.sum(pp, axis=-1, keepdims=True) + jnp.sum(pn, axis=-1, keepdims=True)
        accs += [_dot(pp.astype(BF16), vp) + _dot(pn.astype(BF16), vn), l]
    o_ref[...] = _diff_finish(accs[0], accs[1], accs[2], accs[3], lam_ref[0], ng_ref[...]).astype(o_ref.dtype)


def _diff_step(qb, kb, vb, past_k, past_v, lam, diff_norm_g, n_batch, ln):
    past = past_k.shape[0] // n_batch
    blk = lambda b, h, lam: (b, h)
    return pl.pallas_call(
        functools.partial(_diff_step_kernel, past=past, ln=ln),
        out_shape=jax.ShapeDtypeStruct((n_batch * ln, B_V), BF16),
        grid_spec=pltpu.PrefetchScalarGridSpec(
            num_scalar_prefetch=1,
            grid=(n_batch, H_B),
            in_specs=[pl.BlockSpec((ln, DV_B), blk),
                      pl.BlockSpec((past, DV_B), blk),
                      pl.BlockSpec((past, DV_B), blk),
                      pl.BlockSpec((ln, DV_B), blk),
                      pl.BlockSpec((ln, DV_B), blk),
                      pl.BlockSpec((1, DV_B), lambda b, h, lam: (0, 0))],
            out_specs=pl.BlockSpec((ln, DV_B), blk)),
        compiler_params=_cparams(("parallel", "parallel")),
        name="diff_attn_step",
    )(lam, qb, past_k, past_v, kb, vb, diff_norm_g.reshape(1, DV_B))


def _xattn_kernel(q_ref, mk_ref, mv_ref, o_ref):
    for h in range(X_HEADS):
        sl = slice(h * X_HD, (h + 1) * X_HD)
        s = _dot_nt(q_ref[:, sl], mk_ref[:, sl].astype(BF16))
        p = jnp.exp(s - jnp.max(s, axis=-1, keepdims=True))
        l = jnp.sum(p, axis=-1, keepdims=True)
        o = _dot(p.astype(BF16), mv_ref[:, sl].astype(BF16)) * (1.0 / l)
        o_ref[:, sl] = o.astype(o_ref.dtype)


def _xattn(q, mk, mv, n_batch, tq):
    T = q.shape[0]
    per = T // n_batch
    tq = min(tq, per)
    nt = per // tq
    return pl.pallas_call(
        _xattn_kernel,
        out_shape=jax.ShapeDtypeStruct((T, D_MODEL), BF16),
        grid=(n_batch, nt),
        in_specs=[pl.BlockSpec((tq, D_MODEL), lambda b, t: (b * nt + t, 0)),
                  pl.BlockSpec((N_MEM, D_MODEL), lambda b, t: (b, 0)),
                  pl.BlockSpec((N_MEM, D_MODEL), lambda b, t: (b, 0))],
        out_specs=pl.BlockSpec((tq, D_MODEL), lambda b, t: (b * nt + t, 0)),
        compiler_params=_cparams(("parallel", "arbitrary")),
        name="mem_xattn",
    )(q, mk, mv)


def _swiglu_up_kernel(x_ref, wg_ref, wu_ref, o_ref):
    xb = x_ref[...]
    g = _dot(xb, wg_ref[...])
    u = _dot(xb, wu_ref[...])
    o_ref[...] = (_silu(g) * u).astype(o_ref.dtype)


def _swiglu_up(xb, w_gu_bf, tm, tn):
    T = xb.shape[0]
    tm = min(tm, T)
    nj = D_FF // tn
    assert D_FF % tn == 0
    return pl.pallas_call(
        _swiglu_up_kernel,
        out_shape=jax.ShapeDtypeStruct((T, D_FF), BF16),
        grid=(T // tm, nj),
        in_specs=[pl.BlockSpec((tm, D_MODEL), lambda i, j: (i, 0)),
                  pl.BlockSpec((D_MODEL, tn), lambda i, j: (0, j)),
                  pl.BlockSpec((D_MODEL, tn), lambda i, j: (0, nj + j))],
        out_specs=pl.BlockSpec((tm, tn), lambda i, j: (i, j)),
        compiler_params=_cparams(("parallel", "arbitrary")),
        name="swiglu_up",
    )(xb, w_gu_bf, w_gu_bf)


def _conv_kernel(x_ref, prev_ref, st_ref, w_ref, o_ref, buf, *, tm):
    t = pl.program_id(1)
    cb = pl.program_id(2)
    buf[0:8, :] = jnp.where(t == 0, st_ref[0], prev_ref[...])
    buf[8:8 + tm, :] = x_ref[...]
    y = buf[8:8 + tm, :] * w_ref[3:4, :]
    for i in range(CONV_W - 1):
        y = y + buf[5 + i:5 + i + tm, :] * w_ref[i:i + 1, :]
    y = _silu(y)

    @pl.when(cb < 2)
    def _():
        scale = jnp.where(cb == 0, DK_C ** -0.5, 1.0)
        for h in range(H_C):
            yh = y[:, h * DK_C:(h + 1) * DK_C]
            ss = jnp.sum(yh * yh, axis=-1, keepdims=True)
            o_ref[:, h * DK_C:(h + 1) * DK_C] = (yh * (lax.rsqrt(ss + 1e-6) * scale)).astype(o_ref.dtype)

    @pl.when(cb == 2)
    def _():
        o_ref[...] = y.astype(o_ref.dtype)


def _gdn_conv(proj1, conv_state8, conv_w, n_batch, tm):
    T = proj1.shape[0]
    per = T // n_batch
    tm = min(tm, per)
    nt = per // tm
    assert per % tm == 0 and tm % 8 == 0
    return pl.pallas_call(
        functools.partial(_conv_kernel, tm=tm),
        out_shape=jax.ShapeDtypeStruct((T, 3 * C_W), BF16),
        grid=(n_batch, nt, 3),
        in_specs=[pl.BlockSpec((tm, C_W), lambda b, t, c: (b * nt + t, c)),
                  pl.BlockSpec((8, C_W), lambda b, t, c: (jnp.maximum((b * nt + t) * (tm // 8) - 1, 0), c)),
                  pl.BlockSpec((1, 8, C_W), lambda b, t, c: (b, 0, c)),
                  pl.BlockSpec((CONV_W, C_W), lambda b, t, c: (0, c))],
        out_specs=pl.BlockSpec((tm, C_W), lambda b, t, c: (b * nt + t, c)),
        scratch_shapes=[pltpu.VMEM((8 + tm, C_W), F32)],
        compiler_params=_cparams(("parallel", "arbitrary", "arbitrary")),
        name="gdn_conv",
    )(proj1, proj1, conv_state8, conv_w)


def _gdn_kernel(q_ref, k_ref, v_ref, z_ref, ab_ref, alog_ref, dtb_ref, ng_ref, s0_ref,
                o_ref, sout_ref, s_scr, *, lc, l_real):
    c = pl.program_id(1)

    @pl.when(c == 0)
    def _():
        s_scr[...] = s0_ref[0]

    ri = lax.broadcasted_iota(I32, (lc, lc), 0)
    ci = lax.broadcasted_iota(I32, (lc, lc), 1)
    incl = ci <= ri
    strict = ci < ri
    eye = (ci == ri).astype(F32)
    ab = ab_ref[...]
    sp = jnp.maximum(ab + dtb_ref[...], 0.0) + jnp.log1p(jnp.exp(-jnp.abs(ab + dtb_ref[...])))
    glog = -jnp.exp(alog_ref[...]) * sp
    if l_real < lc:
        live = lax.broadcasted_iota(I32, (lc, 1), 0) < l_real
        glog = jnp.where(live, glog, 0.0)
    gcum = jnp.dot(incl.astype(F32), glog, preferred_element_type=F32, precision=lax.Precision.HIGHEST)
    gcum_t = gcum.T
    beta_all = 1.0 / (1.0 + jnp.exp(-ab))

    levels = []
    s = 1
    while s < lc:
        levels.append(((ri // (2 * s)) == (ci // (2 * s))) & (((ri // s) % 2) == 1) & (((ci // s) % 2) == 0))
        s *= 2

    for h0 in range(0, H_C, GDN_HEAD_GROUP):
        heads = range(h0, h0 + GDN_HEAD_GROUP)
        sls = [slice(h * DK_C, (h + 1) * DK_C) for h in heads]
        ks, kfs, vs, gcols, decays, egs, kbs, a_lows = [], [], [], [], [], [], [], []
        for h, sl in zip(heads, sls):
            k = k_ref[:, sl]
            v = v_ref[:, sl].astype(F32)
            if l_real < lc:
                k = jnp.where(live, k, jnp.zeros_like(k))
                v = jnp.where(live, v, 0.0)
            kf = k.astype(F32)
            gcol = gcum[:, h:h + 1]
            grow = gcum_t[h:h + 1, :]
            beta = beta_all[:, H_C + h:H_C + h + 1]
            decay = jnp.exp(jnp.where(incl, gcol - grow, NEG_BIG))
            kb = kf * beta
            ks.append(k)
            kfs.append(kf)
            gcols.append(gcol)
            decays.append(decay)
            egs.append(jnp.exp(gcol))
            kbs.append(kb)
            vs.append(v * beta)
            a_lows.append(jnp.where(strict, _dot_nt(kb.astype(BF16), k) * decay, 0.0))
        tinvs = [eye - jnp.where(levels[0], a, 0.0) for a in a_lows]
        for off in levels[1:]:
            a_offs = [jnp.where(off, a, 0.0).astype(BF16) for a in a_lows]
            tbs = [t.astype(BF16) for t in tinvs]
            ws = [_dot(a, t).astype(BF16) for a, t in zip(a_offs, tbs)]
            tinvs = [t - _dot(tb, w) for t, tb, w in zip(tinvs, tbs, ws)]
        rhss = [jnp.concatenate([vb, kb * eg], axis=-1).astype(BF16) for vb, kb, eg in zip(vs, kbs, egs)]
        sols = [_dot(t.astype(BF16), r) for t, r in zip(tinvs, rhss)]
        sts = [s_scr[h] for h in heads]
        stbs = [st.astype(BF16) for st in sts]
        ubs = [(sol[:, :DV_C] - _dot(sol[:, DV_C:].astype(BF16), stb)).astype(BF16) for sol, stb in zip(sols, stbs)]
        qs = [q_ref[:, sl] for sl in sls]
        qks = [(_dot_nt(q, k) * decay).astype(BF16) for q, k, decay in zip(qs, ks, decays)]
        os_ = [_dot((q.astype(F32) * eg).astype(BF16), stb) + _dot(qk, ub)
               for q, eg, stb, qk, ub in zip(qs, egs, stbs, qks, ubs)]
        for h, st, kf, gcol, ub in zip(heads, sts, kfs, gcols, ubs):
            g_last = gcol[lc - 1:lc, :]
            kdec = (kf * jnp.exp(g_last - gcol)).astype(BF16)
            s_scr[h] = st * jnp.exp(g_last) + _dot_tn(kdec, ub)
        for sl, o in zip(sls, os_):
            ms = jnp.mean(o * o, axis=-1, keepdims=True)
            y = o * lax.rsqrt(ms + LN_EPS) * ng_ref[...] * _silu(z_ref[:, sl])
            o_ref[:, sl] = y.astype(o_ref.dtype)

    @pl.when(c == pl.num_programs(1) - 1)
    def _():
        sout_ref[0] = s_scr[...]


def _gdn(qkvn, proj1, ab, a_log, dt_bias, gdn_norm_g, state0, n_batch, lc, l_real):
    T = qkvn.shape[0]
    nc = T // (n_batch * lc)
    assert nc * n_batch * lc == T and (nc == 1 or l_real == lc)
    pad = lambda v: jnp.pad(v.astype(F32).reshape(1, H_C), ((0, 0), (0, 128 - H_C)))
    return pl.pallas_call(
        functools.partial(_gdn_kernel, lc=lc, l_real=l_real),
        out_shape=[jax.ShapeDtypeStruct((T, C_W), BF16),
                   jax.ShapeDtypeStruct((n_batch, H_C, DK_C, DV_C), F32)],
        grid=(n_batch, nc),
        in_specs=[pl.BlockSpec((lc, C_W), lambda b, c: (b * nc + c, 0)),
                  pl.BlockSpec((lc, C_W), lambda b, c: (b * nc + c, 1)),
                  pl.BlockSpec((lc, C_W), lambda b, c: (b * nc + c, 2)),
                  pl.BlockSpec((lc, C_W), lambda b, c: (b * nc + c, 3)),
                  pl.BlockSpec((lc, 128), lambda b, c: (b * nc + c, 0)),
                  pl.BlockSpec((1, 128), lambda b, c: (0, 0)),
                  pl.BlockSpec((1, 128), lambda b, c: (0, 0)),
                  pl.BlockSpec((1, DV_C), lambda b, c: (0, 0)),
                  pl.BlockSpec((1, H_C, DK_C, DV_C), lambda b, c: (b, 0, 0, 0))],
        out_specs=[pl.BlockSpec((lc, C_W), lambda b, c: (b * nc + c, 0)),
                   pl.BlockSpec((1, H_C, DK_C, DV_C), lambda b, c: (b, 0, 0, 0))],
        scratch_shapes=[pltpu.VMEM((H_C, DK_C, DV_C), F32)],
        compiler_params=_cparams(("parallel", "arbitrary")),
        name="gated_delta",
    )(qkvn, qkvn, qkvn, proj1, ab, pad(a_log), pad(dt_bias), gdn_norm_g.reshape(1, DV_C), state0)


def _router_kernel(x_ref, wt_ref, rt_ref, rtt_ref, cnt_ref):
    tt = x_ref.shape[0]
    logits = lax.dot_general(wt_ref[...], x_ref[...], (((1,), (1,)), ((), ())),
                             preferred_element_type=F32, precision=lax.Precision.HIGHEST)
    eid = lax.broadcasted_iota(I32, (N_EXPERTS, tt), 0).astype(F32)
    m1 = jnp.max(logits, axis=0, keepdims=True)
    e1 = jnp.min(jnp.where(logits == m1, eid, float(N_EXPERTS)), axis=0, keepdims=True)
    rest = jnp.where(eid == e1, -jnp.inf, logits)
    m2 = jnp.max(rest, axis=0, keepdims=True)
    e2 = jnp.min(jnp.where(rest == m2, eid, float(N_EXPERTS)), axis=0, keepdims=True)
    ev = jnp.exp(m2 - m1)
    g1 = 1.0 / (1.0 + ev)
    g2 = ev / (1.0 + ev)
    hit1 = eid == e1
    hit2 = eid == e2
    member = jnp.where(hit1 | hit2, 1.0, 0.0)
    before = (lax.broadcasted_iota(I32, (tt, tt), 0) < lax.broadcasted_iota(I32, (tt, tt), 1))
    rank = _dot(member.astype(BF16), jnp.where(before, 1.0, 0.0).astype(BF16))
    r1 = jnp.sum(jnp.where(hit1, rank, 0.0), axis=0, keepdims=True)
    r2 = jnp.sum(jnp.where(hit2, rank, 0.0), axis=0, keepdims=True)
    rows = jnp.concatenate([e1, e2, r1, r2, g1, g2, jnp.zeros((2, tt), F32)], axis=0)
    rt_ref[0] = rows
    rtt_ref[0] = jnp.concatenate([rows, jnp.zeros((128 - 8, tt), F32)], axis=0).T
    cnt = jnp.sum(member, axis=1, keepdims=True).astype(I32)
    cnt_ref[0] = jnp.broadcast_to(cnt, (N_EXPERTS, 128))


def _router(x, w_router):
    T = x.shape[0]
    tt = MOE_TOK_TILE
    nt = T // tt
    assert T % tt == 0
    return pl.pallas_call(
        _router_kernel,
        out_shape=[jax.ShapeDtypeStruct((nt, 8, tt), F32),
                   jax.ShapeDtypeStruct((nt, tt, 128), F32),
                   jax.ShapeDtypeStruct((nt, N_EXPERTS, 128), I32)],
        grid=(nt,),
        in_specs=[pl.BlockSpec((tt, D_MODEL), lambda i: (i, 0)),
                  pl.BlockSpec((N_EXPERTS, D_MODEL), lambda i: (0, 0))],
        out_specs=[pl.BlockSpec((1, 8, tt), lambda i: (i, 0, 0)),
                   pl.BlockSpec((1, tt, 128), lambda i: (i, 0, 0)),
                   pl.BlockSpec((1, N_EXPERTS, 128), lambda i: (i, 0, 0))],
        compiler_params=_cparams(("parallel",)),
        name="moe_router",
    )(x, w_router.T)


def _moe_tables(cnt, n_tiles, group):
    E = N_EXPERTS
    nb_max = (2 * n_tiles * MOE_TOK_TILE) // MOE_SUB + n_tiles * E
    nb_tot = -(-(nb_max + E * (group - 1)) // group) * group
    nblk = (cnt + MOE_SUB - 1) // MOE_SUB
    nbe = jnp.sum(nblk, axis=0)
    nbe_pad = (nbe + group - 1) // group * group
    ends_e = jnp.cumsum(nbe_pad)
    base = ends_e - nbe_pad
    dest0 = base[None, :] + jnp.cumsum(nblk, axis=0) - nblk
    ends_t = jnp.cumsum(nblk, axis=1)
    kk = jnp.arange(MOE_SLOTS, dtype=I32)
    slot_e = jnp.minimum(jnp.sum(ends_t[:, None, :] <= kk[None, :, None], axis=-1), E - 1).astype(I32)
    slot_valid = kk[None, :] < ends_t[:, -1:]
    start_t = jnp.take_along_axis(ends_t - nblk, slot_e, axis=1)
    slot_j = jnp.where(slot_valid, kk[None, :] - start_t, 0)
    slot_dest = jnp.where(slot_valid, jnp.take_along_axis(dest0, slot_e, axis=1) + slot_j, 0)
    npad = nbe_pad - nbe
    pm = jnp.arange(E * (group - 1), dtype=I32)
    pe, pt = pm // max(group - 1, 1), pm % max(group - 1, 1)
    pad_valid = pt < npad[pe]
    pad_dest = jnp.where(pad_valid, base[pe] + nbe[pe] + pt, 0)
    n_rt = nb_tot // group
    r0 = jnp.arange(n_rt, dtype=I32) * group
    rt_valid = r0 < ends_e[-1]
    rt_e = jnp.minimum(jnp.sum(ends_e[None, :] <= r0[:, None], axis=-1), E - 1)
    last_e = jnp.minimum(jnp.sum(ends_e <= ends_e[-1] - 1), E - 1)
    rt_e = jnp.where(rt_valid, rt_e, last_e)
    i32 = lambda a: a.astype(I32).reshape(-1)
    return dict(nb_tot=nb_tot, n_rt=n_rt, slot_e=i32(slot_e), slot_j=i32(slot_j), slot_dest=i32(slot_dest),
                slot_valid=i32(slot_valid), pad_dest=i32(pad_dest), pad_valid=i32(pad_valid),
                rt_e=i32(rt_e), rt_valid=i32(rt_valid), n_valid_rt=i32(ends_e[-1] // group))


def _moe_gather_kernel(se_ref, sj_ref, sd_ref, sv_ref, pd_ref, pv_ref, nv_ref, x_ref, rt_ref, xg_hbm,
                       buf, zbuf, sem, zsem, *, n_pad, n_rt, rows):
    i = pl.program_id(0)
    rt = rt_ref[0]
    height = MOE_SLOTS * MOE_SUB
    row = lax.broadcasted_iota(I32, (height, 1), 0)
    row_e = jnp.full((height, 1), -1.0, F32)
    row_r = (row % MOE_SUB).astype(F32)
    for k in range(MOE_SLOTS):
        idx = i * MOE_SLOTS + k
        here = (row // MOE_SUB) == k
        row_e = jnp.where(here, jnp.where(sv_ref[idx] == 1, se_ref[idx], -1).astype(F32), row_e)
        row_r = row_r + jnp.where(here, (sj_ref[idx] * MOE_SUB).astype(F32), 0.0)
    sel = ((rt[0:1, :] == row_e) & (rt[2:3, :] == row_r)) | ((rt[1:2, :] == row_e) & (rt[3:4, :] == row_r))
    buf[...] = _dot(jnp.where(sel, 1.0, 0.0).astype(BF16), x_ref[...]).astype(BF16)

    def slot_copy(k):
        dst = pl.multiple_of(sd_ref[i * MOE_SLOTS + k] * MOE_SUB, MOE_SUB)
        return pltpu.make_async_copy(buf.at[pl.ds(k * MOE_SUB, MOE_SUB)], xg_hbm.at[pl.ds(dst, MOE_SUB)], sem.at[k])

    for k in range(MOE_SLOTS):
        @pl.when(sv_ref[i * MOE_SLOTS + k] == 1)
        def _(k=k):
            slot_copy(k).start()

    for k in range(MOE_SLOTS):
        @pl.when(sv_ref[i * MOE_SLOTS + k] == 1)
        def _(k=k):
            slot_copy(k).wait()

    def pad_copy(m):
        dst = pl.multiple_of(pd_ref[m] * MOE_SUB, MOE_SUB)
        return pltpu.make_async_copy(zbuf.at[pl.ds(0, MOE_SUB)], xg_hbm.at[pl.ds(dst, MOE_SUB)], zsem.at[0])

    def tail_copy(r):
        dst = pl.multiple_of(r * rows, rows)
        return pltpu.make_async_copy(zbuf, xg_hbm.at[pl.ds(dst, rows)], zsem.at[1])

    @pl.when(i == pl.num_programs(0) - 1)
    def _():
        zbuf[...] = jnp.zeros_like(zbuf)

        def pad_start(m, carry):
            @pl.when(pv_ref[m] == 1)
            def _():
                pad_copy(m).start()
            return carry

        def pad_wait(m, carry):
            @pl.when(pv_ref[m] == 1)
            def _():
                pad_copy(m).wait()
            return carry

        def tail_start(r, carry):
            tail_copy(r).start()
            return carry

        def tail_wait(r, carry):
            tail_copy(r).wait()
            return carry

        lax.fori_loop(0, n_pad, pad_start, 0)
        lax.fori_loop(nv_ref[0], n_rt, tail_start, 0)
        lax.fori_loop(0, n_pad, pad_wait, 0)
        lax.fori_loop(nv_ref[0], n_rt, tail_wait, 0)


def _moe_gather(xb, rt, tb, rows):
    nb = tb["nb_tot"]
    tt = MOE_TOK_TILE
    nt = xb.shape[0] // tt
    n_pad = tb["pad_dest"].shape[0]
    return pl.pallas_call(
        functools.partial(_moe_gather_kernel, n_pad=n_pad, n_rt=tb["n_rt"], rows=rows),
        out_shape=jax.ShapeDtypeStruct((nb * MOE_SUB, D_MODEL), BF16),
        grid_spec=pltpu.PrefetchScalarGridSpec(
            num_scalar_prefetch=7,
            grid=(nt,),
            in_specs=[pl.BlockSpec((tt, D_MODEL), lambda i, *_: (i, 0)),
                      pl.BlockSpec((1, 8, tt), lambda i, *_: (i, 0, 0))],
            out_specs=pl.BlockSpec(memory_space=pl.ANY),
            scratch_shapes=[pltpu.VMEM((MOE_SLOTS * MOE_SUB, D_MODEL), BF16),
                            pltpu.VMEM((rows, D_MODEL), BF16),
                            pltpu.SemaphoreType.DMA((MOE_SLOTS,)),
                            pltpu.SemaphoreType.DMA((2,))]),
        compiler_params=_cparams(("arbitrary",)),
        name="moe_gather",
    )(tb["slot_e"], tb["slot_j"], tb["slot_dest"], tb["slot_valid"], tb["pad_dest"], tb["pad_valid"],
      tb["n_valid_rt"], xb, rt)


def _moe_ffn_kernel(e_ref, valid_ref, x_ref, wg_ref, wu_ref, wd_ref, y_ref, acc, *, rows):
    r = pl.program_id(0)
    f = pl.program_id(1)
    nf = pl.num_programs(1)
    ok = valid_ref[r] == 1

    @pl.when(jnp.logical_and(ok, f == 0))
    def _():
        acc[...] = jnp.zeros_like(acc)

    @pl.when(ok)
    def _():
        for rs in range(rows // MOE_ROW_SUB):
            sl = slice(rs * MOE_ROW_SUB, (rs + 1) * MOE_ROW_SUB)
            xb = x_ref[sl, :]
            g = _dot(xb, wg_ref[0])
            u = _dot(xb, wu_ref[0])
            acc[sl, :] += _dot((_silu(g) * u).astype(BF16), wd_ref[0])

    @pl.when(jnp.logical_and(ok, f == nf - 1))
    def _():
        y_ref[...] = acc[...].astype(y_ref.dtype)

    @pl.when(jnp.logical_and(jnp.logical_not(ok), f == nf - 1))
    def _():
        y_ref[...] = jnp.zeros_like(y_ref)


def _moe_ffn(xg, w_gu_bf, w_d_bf, tb, rows):
    n_rt = tb["n_rt"]
    nf = D_FF_E // MOE_FF_BLK
    fb = MOE_FF_BLK
    assert rows % MOE_ROW_SUB == 0

    def fsel(r, f, v):
        return jnp.where(v[r] == 1, f, nf - 1)

    return pl.pallas_call(
        functools.partial(_moe_ffn_kernel, rows=rows),
        out_shape=jax.ShapeDtypeStruct(xg.shape, BF16),
        grid_spec=pltpu.PrefetchScalarGridSpec(
            num_scalar_prefetch=2,
            grid=(n_rt, nf),
            in_specs=[pl.BlockSpec((rows, D_MODEL), lambda r, f, e, v: (r, 0)),
                      pl.BlockSpec((1, D_MODEL, fb), lambda r, f, e, v: (e[r], 0, fsel(r, f, v))),
                      pl.BlockSpec((1, D_MODEL, fb), lambda r, f, e, v: (e[r], 0, nf + fsel(r, f, v))),
                      pl.BlockSpec((1, fb, D_MODEL), lambda r, f, e, v: (e[r], fsel(r, f, v), 0))],
            out_specs=pl.BlockSpec((rows, D_MODEL), lambda r, f, e, v: (r, 0)),
            scratch_shapes=[pltpu.VMEM((rows, D_MODEL), F32)]),
        compiler_params=_cparams(("arbitrary", "arbitrary")),
        name="moe_grouped_ffn",
    )(tb["rt_e"], tb["rt_valid"], xg, w_gu_bf, w_gu_bf, w_d_bf)


def _moe_combine_kernel(se_ref, sj_ref, sd_ref, sv_ref, *refs):
    y_refs = refs[:MOE_SLOTS]
    rtt_ref, x_ref, g_ref, b_ref, o_ref, ybuf = refs[MOE_SLOTS:]
    i = pl.program_id(0)
    width = MOE_SLOTS * MOE_SUB
    lane = lax.broadcasted_iota(I32, (1, width), 1)
    lane_e = jnp.full((1, width), -1.0, F32)
    lane_r = (lane % MOE_SUB).astype(F32)
    for k in range(MOE_SLOTS):
        idx = i * MOE_SLOTS + k
        valid = sv_ref[idx] == 1
        yk = y_refs[k][...]
        ybuf[k * MOE_SUB:(k + 1) * MOE_SUB, :] = jnp.where(valid, yk, jnp.zeros_like(yk))
        here = (lane // MOE_SUB) == k
        lane_e = jnp.where(here, jnp.where(valid, se_ref[idx], -1).astype(F32), lane_e)
        lane_r = lane_r + jnp.where(here, (sj_ref[idx] * MOE_SUB).astype(F32), 0.0)
    rtt = rtt_ref[0]
    a1 = jnp.where((rtt[:, 0:1] == lane_e) & (rtt[:, 2:3] == lane_r), 1.0, 0.0).astype(BF16)
    a2 = jnp.where((rtt[:, 1:2] == lane_e) & (rtt[:, 3:4] == lane_r), 1.0, 0.0).astype(BF16)
    yb = ybuf[...]
    ff = rtt[:, 4:5] * _dot(a1, yb) + rtt[:, 5:6] * _dot(a2, yb)
    o_ref[...] = _layer_norm_rows(DN_ALPHA * x_ref[...] + ff, g_ref[...], b_ref[...])


def _moe_combine(yg, rtt, x, g, b, tb):
    tt = MOE_TOK_TILE
    T = x.shape[0]

    def y_map(i, se, sj, sd, sv, *, k):
        return (sd[i * MOE_SLOTS + k], 0)

    y_specs = [pl.BlockSpec((MOE_SUB, D_MODEL), functools.partial(y_map, k=k)) for k in range(MOE_SLOTS)]
    return pl.pallas_call(
        _moe_combine_kernel,
        out_shape=jax.ShapeDtypeStruct((T, D_MODEL), F32),
        grid_spec=pltpu.PrefetchScalarGridSpec(
            num_scalar_prefetch=4,
            grid=(T // tt,),
            in_specs=y_specs + [pl.BlockSpec((1, tt, 128), lambda i, *_: (i, 0, 0)),
                                pl.BlockSpec((tt, D_MODEL), lambda i, *_: (i, 0)),
                                pl.BlockSpec((1, D_MODEL), lambda i, *_: (0, 0)),
                                pl.BlockSpec((1, D_MODEL), lambda i, *_: (0, 0))],
            out_specs=pl.BlockSpec((tt, D_MODEL), lambda i, *_: (i, 0)),
            scratch_shapes=[pltpu.VMEM((MOE_SLOTS * MOE_SUB, D_MODEL), BF16)]),
        compiler_params=_cparams(("arbitrary",)),
        name="moe_combine",
    )(tb["slot_e"], tb["slot_j"], tb["slot_dest"], tb["slot_valid"], *([yg] * MOE_SLOTS), rtt, x,
      g.reshape(1, D_MODEL), b.reshape(1, D_MODEL))


def _moe(x, xb, w_router, w_gu_bf, w_d_bf, g, b, rows):
    T = x.shape[0]
    nt = T // MOE_TOK_TILE
    rt, rtt, cnt = _router(x, w_router)
    tb = _moe_tables(cnt[:, :, 0], nt, rows // MOE_SUB)
    xg = _moe_gather(xb, rt, tb, rows)
    yg = _moe_ffn(xg, w_gu_bf, w_d_bf, tb, rows)
    return _moe_combine(yg, rtt, x, g, b, tb)


def _trunk(x, pos_rows, n_batch, mem_k, mem_v, ret_state, past_k, past_v, conv_state, gdn_state, p, cfg):
    T = x.shape[0]
    per = T // n_batch
    tm = cfg["tm"]
    tm_mm = cfg["tm_mm"]
    ln_g, ln_b = p["ln_g"], p["ln_b"]

    qka, va, ga, qb, kb, vb = _proj0(x.astype(BF16), p["w_in0"], _rope_tables(pos_rows), tm_mm)
    oa, ret_new = _retention(qka, va, ga, ret_state, p["ret_norm_g"], n_batch, cfg["ret_lc"])
    lp = p["diff_lambda"].astype(F32)
    lam = (jnp.exp(jnp.sum(lp[0] * lp[1])) - jnp.exp(jnp.sum(lp[2] * lp[3])) + DIFF_LAMBDA_INIT).reshape(1)
    if past_k is None:
        ob = _diff_prompt(qb, kb, vb, lam, p["diff_norm_g"], n_batch, per, cfg["diff_tq"])
    else:
        ob = _diff_step(qb, kb, vb, past_k, past_v, lam, p["diff_norm_g"], n_batch, per)
    x, xb = _matmul_deepnorm([oa, ob], [p["w_out0"][:A_V], p["w_out0"][A_V:]], x, ln_g[0, 0], ln_b[0, 0], tm,
                             name="out0_deepnorm")
    q = _matmul(xb, p["w_xq"][0:1], BF16, tm_mm, 1024, scale=X_HD ** -0.5, name="xq0")[0]
    xa = _xattn(q, mem_k[0], mem_v[0], n_batch, cfg["x_tq"])
    x, xb = _matmul_deepnorm([xa], [p["w_xo"][0]], x, ln_g[0, 1], ln_b[0, 1], tm, name="xo0_deepnorm")
    hmid = _swiglu_up(xb, p["w_ffn_gu"], tm, cfg["ffn_tn"])
    x, xb = _matmul_deepnorm([hmid], [p["w_ffn_d"]], x, ln_g[0, 2], ln_b[0, 2], tm, name="ffn_down_deepnorm")

    proj1 = _matmul(xb, p["w_in1_main"][None], F32, tm_mm, 1024, name="proj1")[0]
    ab = _matmul(xb, p["w_in1_ab"][None], F32, tm_mm, 128, name="proj1_gates")[0]
    qkv3 = proj1.reshape(n_batch, per, 4 * C_W)[:, :, :3 * C_W]
    conv_new = qkv3[:, per - (CONV_W - 1):, :]
    state8 = jnp.pad(conv_state.astype(F32), ((0, 0), (8 - (CONV_W - 1), 0), (0, 0)))
    qkvn = _gdn_conv(proj1, state8, p["conv_w"], n_batch, cfg["conv_tm"])
    lc = cfg["gdn_lc"]
    if per < lc:
        padrows = lambda a: jnp.pad(a.reshape(n_batch, per, a.shape[-1]),
                                    ((0, 0), (0, lc - per), (0, 0))).reshape(n_batch * lc, a.shape[-1])
        og, gdn_new = _gdn(padrows(qkvn), padrows(proj1), padrows(ab), p["a_log"], p["dt_bias"],
                           p["gdn_norm_g"], gdn_state, n_batch, lc, per)
        og = og.reshape(n_batch, lc, C_W)[:, :per].reshape(T, C_W)
    else:
        og, gdn_new = _gdn(qkvn, proj1, ab, p["a_log"], p["dt_bias"], p["gdn_norm_g"], gdn_state,
                           n_batch, lc, lc)
    x, xb = _matmul_deepnorm([og], [p["w_out1"]], x, ln_g[1, 0], ln_b[1, 0], tm, name="out1_deepnorm")
    q = _matmul(xb, p["w_xq"][1:2], BF16, tm_mm, 1024, scale=X_HD ** -0.5, name="xq1")[0]
    xa = _xattn(q, mem_k[1], mem_v[1], n_batch, cfg["x_tq"])
    x, xb = _matmul_deepnorm([xa], [p["w_xo"][1]], x, ln_g[1, 1], ln_b[1, 1], tm, name="xo1_deepnorm")
    x = _moe(x, xb, p["w_router"], p["w_moe_gu"], p["w_moe_d"], ln_g[1, 2], ln_b[1, 2], cfg["moe_rows"])
    return x, kb, vb, ret_new, conv_new, gdn_new


PROMPT_CFG = dict(tm=512, tm_mm=1024, ret_lc=256, diff_tq=512, x_tq=512, ffn_tn=1408, conv_tm=256,
                  gdn_lc=256, moe_rows=1024)
STEP_CFG = dict(tm=512, tm_mm=512, ret_lc=16, diff_tq=16, x_tq=16, ffn_tn=1408, conv_tm=16,
                gdn_lc=128, moe_rows=256)


def kernel(x_prompt, x_sample, cache_diff_k, cache_diff_v, state_ret, state_gdn_conv, state_gdn, cache_mem_k, cache_mem_v, mem_prompt, w_in0, ret_norm_g, diff_lambda, diff_norm_g, w_out0, w_in1, conv_w, a_log, dt_bias, gdn_norm_g, w_out1, w_xq, w_xkv, w_xo, w_ffn_gu, w_ffn_d, w_router, w_moe_gu, w_moe_d, ln_g, ln_b):
    B, S, _ = x_prompt.shape
    DB, L, _ = x_sample.shape
    P = cache_diff_k.shape[1]
    bf = lambda w: w.astype(BF16)
    w_ab = jnp.pad(w_in1[:, 4 * C_W:], ((0, 0), (0, 128 - 2 * H_C)))
    p = dict(w_in0=bf(w_in0), ret_norm_g=ret_norm_g, diff_lambda=diff_lambda, diff_norm_g=diff_norm_g,
             w_out0=bf(w_out0), w_in1_main=bf(w_in1[:, :4 * C_W]), w_in1_ab=bf(w_ab), conv_w=conv_w,
             a_log=a_log, dt_bias=dt_bias, gdn_norm_g=gdn_norm_g, w_out1=bf(w_out1), w_xq=bf(w_xq),
             w_xo=bf(w_xo), w_ffn_gu=bf(w_ffn_gu), w_ffn_d=bf(w_ffn_d), w_router=w_router,
             w_moe_gu=bf(w_moe_gu), w_moe_d=bf(w_moe_d), ln_g=ln_g, ln_b=ln_b)

    mem = mem_prompt.reshape(B * N_MEM, D_MODEL)
    mk_p = _matmul(mem, bf(w_xkv[:, :, :D_MODEL]), F32, 1024, 1024, name="mem_k")
    mv_p = _matmul(mem, bf(w_xkv[:, :, D_MODEL:]), F32, 1024, 1024, name="mem_v")

    y_p, dk_p, dv_p, ret_p, conv_p, gdn_p = _trunk(
        x_prompt.reshape(B * S, D_MODEL), jnp.arange(S), B, mk_p, mv_p,
        jnp.zeros((B, H_A, DK_A, DV_A), F32), None, None,
        jnp.zeros((B, CONV_W - 1, 3 * C_W), F32), jnp.zeros((B, H_C, DK_C, DV_C), F32), p, PROMPT_CFG)

    pos_s = jnp.tile(P + jnp.arange(L), DB)
    y_s, dk_s, dv_s, ret_s, conv_s, gdn_s = _trunk(
        x_sample.reshape(DB * L, D_MODEL), pos_s, DB,
        cache_mem_k.reshape(DEPTH, DB * N_MEM, D_MODEL), cache_mem_v.reshape(DEPTH, DB * N_MEM, D_MODEL),
        state_ret.astype(F32), cache_diff_k.reshape(DB * P, B_QK), cache_diff_v.reshape(DB * P, B_V),
        state_gdn_conv, state_gdn.astype(F32), p, STEP_CFG)

    shape5 = (DEPTH, B, N_MEM, X_HEADS, X_HD)
    return (y_p.reshape(B, S, D_MODEL), y_s.reshape(DB, L, D_MODEL),
            mk_p.reshape(shape5), mv_p.reshape(shape5),
            dk_p.reshape(B, S, H_B, 2 * DK_B), dv_p.reshape(B, S, H_B, DV_B),
            ret_p, conv_p, gdn_p,
            dk_s.reshape(DB, L, H_B, 2 * DK_B), dv_s.reshape(DB, L, H_B, DV_B),
            ret_s, conv_s, gdn_s)
```

```python
import functools
import math

import jax
import jax.numpy as jnp
from jax import lax
from jax.experimental import pallas as pl
from jax.experimental.pallas import tpu as pltpu

F32 = jnp.float32
BF16 = jnp.bfloat16
I32 = jnp.int32

D_MODEL = 1024
DEPTH = 2
CHUNK = 64
H_A, DK_A, DV_A = 4, 64, 128
RET_THETA = 10000.0
H_B, DK_B = 4, 64
DV_B = 2 * DK_B
ROT_B = DK_B // 4
ROPE_THETA = 500000.0
DIFF_LAMBDA_INIT = 0.8 - 0.6 * math.exp(-0.3 * 0)
H_C, DK_C, DV_C = 8, 128, 128
C_W = H_C * DV_C
CONV_W = 4
N_MEM = 256
X_HEADS = 4
X_HD = D_MODEL // X_HEADS
D_FF = 2816
N_EXPERTS = 8
D_FF_E = 3584
DN_ALPHA = (2 * DEPTH) ** 0.25
LN_EPS = 1e-5
A_QK = H_A * DK_A
A_V = H_A * DV_A
B_QK = H_B * 2 * DK_B
B_V = H_B * DV_B
PROJ0 = 2 * A_QK + 2 * A_V + 2 * B_QK + B_V

VMEM_LIMIT_V7X = 52 * 1024 * 1024
NEG_BIG = -1e30

MOE_TOK_TILE = 512
MOE_SUB = 32
MOE_FF_BLK = 512
MOE_SLOTS = 2 * MOE_TOK_TILE // MOE_SUB + N_EXPERTS
GDN_HEAD_GROUP = 4
CONV_ROWS = 32


def _cparams(sem):
    return pltpu.CompilerParams(dimension_semantics=sem, vmem_limit_bytes=VMEM_LIMIT_V7X)


def _dot(a, b):
    return jnp.dot(a, b, preferred_element_type=F32)


def _dot_nt(a, b):
    return lax.dot_general(a, b, (((1,), (1,)), ((), ())), preferred_element_type=F32)


def _dot_tn(a, b):
    return lax.dot_general(a, b, (((0,), (0,)), ((), ())), preferred_element_type=F32)


def _silu(x):
    return x * (1.0 / (1.0 + jnp.exp(-x)))


def _layer_norm_rows(y, g, b):
    mu = jnp.mean(y, axis=-1, keepdims=True)
    d = y - mu
    var = jnp.mean(d * d, axis=-1, keepdims=True)
    return d * lax.rsqrt(var + LN_EPS) * g + b


def _mm_kernel(x_ref, w_ref, o_ref, *, scale):
    acc = _dot(x_ref[...].astype(BF16), w_ref[0])
    if scale != 1.0:
        acc = acc * scale
    o_ref[0] = acc.astype(o_ref.dtype)


def _matmul(x, w, out_dtype, tm, tn, scale=1.0, name="matmul"):
    M, K = x.shape
    G, _, N = w.shape
    tm = min(tm, M)
    tn = min(tn, N)
    assert M % tm == 0 and N % tn == 0
    return pl.pallas_call(
        functools.partial(_mm_kernel, scale=scale),
        out_shape=jax.ShapeDtypeStruct((G, M, N), out_dtype),
        grid=(G, M // tm, N // tn),
        in_specs=[pl.BlockSpec((tm, K), lambda g, i, j: (i, 0)),
                  pl.BlockSpec((1, K, tn), lambda g, i, j: (g, 0, j))],
        out_specs=pl.BlockSpec((1, tm, tn), lambda g, i, j: (g, i, j)),
        compiler_params=_cparams(("parallel", "parallel", "arbitrary")),
        name=name,
    )(x, w)


def _mm_dn_kernel(*refs, n_in, next_scale):
    xs = refs[:n_in]
    ws = refs[n_in:2 * n_in]
    rest = refs[2 * n_in:]
    if next_scale is None:
        r_ref, g_ref, b_ref, o_ref, ob_ref = rest
    else:
        r_ref, g_ref, b_ref, wn_ref, o_ref, ob_ref = rest
    acc = DN_ALPHA * r_ref[...]
    for x_ref, w_ref in zip(xs, ws):
        acc = acc + _dot(x_ref[...].astype(BF16), w_ref[...])
    y = _layer_norm_rows(acc, g_ref[...], b_ref[...])
    o_ref[...] = y
    if next_scale is None:
        ob_ref[...] = y.astype(BF16)
    else:
        ob_ref[...] = (_dot(y.astype(BF16), wn_ref[...]) * next_scale).astype(BF16)


def _matmul_deepnorm(xs, ws, resid, g, b, tm, w_next=None, next_scale=None, name="matmul_deepnorm"):
    M = resid.shape[0]
    tm = min(tm, M)
    assert M % tm == 0
    n_in = len(xs)
    in_specs = [pl.BlockSpec((tm, x.shape[1]), lambda i: (i, 0)) for x in xs]
    in_specs += [pl.BlockSpec(w.shape, lambda i: (0, 0)) for w in ws]
    in_specs += [pl.BlockSpec((tm, D_MODEL), lambda i: (i, 0)),
                 pl.BlockSpec((1, D_MODEL), lambda i: (0, 0)),
                 pl.BlockSpec((1, D_MODEL), lambda i: (0, 0))]
    args = [*xs, *ws, resid, g.reshape(1, D_MODEL), b.reshape(1, D_MODEL)]
    if w_next is not None:
        in_specs.append(pl.BlockSpec(w_next.shape, lambda i: (0, 0)))
        args.append(w_next)
    return pl.pallas_call(
        functools.partial(_mm_dn_kernel, n_in=n_in, next_scale=next_scale),
        out_shape=[jax.ShapeDtypeStruct((M, D_MODEL), F32), jax.ShapeDtypeStruct((M, D_MODEL), BF16)],
        grid=(M // tm,),
        in_specs=in_specs,
        out_specs=[pl.BlockSpec((tm, D_MODEL), lambda i: (i, 0))] * 2,
        compiler_params=_cparams(("parallel",)),
        name=name,
    )(*args)


def _swap_halves(x, group, half):
    n = x.shape[-1]
    lane = lax.broadcasted_iota(I32, x.shape, x.ndim - 1) % group
    up = pltpu.roll(x, n - half, x.ndim - 1)
    dn = pltpu.roll(x, half, x.ndim - 1)
    return jnp.where(lane < half, up, dn)


def _store_rows(o_ref, y, by_head):
    if not by_head:
        o_ref[...] = y.astype(o_ref.dtype)
        return
    tm = y.shape[0]
    for h in range(H_B):
        o_ref[pl.ds(h, tm, stride=H_B), :] = y[:, h * DV_B:(h + 1) * DV_B].astype(o_ref.dtype)


def _proj_plain_kernel(x_ref, w_ref, o_ref, *, by_head):
    _store_rows(o_ref, _dot(x_ref[...], w_ref[...]), by_head)


def _proj_rot_kernel(x_ref, w_ref, c_ref, s_ref, o_ref, *, group, half, by_head):
    acc = _dot(x_ref[...], w_ref[...])
    _store_rows(o_ref, acc * c_ref[...] + _swap_halves(acc, group, half) * s_ref[...], by_head)


def _proj0_block(xb, w, out_dtype, tm, rot=None, by_head=False, name="proj0"):
    T = xb.shape[0]
    blk = w.shape[1]
    tm = min(tm, T)
    row = lambda i: (i, 0)
    in_specs = [pl.BlockSpec((tm, D_MODEL), row), pl.BlockSpec((D_MODEL, blk), lambda i: (0, 0))]
    args = [xb, w]
    if rot is None:
        body = functools.partial(_proj_plain_kernel, by_head=by_head)
    else:
        cos, sin, group, half = rot
        P = cos.shape[0]
        tm = min(tm, P)
        assert P % tm == 0
        npb = P // tm
        in_specs[0] = pl.BlockSpec((tm, D_MODEL), row)
        in_specs += [pl.BlockSpec((tm, blk), lambda i: (i % npb, 0))] * 2
        args += [cos, sin]
        body = functools.partial(_proj_rot_kernel, group=group, half=half, by_head=by_head)
    assert T % tm == 0
    if by_head:
        assert blk == H_B * DV_B
        out_shape = jax.ShapeDtypeStruct((T * H_B, DV_B), out_dtype)
        out_spec = pl.BlockSpec((tm * H_B, DV_B), row)
    else:
        out_shape = jax.ShapeDtypeStruct((T, blk), out_dtype)
        out_spec = pl.BlockSpec((tm, blk), row)
    return pl.pallas_call(
        body,
        out_shape=out_shape,
        grid=(T // tm,),
        in_specs=in_specs,
        out_specs=out_spec,
        compiler_params=_cparams(("parallel",)),
        name=name,
    )(*args)


def _rope_tables(pos):
    pos = pos.astype(F32)
    inv = RET_THETA ** (-jnp.arange(0, DK_A, 2, dtype=F32) / DK_A)
    ang = pos[:, None] * inv[None, :]
    c, s = jnp.cos(ang), jnp.sin(ang)
    c64 = jnp.concatenate([c, c], axis=-1)
    s64 = jnp.concatenate([-s, s], axis=-1)
    kscale = DK_A ** -0.5
    ca = jnp.concatenate([jnp.tile(c64, (1, H_A)), jnp.tile(c64, (1, H_A)) * kscale], axis=-1)
    sa = jnp.concatenate([jnp.tile(s64, (1, H_A)), jnp.tile(s64, (1, H_A)) * kscale], axis=-1)
    invb = ROPE_THETA ** (-jnp.arange(0, ROT_B, 2, dtype=F32) / ROT_B)
    angb = pos[:, None] * invb[None, :]
    cb, sb = jnp.cos(angb), jnp.sin(angb)
    rest = DK_B - ROT_B
    c64b = jnp.concatenate([cb, cb, jnp.ones((pos.shape[0], rest), F32)], axis=-1)
    s64b = jnp.concatenate([-sb, sb, jnp.zeros((pos.shape[0], rest), F32)], axis=-1)
    ck = jnp.tile(c64b, (1, 2 * H_B))
    sk = jnp.tile(s64b, (1, 2 * H_B))
    qscale = DK_B ** -0.5
    return ca, sa, ck * qscale, sk * qscale, ck, sk


def _proj0(xb, w_in0_bf, tables, tm):
    ca, sa, cq, sq, ck, sk = tables
    blk = 512
    assert PROJ0 == 6 * blk
    w = [w_in0_bf[:, j * blk:(j + 1) * blk] for j in range(6)]
    rot_a = (DK_A, DK_A // 2)
    rot_b = (DK_B, ROT_B // 2)
    qka = _proj0_block(xb, w[0], BF16, tm, (ca, sa) + rot_a, name="proj0_qk_ret")
    va = _proj0_block(xb, w[1], BF16, tm, name="proj0_v_ret")
    ga = _proj0_block(xb, w[2], F32, tm, name="proj0_gate_ret")
    qb = _proj0_block(xb, w[3], BF16, tm, (cq, sq) + rot_b, name="proj0_q_diff")
    kb = _proj0_block(xb, w[4], F32, tm, (ck, sk) + rot_b, by_head=True, name="proj0_k_diff")
    vb = _proj0_block(xb, w[5], F32, tm, by_head=True, name="proj0_v_diff")
    return qka, va, ga, qb, kb, vb


def _retention_kernel(qk_ref, v_ref, g_ref, dmat_ref, qdec_ref, kdec_ref, sdec_ref, s0_ref, ng_ref,
                      o_ref, sout_ref, s_scr):
    c = pl.program_id(1)

    @pl.when(c == 0)
    def _():
        s_scr[...] = s0_ref[0]

    for h in range(H_A):
        q = qk_ref[:, h * DK_A:(h + 1) * DK_A]
        k = qk_ref[:, A_QK + h * DK_A:A_QK + (h + 1) * DK_A]
        v = v_ref[:, h * DV_A:(h + 1) * DV_A]
        s = s_scr[h]
        scores = _dot_nt(q, k) * dmat_ref[h]
        o = _dot(scores.astype(BF16), v) + _dot(q, s.astype(BF16)) * qdec_ref[h]
        kd = (k.astype(F32) * kdec_ref[h]).astype(BF16)
        s_scr[h] = s * sdec_ref[h] + _dot_tn(kd, v)
        mu = jnp.mean(o, axis=-1, keepdims=True)
        d = o - mu
        var = jnp.mean(d * d, axis=-1, keepdims=True)
        gate = g_ref[:, h * DV_A:(h + 1) * DV_A]
        y = d * lax.rsqrt(var + LN_EPS) * ng_ref[:, h * DV_A:(h + 1) * DV_A] * _silu(gate)
        o_ref[:, h * DV_A:(h + 1) * DV_A] = y.astype(o_ref.dtype)

    @pl.when(c == pl.num_programs(1) - 1)
    def _():
        sout_ref[0] = s_scr[...]


def _retention_tables(lc):
    lg = jnp.log1p(-jnp.exp2(-5.0 - jnp.arange(H_A, dtype=F32)))
    t = jnp.arange(lc, dtype=F32)
    rel = t[:, None] - t[None, :]
    causal = rel >= 0
    dmat = jnp.where(causal, jnp.exp(lg[:, None, None] * jnp.where(causal, rel, 0.0)), 0.0)
    qdec = jnp.exp(lg[:, None] * (t[None, :] + 1.0))
    kdec = jnp.exp(lg[:, None] * (lc - 1.0 - t[None, :]))
    sdec = jnp.exp(lg * lc)
    return (dmat,
            jnp.broadcast_to(qdec[:, :, None], (H_A, lc, DV_A)),
            jnp.broadcast_to(kdec[:, :, None], (H_A, lc, DK_A)),
            jnp.broadcast_to(sdec[:, None, None], (H_A, DK_A, DV_A)))


def _retention(qka, va, ga, state0, ret_norm_g, n_batch, lc):
    T = qka.shape[0]
    nc = T // (n_batch * lc)
    assert nc * n_batch * lc == T
    dmat, qdec, kdec, sdec = _retention_tables(lc)
    row = lambda b, c: (b * nc + c, 0)
    const3 = lambda b, c: (0, 0, 0)
    return pl.pallas_call(
        _retention_kernel,
        out_shape=[jax.ShapeDtypeStruct((T, A_V), BF16),
                   jax.ShapeDtypeStruct((n_batch, H_A, DK_A, DV_A), F32)],
        grid=(n_batch, nc),
        in_specs=[pl.BlockSpec((lc, 2 * A_QK), row),
                  pl.BlockSpec((lc, A_V), row),
                  pl.BlockSpec((lc, A_V), row),
                  pl.BlockSpec((H_A, lc, lc), const3),
                  pl.BlockSpec((H_A, lc, DV_A), const3),
                  pl.BlockSpec((H_A, lc, DK_A), const3),
                  pl.BlockSpec((H_A, DK_A, DV_A), const3),
                  pl.BlockSpec((1, H_A, DK_A, DV_A), lambda b, c: (b, 0, 0, 0)),
                  pl.BlockSpec((1, A_V), lambda b, c: (0, 0))],
        out_specs=[pl.BlockSpec((lc, A_V), row),
                   pl.BlockSpec((1, H_A, DK_A, DV_A), lambda b, c: (b, 0, 0, 0))],
        scratch_shapes=[pltpu.VMEM((H_A, DK_A, DV_A), F32)],
        compiler_params=_cparams(("parallel", "arbitrary")),
        name="retention",
    )(qka, va, ga, dmat, qdec, kdec, sdec, state0, ret_norm_g.reshape(1, A_V))


def _diff_finish(acc1, l1, acc2, l2, lam, ng):
    o = acc1 * (1.0 / l1) - lam * (acc2 * (1.0 / l2))
    ms = jnp.mean(o * o, axis=-1, keepdims=True)
    return o * lax.rsqrt(ms + LN_EPS) * ng * (1.0 - DIFF_LAMBDA_INIT)


def _diff_prompt_kernel(lam_ref, q_ref, k_ref, v_ref, ng_ref, o_ref, kbf, vbf, *, tq):
    qi = pl.program_id(2)

    @pl.when(qi == 0)
    def _():
        h = pl.program_id(1)
        seq = kbf.shape[0]
        kbf[...] = k_ref[pl.ds(h, seq, stride=H_B), :].astype(BF16)
        vbf[...] = v_ref[pl.ds(h, seq, stride=H_B), :].astype(BF16)

    q1 = q_ref[:, :DK_B]
    q2 = q_ref[:, DK_B:]

    def block(kblk, carry, masked):
        m1, l1, a1, m2, l2, a2 = carry
        off = pl.multiple_of(kblk * tq, tq)
        kk = kbf[pl.ds(off, tq), :]
        vv = vbf[pl.ds(off, tq), :]
        s1 = _dot_nt(q1, kk[:, :DK_B])
        s2 = _dot_nt(q2, kk[:, DK_B:])
        if masked:
            rc = lax.broadcasted_iota(I32, (tq, tq), 0) // CHUNK
            cc = lax.broadcasted_iota(I32, (tq, tq), 1) // CHUNK
            vis = cc <= rc
            s1 = jnp.where(vis, s1, NEG_BIG)
            s2 = jnp.where(vis, s2, NEG_BIG)
        out = []
        for s, m, l, a in ((s1, m1, l1, a1), (s2, m2, l2, a2)):
            mn = jnp.maximum(m, jnp.max(s, axis=-1, keepdims=True))
            alpha = jnp.exp(m - mn)
            p = jnp.exp(s - mn)
            l = alpha * l + jnp.sum(p, axis=-1, keepdims=True)
            a = alpha * a + _dot(p.astype(BF16), vv)
            out += [mn, l, a]
        return tuple(out)

    init = (jnp.full((tq, 1), NEG_BIG, F32), jnp.zeros((tq, 1), F32), jnp.zeros((tq, DV_B), F32)) * 2
    carry = lax.fori_loop(0, qi, lambda kb, c: block(kb, c, False), init)
    m1, l1, a1, m2, l2, a2 = block(qi, carry, True)
    o_ref[...] = _diff_finish(a1, l1, a2, l2, lam_ref[0], ng_ref[...]).astype(o_ref.dtype)


def _diff_prompt(qb, kb, vb, lam, diff_norm_g, n_batch, seq, tq):
    T = qb.shape[0]
    assert seq % tq == 0 and tq % CHUNK == 0
    nq = seq // tq
    return pl.pallas_call(
        functools.partial(_diff_prompt_kernel, tq=tq),
        out_shape=jax.ShapeDtypeStruct((T, B_V), BF16),
        grid_spec=pltpu.PrefetchScalarGridSpec(
            num_scalar_prefetch=1,
            grid=(n_batch, H_B, nq),
            in_specs=[pl.BlockSpec((tq, DV_B), lambda b, h, i, lam: (b * nq + i, h)),
                      pl.BlockSpec((seq * H_B, DV_B), lambda b, h, i, lam: (b, 0)),
                      pl.BlockSpec((seq * H_B, DV_B), lambda b, h, i, lam: (b, 0)),
                      pl.BlockSpec((1, DV_B), lambda b, h, i, lam: (0, 0))],
            out_specs=pl.BlockSpec((tq, DV_B), lambda b, h, i, lam: (b * nq + i, h)),
            scratch_shapes=[pltpu.VMEM((seq, DV_B), BF16), pltpu.VMEM((seq, DV_B), BF16)]),
        compiler_params=_cparams(("parallel", "parallel", "arbitrary")),
        name="diff_attn_prompt",
    )(lam, qb, kb, vb, diff_norm_g.reshape(1, DV_B))


def _diff_step_kernel(lam_ref, q_ref, kp_ref, vp_ref, kn_ref, vn_ref, ng_ref, o_ref, *, past, ln):
    qchunk = (past + lax.broadcasted_iota(I32, (ln, 1), 0)) // CHUNK
    vis_p = (lax.broadcasted_iota(I32, (ln, past), 1) // CHUNK) <= qchunk
    vis_n = ((past + lax.broadcasted_iota(I32, (ln, ln), 1)) // CHUNK) <= qchunk
    for h in range(H_B):
        q = q_ref[:, h * DV_B:(h + 1) * DV_B]
        kp = kp_ref[pl.ds(h, past, stride=H_B), :].astype(BF16)
        vp = vp_ref[pl.ds(h, past, stride=H_B), :].astype(BF16)
        kn = kn_ref[pl.ds(h, ln, stride=H_B), :].astype(BF16)
        vn = vn_ref[pl.ds(h, ln, stride=H_B), :].astype(BF16)
        accs = []
        for c in range(2):
            sl = slice(c * DK_B, (c + 1) * DK_B)
            sp = jnp.where(vis_p, _dot_nt(q[:, sl], kp[:, sl]), NEG_BIG)
            sn = jnp.where(vis_n, _dot_nt(q[:, sl], kn[:, sl]), NEG_BIG)
            m = jnp.maximum(jnp.max(sp, axis=-1, keepdims=True), jnp.max(sn, axis=-1, keepdims=True))
            pp = jnp.exp(sp - m)
            pn = jnp.exp(sn - m)
            l = jnp.sum(pp, axis=-1, keepdims=True) + jnp.sum(pn, axis=-1, keepdims=True)
            accs += [_dot(pp.astype(BF16), vp) + _dot(pn.astype(BF16), vn), l]
        y = _diff_finish(accs[0], accs[1], accs[2], accs[3], lam_ref[0], ng_ref[...])
        o_ref[:, h * DV_B:(h + 1) * DV_B] = y.astype(o_ref.dtype)


def _diff_step(qb, kb, vb, past_k, past_v, lam, diff_norm_g, n_batch, ln):
    past = past_k.shape[0] // (n_batch * H_B)
    blk = lambda b, lam: (b, 0)
    return pl.pallas_call(
        functools.partial(_diff_step_kernel, past=past, ln=ln),
        out_shape=jax.ShapeDtypeStruct((n_batch * ln, B_V), BF16),
        grid_spec=pltpu.PrefetchScalarGridSpec(
            num_scalar_prefetch=1,
            grid=(n_batch,),
            in_specs=[pl.BlockSpec((ln, B_QK), blk),
                      pl.BlockSpec((past * H_B, DV_B), blk),
                      pl.BlockSpec((past * H_B, DV_B), blk),
                      pl.BlockSpec((ln * H_B, DV_B), blk),
                      pl.BlockSpec((ln * H_B, DV_B), blk),
                      pl.BlockSpec((1, DV_B), lambda b, lam: (0, 0))],
            out_specs=pl.BlockSpec((ln, B_V), blk)),
        compiler_params=_cparams(("parallel",)),
        name="diff_attn_step",
    )(lam, qb, past_k, past_v, kb, vb, diff_norm_g.reshape(1, DV_B))


def _xattn_kernel(q_ref, mk_ref, mv_ref, o_ref):
    for h in range(X_HEADS):
        sl = slice(h * X_HD, (h + 1) * X_HD)
        s = _dot_nt(q_ref[:, sl], mk_ref[:, sl].astype(BF16))
        p = jnp.exp(s - jnp.max(s, axis=-1, keepdims=True))
        l = jnp.sum(p, axis=-1, keepdims=True)
        o = _dot(p.astype(BF16), mv_ref[:, sl].astype(BF16)) * (1.0 / l)
        o_ref[:, sl] = o.astype(o_ref.dtype)


def _xattn(q, mk, mv, n_batch, tq):
    T = q.shape[0]
    per = T // n_batch
    tq = min(tq, per)
    nt = per // tq
    return pl.pallas_call(
        _xattn_kernel,
        out_shape=jax.ShapeDtypeStruct((T, D_MODEL), BF16),
        grid=(n_batch, nt),
        in_specs=[pl.BlockSpec((tq, D_MODEL), lambda b, t: (b * nt + t, 0)),
                  pl.BlockSpec((N_MEM, D_MODEL), lambda b, t: (b, 0)),
                  pl.BlockSpec((N_MEM, D_MODEL), lambda b, t: (b, 0))],
        out_specs=pl.BlockSpec((tq, D_MODEL), lambda b, t: (b * nt + t, 0)),
        compiler_params=_cparams(("parallel", "arbitrary")),
        name="mem_xattn",
    )(q, mk, mv)


def _swiglu_up_kernel(x_ref, wg_ref, wu_ref, o_ref):
    xb = x_ref[...]
    g = _dot(xb, wg_ref[...])
    u = _dot(xb, wu_ref[...])
    o_ref[...] = (_silu(g) * u).astype(o_ref.dtype)


def _swiglu_up(xb, w_gu_bf, tm, tn):
    T = xb.shape[0]
    tm = min(tm, T)
    nj = D_FF // tn
    assert D_FF % tn == 0
    return pl.pallas_call(
        _swiglu_up_kernel,
        out_shape=jax.ShapeDtypeStruct((T, D_FF), BF16),
        grid=(T // tm, nj),
        in_specs=[pl.BlockSpec((tm, D_MODEL), lambda i, j: (i, 0)),
                  pl.BlockSpec((D_MODEL, tn), lambda i, j: (0, j)),
                  pl.BlockSpec((D_MODEL, tn), lambda i, j: (0, nj + j))],
        out_specs=pl.BlockSpec((tm, tn), lambda i, j: (i, j)),
        compiler_params=_cparams(("parallel", "arbitrary")),
        name="swiglu_up",
    )(xb, w_gu_bf, w_gu_bf)


def _conv_kernel(x_ref, prev_ref, st_ref, w_ref, o_ref, buf, *, tm):
    t = pl.program_id(1)
    cb = pl.program_id(2)
    buf[0:8, :] = jnp.where(t == 0, st_ref[0], prev_ref[...])
    buf[8:8 + tm, :] = x_ref[...]
    scale = jnp.where(cb == 0, DK_C ** -0.5, 1.0)
    rc = min(CONV_ROWS, tm)
    for h in range(H_C):
        cs = slice(h * DK_C, (h + 1) * DK_C)
        w = [w_ref[i:i + 1, cs] for i in range(CONV_W)]
        for r0 in range(0, tm, rc):
            y = buf[8 + r0:8 + r0 + rc, cs] * w[CONV_W - 1]
            for i in range(CONV_W - 1):
                y = y + buf[5 + i + r0:5 + i + r0 + rc, cs] * w[i]
            y = _silu(y)
            ss = jnp.sum(y * y, axis=-1, keepdims=True)
            f = jnp.where(cb == 2, 1.0, lax.rsqrt(ss + 1e-6) * scale)
            o_ref[r0:r0 + rc, cs] = (y * f).astype(o_ref.dtype)


def _gdn_conv(proj1, conv_state8, conv_w, n_batch, tm):
    T = proj1.shape[0]
    per = T // n_batch
    tm = min(tm, per)
    nt = per // tm
    assert per % tm == 0 and tm % 8 == 0
    return pl.pallas_call(
        functools.partial(_conv_kernel, tm=tm),
        out_shape=jax.ShapeDtypeStruct((T, 3 * C_W), BF16),
        grid=(n_batch, nt, 3),
        in_specs=[pl.BlockSpec((tm, C_W), lambda b, t, c: (b * nt + t, c)),
                  pl.BlockSpec((8, C_W), lambda b, t, c: (jnp.maximum((b * nt + t) * (tm // 8) - 1, 0), c)),
                  pl.BlockSpec((1, 8, C_W), lambda b, t, c: (b, 0, c)),
                  pl.BlockSpec((CONV_W, C_W), lambda b, t, c: (0, c))],
        out_specs=pl.BlockSpec((tm, C_W), lambda b, t, c: (b * nt + t, c)),
        scratch_shapes=[pltpu.VMEM((8 + tm, C_W), F32)],
        compiler_params=_cparams(("parallel", "arbitrary", "arbitrary")),
        name="gdn_conv",
    )(proj1, proj1, conv_state8, conv_w)


def _gdn_kernel(q_ref, k_ref, v_ref, z_ref, ab_ref, alog_ref, dtb_ref, ng_ref, s0_ref,
                o_ref, sout_ref, s_scr, *, lc, l_real):
    c = pl.program_id(1)

    @pl.when(c == 0)
    def _():
        s_scr[...] = s0_ref[0]

    ri = lax.broadcasted_iota(I32, (lc, lc), 0)
    ci = lax.broadcasted_iota(I32, (lc, lc), 1)
    incl = ci <= ri
    strict = ci < ri
    eye = (ci == ri).astype(F32)
    ab = ab_ref[...]
    sp = jnp.maximum(ab + dtb_ref[...], 0.0) + jnp.log1p(jnp.exp(-jnp.abs(ab + dtb_ref[...])))
    glog = -jnp.exp(alog_ref[...]) * sp
    if l_real < lc:
        live = lax.broadcasted_iota(I32, (lc, 1), 0) < l_real
        glog = jnp.where(live, glog, 0.0)
    gcum = jnp.dot(incl.astype(F32), glog, preferred_element_type=F32, precision=lax.Precision.HIGHEST)
    gcum_t = gcum.T
    beta_all = 1.0 / (1.0 + jnp.exp(-ab))

    levels = []
    s = 1
    while s < lc:
        levels.append(((ri // (2 * s)) == (ci // (2 * s))) & (((ri // s) % 2) == 1) & (((ci // s) % 2) == 0))
        s *= 2

    for h0 in range(0, H_C, GDN_HEAD_GROUP):
        heads = range(h0, h0 + GDN_HEAD_GROUP)
        sls = [slice(h * DK_C, (h + 1) * DK_C) for h in heads]
        ks, kfs, vs, gcols, decays, egs, kbs, a_lows = [], [], [], [], [], [], [], []
        for h, sl in zip(heads, sls):
            k = k_ref[:, sl]
            v = v_ref[:, sl].astype(F32)
            if l_real < lc:
                k = jnp.where(live, k, jnp.zeros_like(k))
                v = jnp.where(live, v, 0.0)
            kf = k.astype(F32)
            gcol = gcum[:, h:h + 1]
            grow = gcum_t[h:h + 1, :]
            beta = beta_all[:, H_C + h:H_C + h + 1]
            decay = jnp.exp(jnp.where(incl, gcol - grow, NEG_BIG))
            kb = kf * beta
            ks.append(k)
            kfs.append(kf)
            gcols.append(gcol)
            decays.append(decay)
            egs.append(jnp.exp(gcol))
            kbs.append(kb)
            vs.append(v * beta)
            a_lows.append(jnp.where(strict, _dot_nt(kb.astype(BF16), k) * decay, 0.0))
        tinvs = [eye - jnp.where(levels[0], a, 0.0) for a in a_lows]
        for lvl, off in enumerate(levels[1:], start=1):
            s = 2 ** lvl
            tbs = [t.astype(BF16) for t in tinvs]
            if s < 8:
                a_offs = [jnp.where(off, a, 0.0).astype(BF16) for a in a_lows]
                ws = [_dot(a, t).astype(BF16) for a, t in zip(a_offs, tbs)]
                tinvs = [t - _dot(tb, w) for t, tb, w in zip(tinvs, tbs, ws)]
            else:
                odd = [slice((2 * j + 1) * s, (2 * j + 2) * s) for j in range(lc // (2 * s))]
                even = [slice(2 * j * s, (2 * j + 1) * s) for j in range(lc // (2 * s))]
                take = lambda x: jnp.concatenate([x[sl, :] for sl in odd], axis=0)
                rh = lax.broadcasted_iota(I32, (lc // 2, lc), 0)
                ch = lax.broadcasted_iota(I32, (lc // 2, lc), 1)
                off_odd = (ch // s) == 2 * (rh // s)
                zero = jnp.zeros((s, lc), F32)
                new = []
                for a, t, tb in zip(a_lows, tinvs, tbs):
                    w_odd = _dot(jnp.where(off_odd, take(a), 0.0).astype(BF16), tb)
                    w_full = jnp.concatenate(
                        [piece for j in range(len(odd)) for piece in (zero, w_odd[j * s:(j + 1) * s, :])], axis=0)
                    t_odd = take(t)
                    t_odd = t_odd - _dot(t_odd.astype(BF16), w_full.astype(BF16))
                    new.append(jnp.concatenate(
                        [piece for j, ev in enumerate(even) for piece in (t[ev, :], t_odd[j * s:(j + 1) * s, :])],
                        axis=0))
                tinvs = new
        rhss = [jnp.concatenate([vb, kb * eg], axis=-1).astype(BF16) for vb, kb, eg in zip(vs, kbs, egs)]
        sols = [_dot(t.astype(BF16), r) for t, r in zip(tinvs, rhss)]
        sts = [s_scr[h] for h in heads]
        stbs = [st.astype(BF16) for st in sts]
        ubs = [(sol[:, :DV_C] - _dot(sol[:, DV_C:].astype(BF16), stb)).astype(BF16) for sol, stb in zip(sols, stbs)]
        qs = [q_ref[:, sl] for sl in sls]
        qks = [(_dot_nt(q, k) * decay).astype(BF16) for q, k, decay in zip(qs, ks, decays)]
        os_ = [_dot((q.astype(F32) * eg).astype(BF16), stb) + _dot(qk, ub)
               for q, eg, stb, qk, ub in zip(qs, egs, stbs, qks, ubs)]
        for h, st, kf, gcol, ub in zip(heads, sts, kfs, gcols, ubs):
            g_last = gcol[lc - 1:lc, :]
            kdec = (kf * jnp.exp(g_last - gcol)).astype(BF16)
            s_scr[h] = st * jnp.exp(g_last) + _dot_tn(kdec, ub)
        for sl, o in zip(sls, os_):
            ms = jnp.mean(o * o, axis=-1, keepdims=True)
            y = o * lax.rsqrt(ms + LN_EPS) * ng_ref[...] * _silu(z_ref[:, sl])
            o_ref[:, sl] = y.astype(o_ref.dtype)

    @pl.when(c == pl.num_programs(1) - 1)
    def _():
        sout_ref[0] = s_scr[...]


def _gdn(qkvn, proj1, ab, a_log, dt_bias, gdn_norm_g, state0, n_batch, lc, l_real):
    T = qkvn.shape[0]
    nc = T // (n_batch * lc)
    assert nc * n_batch * lc == T and (nc == 1 or l_real == lc)
    pad = lambda v: jnp.pad(v.astype(F32).reshape(1, H_C), ((0, 0), (0, 128 - H_C)))
    return pl.pallas_call(
        functools.partial(_gdn_kernel, lc=lc, l_real=l_real),
        out_shape=[jax.ShapeDtypeStruct((T, C_W), BF16),
                   jax.ShapeDtypeStruct((n_batch, H_C, DK_C, DV_C), F32)],
        grid=(n_batch, nc),
        in_specs=[pl.BlockSpec((lc, C_W), lambda b, c: (b * nc + c, 0)),
                  pl.BlockSpec((lc, C_W), lambda b, c: (b * nc + c, 1)),
                  pl.BlockSpec((lc, C_W), lambda b, c: (b * nc + c, 2)),
                  pl.BlockSpec((lc, C_W), lambda b, c: (b * nc + c, 3)),
                  pl.BlockSpec((lc, 128), lambda b, c: (b * nc + c, 0)),
                  pl.BlockSpec((1, 128), lambda b, c: (0, 0)),
                  pl.BlockSpec((1, 128), lambda b, c: (0, 0)),
                  pl.BlockSpec((1, DV_C), lambda b, c: (0, 0)),
                  pl.BlockSpec((1, H_C, DK_C, DV_C), lambda b, c: (b, 0, 0, 0))],
        out_specs=[pl.BlockSpec((lc, C_W), lambda b, c: (b * nc + c, 0)),
                   pl.BlockSpec((1, H_C, DK_C, DV_C), lambda b, c: (b, 0, 0, 0))],
        scratch_shapes=[pltpu.VMEM((H_C, DK_C, DV_C), F32)],
        compiler_params=_cparams(("parallel", "arbitrary")),
        name="gated_delta",
    )(qkvn, qkvn, qkvn, proj1, ab, pad(a_log), pad(dt_bias), gdn_norm_g.reshape(1, DV_C), state0)


def _router_kernel(x_ref, wt_ref, rt_ref, rtt_ref, cnt_ref):
    tt = x_ref.shape[0]
    logits = lax.dot_general(wt_ref[...], x_ref[...], (((1,), (1,)), ((), ())),
                             preferred_element_type=F32, precision=lax.Precision.HIGHEST)
    eid = lax.broadcasted_iota(I32, (N_EXPERTS, tt), 0).astype(F32)
    m1 = jnp.max(logits, axis=0, keepdims=True)
    e1 = jnp.min(jnp.where(logits == m1, eid, float(N_EXPERTS)), axis=0, keepdims=True)
    rest = jnp.where(eid == e1, -jnp.inf, logits)
    m2 = jnp.max(rest, axis=0, keepdims=True)
    e2 = jnp.min(jnp.where(rest == m2, eid, float(N_EXPERTS)), axis=0, keepdims=True)
    ev = jnp.exp(m2 - m1)
    g1 = 1.0 / (1.0 + ev)
    g2 = ev / (1.0 + ev)
    hit1 = eid == e1
    hit2 = eid == e2
    member = jnp.where(hit1 | hit2, 1.0, 0.0)
    before = (lax.broadcasted_iota(I32, (tt, tt), 0) < lax.broadcasted_iota(I32, (tt, tt), 1))
    rank = _dot(member.astype(BF16), jnp.where(before, 1.0, 0.0).astype(BF16))
    r1 = jnp.sum(jnp.where(hit1, rank, 0.0), axis=0, keepdims=True)
    r2 = jnp.sum(jnp.where(hit2, rank, 0.0), axis=0, keepdims=True)
    rows = jnp.concatenate([e1, e2, r1, r2, g1, g2, jnp.zeros((2, tt), F32)], axis=0)
    rt_ref[0] = rows
    rtt_ref[0] = jnp.concatenate([rows, jnp.zeros((128 - 8, tt), F32)], axis=0).T
    cnt = jnp.sum(member, axis=1, keepdims=True).astype(I32)
    cnt_ref[0] = jnp.broadcast_to(cnt, (N_EXPERTS, 128))


def _router(x, w_router):
    T = x.shape[0]
    tt = MOE_TOK_TILE
    nt = T // tt
    assert T % tt == 0
    return pl.pallas_call(
        _router_kernel,
        out_shape=[jax.ShapeDtypeStruct((nt, 8, tt), F32),
                   jax.ShapeDtypeStruct((nt, tt, 128), F32),
                   jax.ShapeDtypeStruct((nt, N_EXPERTS, 128), I32)],
        grid=(nt,),
        in_specs=[pl.BlockSpec((tt, D_MODEL), lambda i: (i, 0)),
                  pl.BlockSpec((N_EXPERTS, D_MODEL), lambda i: (0, 0))],
        out_specs=[pl.BlockSpec((1, 8, tt), lambda i: (i, 0, 0)),
                   pl.BlockSpec((1, tt, 128), lambda i: (i, 0, 0)),
                   pl.BlockSpec((1, N_EXPERTS, 128), lambda i: (i, 0, 0))],
        compiler_params=_cparams(("parallel",)),
        name="moe_router",
    )(x, w_router.T)


def _moe_tables(cnt, n_tiles, group):
    E = N_EXPERTS
    nb_max = (2 * n_tiles * MOE_TOK_TILE) // MOE_SUB + n_tiles * E
    nb_tot = -(-(nb_max + E * (group - 1)) // group) * group
    nblk = (cnt + MOE_SUB - 1) // MOE_SUB
    nbe = jnp.sum(nblk, axis=0)
    nbe_pad = (nbe + group - 1) // group * group
    ends_e = jnp.cumsum(nbe_pad)
    base = ends_e - nbe_pad
    dest0 = base[None, :] + jnp.cumsum(nblk, axis=0) - nblk
    ends_t = jnp.cumsum(nblk, axis=1)
    kk = jnp.arange(MOE_SLOTS, dtype=I32)
    slot_e = jnp.minimum(jnp.sum(ends_t[:, None, :] <= kk[None, :, None], axis=-1), E - 1).astype(I32)
    slot_valid = kk[None, :] < ends_t[:, -1:]
    start_t = jnp.take_along_axis(ends_t - nblk, slot_e, axis=1)
    slot_j = jnp.where(slot_valid, kk[None, :] - start_t, 0)
    slot_dest = jnp.where(slot_valid, jnp.take_along_axis(dest0, slot_e, axis=1) + slot_j, 0)
    npad = nbe_pad - nbe
    pm = jnp.arange(E * (group - 1), dtype=I32)
    pe, pt = pm // max(group - 1, 1), pm % max(group - 1, 1)
    pad_valid = pt < npad[pe]
    pad_dest = jnp.where(pad_valid, base[pe] + nbe[pe] + pt, 0)
    n_rt = nb_tot // group
    r0 = jnp.arange(n_rt, dtype=I32) * group
    rt_valid = r0 < ends_e[-1]
    rt_e = jnp.minimum(jnp.sum(ends_e[None, :] <= r0[:, None], axis=-1), E - 1)
    last_e = jnp.minimum(jnp.sum(ends_e <= ends_e[-1] - 1), E - 1)
    rt_e = jnp.where(rt_valid, rt_e, last_e)
    i32 = lambda a: a.astype(I32).reshape(-1)
    return dict(nb_tot=nb_tot, n_rt=n_rt, slot_e=i32(slot_e), slot_j=i32(slot_j), slot_dest=i32(slot_dest),
                slot_valid=i32(slot_valid), pad_dest=i32(pad_dest), pad_valid=i32(pad_valid),
                rt_e=i32(rt_e), rt_valid=i32(rt_valid), n_valid_rt=i32(ends_e[-1] // group))


def _moe_gather_kernel(se_ref, sj_ref, sd_ref, sv_ref, pd_ref, pv_ref, nv_ref, x_ref, rt_ref, xg_hbm,
                       buf, zbuf, sem, zsem, *, n_pad, n_rt, rows):
    i = pl.program_id(0)
    rt = rt_ref[0]
    height = MOE_SLOTS * MOE_SUB
    row = lax.broadcasted_iota(I32, (height, 1), 0)
    row_e = jnp.full((height, 1), -1.0, F32)
    row_r = (row % MOE_SUB).astype(F32)
    for k in range(MOE_SLOTS):
        idx = i * MOE_SLOTS + k
        here = (row // MOE_SUB) == k
        row_e = jnp.where(here, jnp.where(sv_ref[idx] == 1, se_ref[idx], -1).astype(F32), row_e)
        row_r = row_r + jnp.where(here, (sj_ref[idx] * MOE_SUB).astype(F32), 0.0)
    sel = ((rt[0:1, :] == row_e) & (rt[2:3, :] == row_r)) | ((rt[1:2, :] == row_e) & (rt[3:4, :] == row_r))
    buf[...] = _dot(jnp.where(sel, 1.0, 0.0).astype(BF16), x_ref[...]).astype(BF16)

    def slot_copy(k):
        dst = pl.multiple_of(sd_ref[i * MOE_SLOTS + k] * MOE_SUB, MOE_SUB)
        return pltpu.make_async_copy(buf.at[pl.ds(k * MOE_SUB, MOE_SUB)], xg_hbm.at[pl.ds(dst, MOE_SUB)], sem.at[k])

    for k in range(MOE_SLOTS):
        @pl.when(sv_ref[i * MOE_SLOTS + k] == 1)
        def _(k=k):
            slot_copy(k).start()

    for k in range(MOE_SLOTS):
        @pl.when(sv_ref[i * MOE_SLOTS + k] == 1)
        def _(k=k):
            slot_copy(k).wait()

    def pad_copy(m):
        dst = pl.multiple_of(pd_ref[m] * MOE_SUB, MOE_SUB)
        return pltpu.make_async_copy(zbuf.at[pl.ds(0, MOE_SUB)], xg_hbm.at[pl.ds(dst, MOE_SUB)], zsem.at[0])

    def tail_copy(r):
        dst = pl.multiple_of(r * rows, rows)
        return pltpu.make_async_copy(zbuf, xg_hbm.at[pl.ds(dst, rows)], zsem.at[1])

    @pl.when(i == pl.num_programs(0) - 1)
    def _():
        zbuf[...] = jnp.zeros_like(zbuf)

        def pad_start(m, carry):
            @pl.when(pv_ref[m] == 1)
            def _():
                pad_copy(m).start()
            return carry

        def pad_wait(m, carry):
            @pl.when(pv_ref[m] == 1)
            def _():
                pad_copy(m).wait()
            return carry

        def tail_start(r, carry):
            tail_copy(r).start()
            return carry

        def tail_wait(r, carry):
            tail_copy(r).wait()
            return carry

        lax.fori_loop(0, n_pad, pad_start, 0)
        lax.fori_loop(nv_ref[0], n_rt, tail_start, 0)
        lax.fori_loop(0, n_pad, pad_wait, 0)
        lax.fori_loop(nv_ref[0], n_rt, tail_wait, 0)


def _moe_gather(xb, rt, tb, rows):
    nb = tb["nb_tot"]
    tt = MOE_TOK_TILE
    nt = xb.shape[0] // tt
    n_pad = tb["pad_dest"].shape[0]
    return pl.pallas_call(
        functools.partial(_moe_gather_kernel, n_pad=n_pad, n_rt=tb["n_rt"], rows=rows),
        out_shape=jax.ShapeDtypeStruct((nb * MOE_SUB, D_MODEL), BF16),
        grid_spec=pltpu.PrefetchScalarGridSpec(
            num_scalar_prefetch=7,
            grid=(nt,),
            in_specs=[pl.BlockSpec((tt, D_MODEL), lambda i, *_: (i, 0)),
                      pl.BlockSpec((1, 8, tt), lambda i, *_: (i, 0, 0))],
            out_specs=pl.BlockSpec(memory_space=pl.ANY),
            scratch_shapes=[pltpu.VMEM((MOE_SLOTS * MOE_SUB, D_MODEL), BF16),
                            pltpu.VMEM((rows, D_MODEL), BF16),
                            pltpu.SemaphoreType.DMA((MOE_SLOTS,)),
                            pltpu.SemaphoreType.DMA((2,))]),
        compiler_params=_cparams(("arbitrary",)),
        name="moe_gather",
    )(tb["slot_e"], tb["slot_j"], tb["slot_dest"], tb["slot_valid"], tb["pad_dest"], tb["pad_valid"],
      tb["n_valid_rt"], xb, rt)


def _moe_ffn_kernel(e_ref, valid_ref, x_ref, wg_ref, wu_ref, wd_ref, y_ref, acc, *, rows):
    r = pl.program_id(0)
    f = pl.program_id(1)
    nf = pl.num_programs(1)
    ok = valid_ref[r] == 1

    @pl.when(jnp.logical_and(ok, f == 0))
    def _():
        acc[...] = jnp.zeros_like(acc)

    @pl.when(ok)
    def _():
        xb = x_ref[...]
        g = _dot(xb, wg_ref[0])
        u = _dot(xb, wu_ref[0])
        acc[...] += _dot((_silu(g) * u).astype(BF16), wd_ref[0])

    @pl.when(jnp.logical_and(ok, f == nf - 1))
    def _():
        y_ref[...] = acc[...].astype(y_ref.dtype)

    @pl.when(jnp.logical_and(jnp.logical_not(ok), f == nf - 1))
    def _():
        y_ref[...] = jnp.zeros_like(y_ref)


def _moe_ffn(xg, w_gu_bf, w_d_bf, tb, rows):
    n_rt = tb["n_rt"]
    nf = D_FF_E // MOE_FF_BLK
    fb = MOE_FF_BLK

    def fsel(r, f, v):
        return jnp.where(v[r] == 1, f, nf - 1)

    return pl.pallas_call(
        functools.partial(_moe_ffn_kernel, rows=rows),
        out_shape=jax.ShapeDtypeStruct(xg.shape, BF16),
        grid_spec=pltpu.PrefetchScalarGridSpec(
            num_scalar_prefetch=2,
            grid=(n_rt, nf),
            in_specs=[pl.BlockSpec((rows, D_MODEL), lambda r, f, e, v: (r, 0)),
                      pl.BlockSpec((1, D_MODEL, fb), lambda r, f, e, v: (e[r], 0, fsel(r, f, v))),
                      pl.BlockSpec((1, D_MODEL, fb), lambda r, f, e, v: (e[r], 0, nf + fsel(r, f, v))),
                      pl.BlockSpec((1, fb, D_MODEL), lambda r, f, e, v: (e[r], fsel(r, f, v), 0))],
            out_specs=pl.BlockSpec((rows, D_MODEL), lambda r, f, e, v: (r, 0)),
            scratch_shapes=[pltpu.VMEM((rows, D_MODEL), F32)]),
        compiler_params=_cparams(("arbitrary", "arbitrary")),
        name="moe_grouped_ffn",
    )(tb["rt_e"], tb["rt_valid"], xg, w_gu_bf, w_gu_bf, w_d_bf)


def _moe_combine_kernel(se_ref, sj_ref, sd_ref, sv_ref, *refs):
    y_refs = refs[:MOE_SLOTS]
    rtt_ref, x_ref, g_ref, b_ref, o_ref, ybuf = refs[MOE_SLOTS:]
    i = pl.program_id(0)
    width = MOE_SLOTS * MOE_SUB
    lane = lax.broadcasted_iota(I32, (1, width), 1)
    lane_e = jnp.full((1, width), -1.0, F32)
    lane_r = (lane % MOE_SUB).astype(F32)
    for k in range(MOE_SLOTS):
        idx = i * MOE_SLOTS + k
        valid = sv_ref[idx] == 1
        yk = y_refs[k][...]
        ybuf[k * MOE_SUB:(k + 1) * MOE_SUB, :] = jnp.where(valid, yk, jnp.zeros_like(yk))
        here = (lane // MOE_SUB) == k
        lane_e = jnp.where(here, jnp.where(valid, se_ref[idx], -1).astype(F32), lane_e)
        lane_r = lane_r + jnp.where(here, (sj_ref[idx] * MOE_SUB).astype(F32), 0.0)
    rtt = rtt_ref[0]
    a1 = jnp.where((rtt[:, 0:1] == lane_e) & (rtt[:, 2:3] == lane_r), 1.0, 0.0).astype(BF16)
    a2 = jnp.where((rtt[:, 1:2] == lane_e) & (rtt[:, 3:4] == lane_r), 1.0, 0.0).astype(BF16)
    yb = ybuf[...]
    ff = rtt[:, 4:5] * _dot(a1, yb) + rtt[:, 5:6] * _dot(a2, yb)
    o_ref[...] = _layer_norm_rows(DN_ALPHA * x_ref[...] + ff, g_ref[...], b_ref[...])


def _moe_combine(yg, rtt, x, g, b, tb):
    tt = MOE_TOK_TILE
    T = x.shape[0]

    def y_map(i, se, sj, sd, sv, *, k):
        return (sd[i * MOE_SLOTS + k], 0)

    y_specs = [pl.BlockSpec((MOE_SUB, D_MODEL), functools.partial(y_map, k=k)) for k in range(MOE_SLOTS)]
    return pl.pallas_call(
        _moe_combine_kernel,
        out_shape=jax.ShapeDtypeStruct((T, D_MODEL), F32),
        grid_spec=pltpu.PrefetchScalarGridSpec(
            num_scalar_prefetch=4,
            grid=(T // tt,),
            in_specs=y_specs + [pl.BlockSpec((1, tt, 128), lambda i, *_: (i, 0, 0)),
                                pl.BlockSpec((tt, D_MODEL), lambda i, *_: (i, 0)),
                                pl.BlockSpec((1, D_MODEL), lambda i, *_: (0, 0)),
                                pl.BlockSpec((1, D_MODEL), lambda i, *_: (0, 0))],
            out_specs=pl.BlockSpec((tt, D_MODEL), lambda i, *_: (i, 0)),
            scratch_shapes=[pltpu.VMEM((MOE_SLOTS * MOE_SUB, D_MODEL), BF16)]),
        compiler_params=_cparams(("arbitrary",)),
        name="moe_combine",
    )(tb["slot_e"], tb["slot_j"], tb["slot_dest"], tb["slot_valid"], *([yg] * MOE_SLOTS), rtt, x,
      g.reshape(1, D_MODEL), b.reshape(1, D_MODEL))


def _moe(x, xb, w_router, w_gu_bf, w_d_bf, g, b, rows):
    T = x.shape[0]
    nt = T // MOE_TOK_TILE
    rt, rtt, cnt = _router(x, w_router)
    tb = _moe_tables(cnt[:, :, 0], nt, rows // MOE_SUB)
    xg = _moe_gather(xb, rt, tb, rows)
    yg = _moe_ffn(xg, w_gu_bf, w_d_bf, tb, rows)
    return _moe_combine(yg, rtt, x, g, b, tb)


def _trunk(x, pos_rows, n_batch, mem_k, mem_v, ret_state, past_k, past_v, conv_state, gdn_state, p, cfg):
    T = x.shape[0]
    per = T // n_batch
    tm = cfg["tm"]
    tm_mm = cfg["tm_mm"]
    ln_g, ln_b = p["ln_g"], p["ln_b"]

    qka, va, ga, qb, kb, vb = _proj0(x.astype(BF16), p["w_in0"], _rope_tables(pos_rows), tm_mm)
    oa, ret_new = _retention(qka, va, ga, ret_state, p["ret_norm_g"], n_batch, cfg["ret_lc"])
    lp = p["diff_lambda"].astype(F32)
    lam = (jnp.exp(jnp.sum(lp[0] * lp[1])) - jnp.exp(jnp.sum(lp[2] * lp[3])) + DIFF_LAMBDA_INIT).reshape(1)
    if past_k is None:
        ob = _diff_prompt(qb, kb, vb, lam, p["diff_norm_g"], n_batch, per, cfg["diff_tq"])
    else:
        ob = _diff_step(qb, kb, vb, past_k, past_v, lam, p["diff_norm_g"], n_batch, per)
    x, q = _matmul_deepnorm([oa, ob], [p["w_out0"][:A_V], p["w_out0"][A_V:]], x, ln_g[0, 0], ln_b[0, 0], tm,
                            w_next=p["w_xq"][0], next_scale=X_HD ** -0.5, name="out0_deepnorm_xq")
    xa = _xattn(q, mem_k[0], mem_v[0], n_batch, cfg["x_tq"])
    x, xb = _matmul_deepnorm([xa], [p["w_xo"][0]], x, ln_g[0, 1], ln_b[0, 1], tm, name="xo0_deepnorm")
    hmid = _swiglu_up(xb, p["w_ffn_gu"], tm, cfg["ffn_tn"])
    x, xb = _matmul_deepnorm([hmid], [p["w_ffn_d"]], x, ln_g[0, 2], ln_b[0, 2], tm, name="ffn_down_deepnorm")

    proj1 = _matmul(xb, p["w_in1_main"][None], F32, tm_mm, 1024, name="proj1")[0]
    ab = _matmul(xb, p["w_in1_ab"][None], F32, tm_mm, 128, name="proj1_gates")[0]
    qkv3 = proj1.reshape(n_batch, per, 4 * C_W)[:, :, :3 * C_W]
    conv_new = qkv3[:, per - (CONV_W - 1):, :]
    state8 = jnp.pad(conv_state.astype(F32), ((0, 0), (8 - (CONV_W - 1), 0), (0, 0)))
    qkvn = _gdn_conv(proj1, state8, p["conv_w"], n_batch, cfg["conv_tm"])
    lc = cfg["gdn_lc"]
    if per < lc:
        padrows = lambda a: jnp.pad(a.reshape(n_batch, per, a.shape[-1]),
                                    ((0, 0), (0, lc - per), (0, 0))).reshape(n_batch * lc, a.shape[-1])
        og, gdn_new = _gdn(padrows(qkvn), padrows(proj1), padrows(ab), p["a_log"], p["dt_bias"],
                           p["gdn_norm_g"], gdn_state, n_batch, lc, per)
        og = og.reshape(n_batch, lc, C_W)[:, :per].reshape(T, C_W)
    else:
        og, gdn_new = _gdn(qkvn, proj1, ab, p["a_log"], p["dt_bias"], p["gdn_norm_g"], gdn_state,
                           n_batch, lc, lc)
    x, q = _matmul_deepnorm([og], [p["w_out1"]], x, ln_g[1, 0], ln_b[1, 0], tm,
                            w_next=p["w_xq"][1], next_scale=X_HD ** -0.5, name="out1_deepnorm_xq")
    xa = _xattn(q, mem_k[1], mem_v[1], n_batch, cfg["x_tq"])
    x, xb = _matmul_deepnorm([xa], [p["w_xo"][1]], x, ln_g[1, 1], ln_b[1, 1], tm, name="xo1_deepnorm")
    x = _moe(x, xb, p["w_router"], p["w_moe_gu"], p["w_moe_d"], ln_g[1, 2], ln_b[1, 2], cfg["moe_rows"])
    return x, kb, vb, ret_new, conv_new, gdn_new


PROMPT_CFG = dict(tm=512, tm_mm=1024, ret_lc=256, diff_tq=512, x_tq=512, ffn_tn=1408, conv_tm=256,
                  gdn_lc=256, moe_rows=1024)
STEP_CFG = dict(tm=512, tm_mm=512, ret_lc=16, diff_tq=16, x_tq=16, ffn_tn=1408, conv_tm=16,
                gdn_lc=128, moe_rows=256)


def kernel(x_prompt, x_sample, cache_diff_k, cache_diff_v, state_ret, state_gdn_conv, state_gdn, cache_mem_k, cache_mem_v, mem_prompt, w_in0, ret_norm_g, diff_lambda, diff_norm_g, w_out0, w_in1, conv_w, a_log, dt_bias, gdn_norm_g, w_out1, w_xq, w_xkv, w_xo, w_ffn_gu, w_ffn_d, w_router, w_moe_gu, w_moe_d, ln_g, ln_b):
    B, S, _ = x_prompt.shape
    DB, L, _ = x_sample.shape
    P = cache_diff_k.shape[1]
    bf = lambda w: w.astype(BF16)
    w_ab = jnp.pad(w_in1[:, 4 * C_W:], ((0, 0), (0, 128 - 2 * H_C)))
    p = dict(w_in0=bf(w_in0), ret_norm_g=ret_norm_g, diff_lambda=diff_lambda, diff_norm_g=diff_norm_g,
             w_out0=bf(w_out0), w_in1_main=bf(w_in1[:, :4 * C_W]), w_in1_ab=bf(w_ab), conv_w=conv_w,
             a_log=a_log, dt_bias=dt_bias, gdn_norm_g=gdn_norm_g, w_out1=bf(w_out1), w_xq=bf(w_xq),
             w_xo=bf(w_xo), w_ffn_gu=bf(w_ffn_gu), w_ffn_d=bf(w_ffn_d), w_router=w_router,
             w_moe_gu=bf(w_moe_gu), w_moe_d=bf(w_moe_d), ln_g=ln_g, ln_b=ln_b)

    mem = mem_prompt.reshape(B * N_MEM, D_MODEL)
    mk_p = _matmul(mem, bf(w_xkv[:, :, :D_MODEL]), F32, 1024, 1024, name="mem_k")
    mv_p = _matmul(mem, bf(w_xkv[:, :, D_MODEL:]), F32, 1024, 1024, name="mem_v")

    y_p, dk_p, dv_p, ret_p, conv_p, gdn_p = _trunk(
        x_prompt.reshape(B * S, D_MODEL), jnp.arange(S), B, mk_p, mv_p,
        jnp.zeros((B, H_A, DK_A, DV_A), F32), None, None,
        jnp.zeros((B, CONV_W - 1, 3 * C_W), F32), jnp.zeros((B, H_C, DK_C, DV_C), F32), p, PROMPT_CFG)

    pos_s = jnp.tile(P + jnp.arange(L), DB)
    y_s, dk_s, dv_s, ret_s, conv_s, gdn_s = _trunk(
        x_sample.reshape(DB * L, D_MODEL), pos_s, DB,
        cache_mem_k.reshape(DEPTH, DB * N_MEM, D_MODEL), cache_mem_v.reshape(DEPTH, DB * N_MEM, D_MODEL),
        state_ret.astype(F32), cache_diff_k.reshape(DB * P * H_B, DV_B), cache_diff_v.reshape(DB * P * H_B, DV_B),
        state_gdn_conv, state_gdn.astype(F32), p, STEP_CFG)

    shape5 = (DEPTH, B, N_MEM, X_HEADS, X_HD)
    return (y_p.reshape(B, S, D_MODEL), y_s.reshape(DB, L, D_MODEL),
            mk_p.reshape(shape5), mv_p.reshape(shape5),
            dk_p.reshape(B, S, H_B, 2 * DK_B), dv_p.reshape(B, S, H_B, DV_B),
            ret_p, conv_p, gdn_p,
            dk_s.reshape(DB, L, H_B, 2 * DK_B), dv_s.reshape(DB, L, H_B, DV_B),
            ret_s, conv_s, gdn_s)
```

```python
import functools
import math

import jax
import jax.numpy as jnp
from jax import lax
from jax.experimental import pallas as pl
from jax.experimental.pallas import tpu as pltpu

F32 = jnp.float32
BF16 = jnp.bfloat16
I32 = jnp.int32

D_MODEL = 1024
DEPTH = 2
CHUNK = 64
H_A, DK_A, DV_A = 4, 64, 128
RET_THETA = 10000.0
H_B, DK_B = 4, 64
DV_B = 2 * DK_B
ROT_B = DK_B // 4
ROPE_THETA = 500000.0
DIFF_LAMBDA_INIT = 0.8 - 0.6 * math.exp(-0.3 * 0)
H_C, DK_C, DV_C = 8, 128, 128
C_W = H_C * DV_C
CONV_W = 4
N_MEM = 256
X_HEADS = 4
X_HD = D_MODEL // X_HEADS
D_FF = 2816
N_EXPERTS = 8
D_FF_E = 3584
DN_ALPHA = (2 * DEPTH) ** 0.25
LN_EPS = 1e-5
A_QK = H_A * DK_A
A_V = H_A * DV_A
B_QK = H_B * 2 * DK_B
B_V = H_B * DV_B
PROJ0 = 2 * A_QK + 2 * A_V + 2 * B_QK + B_V

VMEM_LIMIT_V7X = 52 * 1024 * 1024
NEG_BIG = -1e30

MOE_TOK_TILE = 512
MOE_SUB = 32
MOE_FF_BLK = 512
MOE_SLOTS = 2 * MOE_TOK_TILE // MOE_SUB + N_EXPERTS
GDN_HEAD_GROUP = 4
CONV_ROWS = 32


def _cparams(sem):
    return pltpu.CompilerParams(dimension_semantics=sem, vmem_limit_bytes=VMEM_LIMIT_V7X)


def _dot(a, b):
    return jnp.dot(a, b, preferred_element_type=F32)


def _dot_nt(a, b):
    return lax.dot_general(a, b, (((1,), (1,)), ((), ())), preferred_element_type=F32)


def _dot_tn(a, b):
    return lax.dot_general(a, b, (((0,), (0,)), ((), ())), preferred_element_type=F32)


def _silu(x):
    return x * (1.0 / (1.0 + jnp.exp(-x)))


def _layer_norm_rows(y, g, b):
    mu = jnp.mean(y, axis=-1, keepdims=True)
    d = y - mu
    var = jnp.mean(d * d, axis=-1, keepdims=True)
    return d * lax.rsqrt(var + LN_EPS) * g + b


def _mm_kernel(x_ref, w_ref, o_ref, *, scale):
    acc = _dot(x_ref[...].astype(BF16), w_ref[0])
    if scale != 1.0:
        acc = acc * scale
    o_ref[0] = acc.astype(o_ref.dtype)


def _matmul(x, w, out_dtype, tm, tn, scale=1.0, name="matmul"):
    M, K = x.shape
    G, _, N = w.shape
    tm = min(tm, M)
    tn = min(tn, N)
    assert M % tm == 0 and N % tn == 0
    return pl.pallas_call(
        functools.partial(_mm_kernel, scale=scale),
        out_shape=jax.ShapeDtypeStruct((G, M, N), out_dtype),
        grid=(G, M // tm, N // tn),
        in_specs=[pl.BlockSpec((tm, K), lambda g, i, j: (i, 0)),
                  pl.BlockSpec((1, K, tn), lambda g, i, j: (g, 0, j))],
        out_specs=pl.BlockSpec((1, tm, tn), lambda g, i, j: (g, i, j)),
        compiler_params=_cparams(("parallel", "parallel", "arbitrary")),
        name=name,
    )(x, w)


def _mm_dn_kernel(*refs, n_in, next_scale):
    xs = refs[:n_in]
    ws = refs[n_in:2 * n_in]
    rest = refs[2 * n_in:]
    if next_scale is None:
        r_ref, g_ref, b_ref, o_ref, ob_ref = rest
    else:
        r_ref, g_ref, b_ref, wn_ref, o_ref, ob_ref = rest
    acc = DN_ALPHA * r_ref[...]
    for x_ref, w_ref in zip(xs, ws):
        acc = acc + _dot(x_ref[...].astype(BF16), w_ref[...])
    y = _layer_norm_rows(acc, g_ref[...], b_ref[...])
    o_ref[...] = y
    if next_scale is None:
        ob_ref[...] = y.astype(BF16)
    else:
        ob_ref[...] = (_dot(y.astype(BF16), wn_ref[...]) * next_scale).astype(BF16)


def _matmul_deepnorm(xs, ws, resid, g, b, tm, w_next=None, next_scale=None, name="matmul_deepnorm"):
    M = resid.shape[0]
    tm = min(tm, M)
    assert M % tm == 0
    n_in = len(xs)
    in_specs = [pl.BlockSpec((tm, x.shape[1]), lambda i: (i, 0)) for x in xs]
    in_specs += [pl.BlockSpec(w.shape, lambda i: (0, 0)) for w in ws]
    in_specs += [pl.BlockSpec((tm, D_MODEL), lambda i: (i, 0)),
                 pl.BlockSpec((1, D_MODEL), lambda i: (0, 0)),
                 pl.BlockSpec((1, D_MODEL), lambda i: (0, 0))]
    args = [*xs, *ws, resid, g.reshape(1, D_MODEL), b.reshape(1, D_MODEL)]
    if w_next is not None:
        in_specs.append(pl.BlockSpec(w_next.shape, lambda i: (0, 0)))
        args.append(w_next)
    return pl.pallas_call(
        functools.partial(_mm_dn_kernel, n_in=n_in, next_scale=next_scale),
        out_shape=[jax.ShapeDtypeStruct((M, D_MODEL), F32), jax.ShapeDtypeStruct((M, D_MODEL), BF16)],
        grid=(M // tm,),
        in_specs=in_specs,
        out_specs=[pl.BlockSpec((tm, D_MODEL), lambda i: (i, 0))] * 2,
        compiler_params=_cparams(("parallel",)),
        name=name,
    )(*args)


def _swap_halves(x, group, half):
    n = x.shape[-1]
    lane = lax.broadcasted_iota(I32, x.shape, x.ndim - 1) % group
    up = pltpu.roll(x, n - half, x.ndim - 1)
    dn = pltpu.roll(x, half, x.ndim - 1)
    return jnp.where(lane < half, up, dn)


def _store_rows(o_ref, y, by_head):
    if not by_head:
        o_ref[...] = y.astype(o_ref.dtype)
        return
    tm = y.shape[0]
    for h in range(H_B):
        o_ref[pl.ds(h, tm, stride=H_B), :] = y[:, h * DV_B:(h + 1) * DV_B].astype(o_ref.dtype)


def _proj_plain_kernel(x_ref, w_ref, o_ref, *, by_head):
    _store_rows(o_ref, _dot(x_ref[...], w_ref[...]), by_head)


def _proj_rot_kernel(x_ref, w_ref, c_ref, s_ref, o_ref, *, group, half, by_head):
    acc = _dot(x_ref[...], w_ref[...])
    _store_rows(o_ref, acc * c_ref[...] + _swap_halves(acc, group, half) * s_ref[...], by_head)


def _proj0_block(xb, w, out_dtype, tm, rot=None, by_head=False, name="proj0"):
    T = xb.shape[0]
    blk = w.shape[1]
    tm = min(tm, T)
    row = lambda i: (i, 0)
    in_specs = [pl.BlockSpec((tm, D_MODEL), row), pl.BlockSpec((D_MODEL, blk), lambda i: (0, 0))]
    args = [xb, w]
    if rot is None:
        body = functools.partial(_proj_plain_kernel, by_head=by_head)
    else:
        cos, sin, group, half = rot
        P = cos.shape[0]
        tm = min(tm, P)
        assert P % tm == 0
        npb = P // tm
        in_specs[0] = pl.BlockSpec((tm, D_MODEL), row)
        in_specs += [pl.BlockSpec((tm, blk), lambda i: (i % npb, 0))] * 2
        args += [cos, sin]
        body = functools.partial(_proj_rot_kernel, group=group, half=half, by_head=by_head)
    assert T % tm == 0
    if by_head:
        assert blk == H_B * DV_B
        out_shape = jax.ShapeDtypeStruct((T * H_B, DV_B), out_dtype)
        out_spec = pl.BlockSpec((tm * H_B, DV_B), row)
    else:
        out_shape = jax.ShapeDtypeStruct((T, blk), out_dtype)
        out_spec = pl.BlockSpec((tm, blk), row)
    return pl.pallas_call(
        body,
        out_shape=out_shape,
        grid=(T // tm,),
        in_specs=in_specs,
        out_specs=out_spec,
        compiler_params=_cparams(("parallel",)),
        name=name,
    )(*args)


def _rope_tables(pos):
    pos = pos.astype(F32)
    inv = RET_THETA ** (-jnp.arange(0, DK_A, 2, dtype=F32) / DK_A)
    ang = pos[:, None] * inv[None, :]
    c, s = jnp.cos(ang), jnp.sin(ang)
    c64 = jnp.concatenate([c, c], axis=-1)
    s64 = jnp.concatenate([-s, s], axis=-1)
    kscale = DK_A ** -0.5
    ca = jnp.concatenate([jnp.tile(c64, (1, H_A)), jnp.tile(c64, (1, H_A)) * kscale], axis=-1)
    sa = jnp.concatenate([jnp.tile(s64, (1, H_A)), jnp.tile(s64, (1, H_A)) * kscale], axis=-1)
    invb = ROPE_THETA ** (-jnp.arange(0, ROT_B, 2, dtype=F32) / ROT_B)
    angb = pos[:, None] * invb[None, :]
    cb, sb = jnp.cos(angb), jnp.sin(angb)
    rest = DK_B - ROT_B
    c64b = jnp.concatenate([cb, cb, jnp.ones((pos.shape[0], rest), F32)], axis=-1)
    s64b = jnp.concatenate([-sb, sb, jnp.zeros((pos.shape[0], rest), F32)], axis=-1)
    ck = jnp.tile(c64b, (1, 2 * H_B))
    sk = jnp.tile(s64b, (1, 2 * H_B))
    qscale = DK_B ** -0.5
    return ca, sa, ck * qscale, sk * qscale, ck, sk


def _proj0(xb, w_in0_bf, tables, tm):
    ca, sa, cq, sq, ck, sk = tables
    blk = 512
    assert PROJ0 == 6 * blk
    w = [w_in0_bf[:, j * blk:(j + 1) * blk] for j in range(6)]
    rot_a = (DK_A, DK_A // 2)
    rot_b = (DK_B, ROT_B // 2)
    qka = _proj0_block(xb, w[0], BF16, tm, (ca, sa) + rot_a, name="proj0_qk_ret")
    va = _proj0_block(xb, w[1], BF16, tm, name="proj0_v_ret")
    ga = _proj0_block(xb, w[2], F32, tm, name="proj0_gate_ret")
    qb = _proj0_block(xb, w[3], BF16, tm, (cq, sq) + rot_b, name="proj0_q_diff")
    kb = _proj0_block(xb, w[4], F32, tm, (ck, sk) + rot_b, by_head=True, name="proj0_k_diff")
    vb = _proj0_block(xb, w[5], F32, tm, by_head=True, name="proj0_v_diff")
    return qka, va, ga, qb, kb, vb


def _retention_kernel(qk_ref, v_ref, g_ref, dmat_ref, qdec_ref, kdec_ref, sdec_ref, s0_ref, ng_ref,
                      o_ref, sout_ref, s_scr):
    c = pl.program_id(1)

    @pl.when(c == 0)
    def _():
        s_scr[...] = s0_ref[0]

    for h in range(H_A):
        q = qk_ref[:, h * DK_A:(h + 1) * DK_A]
        k = qk_ref[:, A_QK + h * DK_A:A_QK + (h + 1) * DK_A]
        v = v_ref[:, h * DV_A:(h + 1) * DV_A]
        s = s_scr[h]
        scores = _dot_nt(q, k) * dmat_ref[h]
        o = _dot(scores.astype(BF16), v) + _dot(q, s.astype(BF16)) * qdec_ref[h]
        kd = (k.astype(F32) * kdec_ref[h]).astype(BF16)
        s_scr[h] = s * sdec_ref[h] + _dot_tn(kd, v)
        mu = jnp.mean(o, axis=-1, keepdims=True)
        d = o - mu
        var = jnp.mean(d * d, axis=-1, keepdims=True)
        gate = g_ref[:, h * DV_A:(h + 1) * DV_A]
        y = d * lax.rsqrt(var + LN_EPS) * ng_ref[:, h * DV_A:(h + 1) * DV_A] * _silu(gate)
        o_ref[:, h * DV_A:(h + 1) * DV_A] = y.astype(o_ref.dtype)

    @pl.when(c == pl.num_programs(1) - 1)
    def _():
        sout_ref[0] = s_scr[...]


def _retention_tables(lc):
    lg = jnp.log1p(-jnp.exp2(-5.0 - jnp.arange(H_A, dtype=F32)))
    t = jnp.arange(lc, dtype=F32)
    rel = t[:, None] - t[None, :]
    causal = rel >= 0
    dmat = jnp.where(causal, jnp.exp(lg[:, None, None] * jnp.where(causal, rel, 0.0)), 0.0)
    qdec = jnp.exp(lg[:, None] * (t[None, :] + 1.0))
    kdec = jnp.exp(lg[:, None] * (lc - 1.0 - t[None, :]))
    sdec = jnp.exp(lg * lc)
    return (dmat,
            jnp.broadcast_to(qdec[:, :, None], (H_A, lc, DV_A)),
            jnp.broadcast_to(kdec[:, :, None], (H_A, lc, DK_A)),
            jnp.broadcast_to(sdec[:, None, None], (H_A, DK_A, DV_A)))


def _retention(qka, va, ga, state0, ret_norm_g, n_batch, lc):
    T = qka.shape[0]
    nc = T // (n_batch * lc)
    assert nc * n_batch * lc == T
    dmat, qdec, kdec, sdec = _retention_tables(lc)
    row = lambda b, c: (b * nc + c, 0)
    const3 = lambda b, c: (0, 0, 0)
    return pl.pallas_call(
        _retention_kernel,
        out_shape=[jax.ShapeDtypeStruct((T, A_V), BF16),
                   jax.ShapeDtypeStruct((n_batch, H_A, DK_A, DV_A), F32)],
        grid=(n_batch, nc),
        in_specs=[pl.BlockSpec((lc, 2 * A_QK), row),
                  pl.BlockSpec((lc, A_V), row),
                  pl.BlockSpec((lc, A_V), row),
                  pl.BlockSpec((H_A, lc, lc), const3),
                  pl.BlockSpec((H_A, lc, DV_A), const3),
                  pl.BlockSpec((H_A, lc, DK_A), const3),
                  pl.BlockSpec((H_A, DK_A, DV_A), const3),
                  pl.BlockSpec((1, H_A, DK_A, DV_A), lambda b, c: (b, 0, 0, 0)),
                  pl.BlockSpec((1, A_V), lambda b, c: (0, 0))],
        out_specs=[pl.BlockSpec((lc, A_V), row),
                   pl.BlockSpec((1, H_A, DK_A, DV_A), lambda b, c: (b, 0, 0, 0))],
        scratch_shapes=[pltpu.VMEM((H_A, DK_A, DV_A), F32)],
        compiler_params=_cparams(("parallel", "arbitrary")),
        name="retention",
    )(qka, va, ga, dmat, qdec, kdec, sdec, state0, ret_norm_g.reshape(1, A_V))


def _diff_finish(acc1, l1, acc2, l2, lam, ng):
    o = acc1 * (1.0 / l1) - lam * (acc2 * (1.0 / l2))
    ms = jnp.mean(o * o, axis=-1, keepdims=True)
    return o * lax.rsqrt(ms + LN_EPS) * ng * (1.0 - DIFF_LAMBDA_INIT)


def _diff_prompt_kernel(lam_ref, q_ref, k_ref, v_ref, ng_ref, o_ref, kbf, vbf, *, tq):
    qi = pl.program_id(2)

    @pl.when(qi == 0)
    def _():
        h = pl.program_id(1)
        seq = kbf.shape[0]
        kbf[...] = k_ref[pl.ds(h, seq, stride=H_B), :].astype(BF16)
        vbf[...] = v_ref[pl.ds(h, seq, stride=H_B), :].astype(BF16)

    q1 = q_ref[:, :DK_B]
    q2 = q_ref[:, DK_B:]

    def block(kblk, carry, masked):
        m1, l1, a1, m2, l2, a2 = carry
        off = pl.multiple_of(kblk * tq, tq)
        kk = kbf[pl.ds(off, tq), :]
        vv = vbf[pl.ds(off, tq), :]
        s1 = _dot_nt(q1, kk[:, :DK_B])
        s2 = _dot_nt(q2, kk[:, DK_B:])
        if masked:
            rc = lax.broadcasted_iota(I32, (tq, tq), 0) // CHUNK
            cc = lax.broadcasted_iota(I32, (tq, tq), 1) // CHUNK
            vis = cc <= rc
            s1 = jnp.where(vis, s1, NEG_BIG)
            s2 = jnp.where(vis, s2, NEG_BIG)
        out = []
        for s, m, l, a in ((s1, m1, l1, a1), (s2, m2, l2, a2)):
            mn = jnp.maximum(m, jnp.max(s, axis=-1, keepdims=True))
            alpha = jnp.exp(m - mn)
            p = jnp.exp(s - mn)
            l = alpha * l + jnp.sum(p, axis=-1, keepdims=True)
            a = alpha * a + _dot(p.astype(BF16), vv)
            out += [mn, l, a]
        return tuple(out)

    init = (jnp.full((tq, 1), NEG_BIG, F32), jnp.zeros((tq, 1), F32), jnp.zeros((tq, DV_B), F32)) * 2
    carry = lax.fori_loop(0, qi, lambda kb, c: block(kb, c, False), init)
    m1, l1, a1, m2, l2, a2 = block(qi, carry, True)
    o_ref[...] = _diff_finish(a1, l1, a2, l2, lam_ref[0], ng_ref[...]).astype(o_ref.dtype)


def _diff_prompt(qb, kb, vb, lam, diff_norm_g, n_batch, seq, tq):
    T = qb.shape[0]
    assert seq % tq == 0 and tq % CHUNK == 0
    nq = seq // tq
    return pl.pallas_call(
        functools.partial(_diff_prompt_kernel, tq=tq),
        out_shape=jax.ShapeDtypeStruct((T, B_V), BF16),
        grid_spec=pltpu.PrefetchScalarGridSpec(
            num_scalar_prefetch=1,
            grid=(n_batch, H_B, nq),
            in_specs=[pl.BlockSpec((tq, DV_B), lambda b, h, i, lam: (b * nq + i, h)),
                      pl.BlockSpec((seq * H_B, DV_B), lambda b, h, i, lam: (b, 0)),
                      pl.BlockSpec((seq * H_B, DV_B), lambda b, h, i, lam: (b, 0)),
                      pl.BlockSpec((1, DV_B), lambda b, h, i, lam: (0, 0))],
            out_specs=pl.BlockSpec((tq, DV_B), lambda b, h, i, lam: (b * nq + i, h)),
            scratch_shapes=[pltpu.VMEM((seq, DV_B), BF16), pltpu.VMEM((seq, DV_B), BF16)]),
        compiler_params=_cparams(("parallel", "parallel", "arbitrary")),
        name="diff_attn_prompt",
    )(lam, qb, kb, vb, diff_norm_g.reshape(1, DV_B))


def _diff_step_kernel(lam_ref, q_ref, kp_ref, vp_ref, kn_ref, vn_ref, ng_ref, o_ref, *, past, ln):
    qchunk = (past + lax.broadcasted_iota(I32, (ln, 1), 0)) // CHUNK
    vis_p = (lax.broadcasted_iota(I32, (ln, past), 1) // CHUNK) <= qchunk
    vis_n = ((past + lax.broadcasted_iota(I32, (ln, ln), 1)) // CHUNK) <= qchunk
    for h in range(H_B):
        q = q_ref[:, h * DV_B:(h + 1) * DV_B]
        kp = kp_ref[pl.ds(h, past, stride=H_B), :].astype(BF16)
        vp = vp_ref[pl.ds(h, past, stride=H_B), :].astype(BF16)
        kn = kn_ref[pl.ds(h, ln, stride=H_B), :].astype(BF16)
        vn = vn_ref[pl.ds(h, ln, stride=H_B), :].astype(BF16)
        accs = []
        for c in range(2):
            sl = slice(c * DK_B, (c + 1) * DK_B)
            sp = jnp.where(vis_p, _dot_nt(q[:, sl], kp[:, sl]), NEG_BIG)
            sn = jnp.where(vis_n, _dot_nt(q[:, sl], kn[:, sl]), NEG_BIG)
            m = jnp.maximum(jnp.max(sp, axis=-1, keepdims=True), jnp.max(sn, axis=-1, keepdims=True))
            pp = jnp.exp(sp - m)
            pn = jnp.exp(sn - m)
            l = jnp.sum(pp, axis=-1, keepdims=True) + jnp.sum(pn, axis=-1, keepdims=True)
            accs += [_dot(pp.astype(BF16), vp) + _dot(pn.astype(BF16), vn), l]
        y = _diff_finish(accs[0], accs[1], accs[2], accs[3], lam_ref[0], ng_ref[...])
        o_ref[:, h * DV_B:(h + 1) * DV_B] = y.astype(o_ref.dtype)


def _diff_step(qb, kb, vb, past_k, past_v, lam, diff_norm_g, n_batch, ln):
    past = past_k.shape[0] // (n_batch * H_B)
    blk = lambda b, lam: (b, 0)
    return pl.pallas_call(
        functools.partial(_diff_step_kernel, past=past, ln=ln),
        out_shape=jax.ShapeDtypeStruct((n_batch * ln, B_V), BF16),
        grid_spec=pltpu.PrefetchScalarGridSpec(
            num_scalar_prefetch=1,
            grid=(n_batch,),
            in_specs=[pl.BlockSpec((ln, B_QK), blk),
                      pl.BlockSpec((past * H_B, DV_B), blk),
                      pl.BlockSpec((past * H_B, DV_B), blk),
                      pl.BlockSpec((ln * H_B, DV_B), blk),
                      pl.BlockSpec((ln * H_B, DV_B), blk),
                      pl.BlockSpec((1, DV_B), lambda b, lam: (0, 0))],
            out_specs=pl.BlockSpec((ln, B_V), blk)),
        compiler_params=_cparams(("parallel",)),
        name="diff_attn_step",
    )(lam, qb, past_k, past_v, kb, vb, diff_norm_g.reshape(1, DV_B))


X_LANE_BLKS = X_HD // 128


def _xattn_kernel(q_ref, mk_ref, mv_ref, o_ref, *, split):
    stride = X_LANE_BLKS * X_HEADS
    for h in range(X_HEADS):
        if split:
            pieces = [(slice(h * X_HD + j * 128, h * X_HD + (j + 1) * 128), pl.ds(j * X_HEADS + h, N_MEM, stride=stride))
                      for j in range(X_LANE_BLKS)]
            s = sum(_dot_nt(q_ref[:, cs], mk_ref[rs, :].astype(BF16)) for cs, rs in pieces)
        else:
            sl = slice(h * X_HD, (h + 1) * X_HD)
            s = _dot_nt(q_ref[:, sl], mk_ref[:, sl].astype(BF16))
        p = jnp.exp(s - jnp.max(s, axis=-1, keepdims=True))
        inv_l = 1.0 / jnp.sum(p, axis=-1, keepdims=True)
        pb = p.astype(BF16)
        if split:
            for cs, rs in pieces:
                o_ref[:, cs] = (_dot(pb, mv_ref[rs, :].astype(BF16)) * inv_l).astype(o_ref.dtype)
        else:
            o_ref[:, sl] = (_dot(pb, mv_ref[:, sl].astype(BF16)) * inv_l).astype(o_ref.dtype)


def _xattn(q, mk, mv, layer, n_batch, tq, split):
    T = q.shape[0]
    per = T // n_batch
    tq = min(tq, per)
    nt = per // tq
    mem_blk = (None, N_MEM * X_LANE_BLKS * X_HEADS, 128) if split else (None, N_MEM, D_MODEL)
    return pl.pallas_call(
        functools.partial(_xattn_kernel, split=split),
        out_shape=jax.ShapeDtypeStruct((T, D_MODEL), BF16),
        grid=(n_batch, nt),
        in_specs=[pl.BlockSpec((tq, D_MODEL), lambda b, t: (b * nt + t, 0)),
                  pl.BlockSpec(mem_blk, lambda b, t: (layer, b, 0)),
                  pl.BlockSpec(mem_blk, lambda b, t: (layer, b, 0))],
        out_specs=pl.BlockSpec((tq, D_MODEL), lambda b, t: (b * nt + t, 0)),
        compiler_params=_cparams(("parallel", "arbitrary")),
        name="mem_xattn",
    )(q, mk, mv)


def _swiglu_up_kernel(x_ref, wg_ref, wu_ref, o_ref):
    xb = x_ref[...]
    g = _dot(xb, wg_ref[...])
    u = _dot(xb, wu_ref[...])
    o_ref[...] = (_silu(g) * u).astype(o_ref.dtype)


def _swiglu_up(xb, w_gu_bf, tm, tn):
    T = xb.shape[0]
    tm = min(tm, T)
    nj = D_FF // tn
    assert D_FF % tn == 0
    return pl.pallas_call(
        _swiglu_up_kernel,
        out_shape=jax.ShapeDtypeStruct((T, D_FF), BF16),
        grid=(T // tm, nj),
        in_specs=[pl.BlockSpec((tm, D_MODEL), lambda i, j: (i, 0)),
                  pl.BlockSpec((D_MODEL, tn), lambda i, j: (0, j)),
                  pl.BlockSpec((D_MODEL, tn), lambda i, j: (0, nj + j))],
        out_specs=pl.BlockSpec((tm, tn), lambda i, j: (i, j)),
        compiler_params=_cparams(("parallel", "arbitrary")),
        name="swiglu_up",
    )(xb, w_gu_bf, w_gu_bf)


def _conv_kernel(x_ref, prev_ref, st_ref, w_ref, o_ref, buf, *, tm):
    t = pl.program_id(1)
    cb = pl.program_id(2)
    buf[0:8, :] = jnp.where(t == 0, st_ref[0], prev_ref[...])
    buf[8:8 + tm, :] = x_ref[...]
    scale = jnp.where(cb == 0, DK_C ** -0.5, 1.0)
    rc = min(CONV_ROWS, tm)
    for h in range(H_C):
        cs = slice(h * DK_C, (h + 1) * DK_C)
        w = [w_ref[i:i + 1, cs] for i in range(CONV_W)]
        for r0 in range(0, tm, rc):
            y = buf[8 + r0:8 + r0 + rc, cs] * w[CONV_W - 1]
            for i in range(CONV_W - 1):
                y = y + buf[5 + i + r0:5 + i + r0 + rc, cs] * w[i]
            y = _silu(y)
            ss = jnp.sum(y * y, axis=-1, keepdims=True)
            f = jnp.where(cb == 2, 1.0, lax.rsqrt(ss + 1e-6) * scale)
            o_ref[r0:r0 + rc, cs] = (y * f).astype(o_ref.dtype)


def _gdn_conv(proj1, conv_state8, conv_w, n_batch, tm):
    T = proj1.shape[0]
    per = T // n_batch
    tm = min(tm, per)
    nt = per // tm
    assert per % tm == 0 and tm % 8 == 0
    return pl.pallas_call(
        functools.partial(_conv_kernel, tm=tm),
        out_shape=jax.ShapeDtypeStruct((T, 3 * C_W), BF16),
        grid=(n_batch, nt, 3),
        in_specs=[pl.BlockSpec((tm, C_W), lambda b, t, c: (b * nt + t, c)),
                  pl.BlockSpec((8, C_W), lambda b, t, c: (jnp.maximum((b * nt + t) * (tm // 8) - 1, 0), c)),
                  pl.BlockSpec((1, 8, C_W), lambda b, t, c: (b, 0, c)),
                  pl.BlockSpec((CONV_W, C_W), lambda b, t, c: (0, c))],
        out_specs=pl.BlockSpec((tm, C_W), lambda b, t, c: (b * nt + t, c)),
        scratch_shapes=[pltpu.VMEM((8 + tm, C_W), F32)],
        compiler_params=_cparams(("parallel", "arbitrary", "arbitrary")),
        name="gdn_conv",
    )(proj1, proj1, conv_state8, conv_w)


def _gdn_kernel(q_ref, k_ref, v_ref, z_ref, ab_ref, alog_ref, dtb_ref, ng_ref, s0_ref,
                o_ref, sout_ref, s_scr, *, lc, l_real):
    c = pl.program_id(1)

    @pl.when(c == 0)
    def _():
        s_scr[...] = s0_ref[0]

    ri = lax.broadcasted_iota(I32, (lc, lc), 0)
    ci = lax.broadcasted_iota(I32, (lc, lc), 1)
    incl = ci <= ri
    strict = ci < ri
    eye = (ci == ri).astype(F32)
    ab = ab_ref[...]
    sp = jnp.maximum(ab + dtb_ref[...], 0.0) + jnp.log1p(jnp.exp(-jnp.abs(ab + dtb_ref[...])))
    glog = -jnp.exp(alog_ref[...]) * sp
    if l_real < lc:
        live = lax.broadcasted_iota(I32, (lc, 1), 0) < l_real
        glog = jnp.where(live, glog, 0.0)
    gcum = jnp.dot(incl.astype(F32), glog, preferred_element_type=F32, precision=lax.Precision.HIGHEST)
    gcum_t = gcum.T
    beta_all = 1.0 / (1.0 + jnp.exp(-ab))

    levels = []
    s = 1
    while s < lc:
        levels.append(((ri // (2 * s)) == (ci // (2 * s))) & (((ri // s) % 2) == 1) & (((ci // s) % 2) == 0))
        s *= 2

    for h0 in range(0, H_C, GDN_HEAD_GROUP):
        heads = range(h0, h0 + GDN_HEAD_GROUP)
        sls = [slice(h * DK_C, (h + 1) * DK_C) for h in heads]
        ks, kfs, vs, gcols, decays, egs, kbs, a_lows = [], [], [], [], [], [], [], []
        for h, sl in zip(heads, sls):
            k = k_ref[:, sl]
            v = v_ref[:, sl].astype(F32)
            if l_real < lc:
                k = jnp.where(live, k, jnp.zeros_like(k))
                v = jnp.where(live, v, 0.0)
            kf = k.astype(F32)
            gcol = gcum[:, h:h + 1]
            grow = gcum_t[h:h + 1, :]
            beta = beta_all[:, H_C + h:H_C + h + 1]
            decay = jnp.exp(jnp.where(incl, gcol - grow, NEG_BIG))
            kb = kf * beta
            ks.append(k)
            kfs.append(kf)
            gcols.append(gcol)
            decays.append(decay)
            egs.append(jnp.exp(gcol))
            kbs.append(kb)
            vs.append(v * beta)
            a_lows.append(jnp.where(strict, _dot_nt(kb.astype(BF16), k) * decay, 0.0))
        tinvs = [eye - jnp.where(levels[0], a, 0.0) for a in a_lows]
        for lvl, off in enumerate(levels[1:], start=1):
            s = 2 ** lvl
            tbs = [t.astype(BF16) for t in tinvs]
            if s < 8:
                a_offs = [jnp.where(off, a, 0.0).astype(BF16) for a in a_lows]
                ws = [_dot(a, t).astype(BF16) for a, t in zip(a_offs, tbs)]
                tinvs = [t - _dot(tb, w) for t, tb, w in zip(tinvs, tbs, ws)]
            else:
                odd = [slice((2 * j + 1) * s, (2 * j + 2) * s) for j in range(lc // (2 * s))]
                even = [slice(2 * j * s, (2 * j + 1) * s) for j in range(lc // (2 * s))]
                take = lambda x: jnp.concatenate([x[sl, :] for sl in odd], axis=0)
                rh = lax.broadcasted_iota(I32, (lc // 2, lc), 0)
                ch = lax.broadcasted_iota(I32, (lc // 2, lc), 1)
                off_odd = (ch // s) == 2 * (rh // s)
                zero = jnp.zeros((s, lc), F32)
                new = []
                for a, t, tb in zip(a_lows, tinvs, tbs):
                    w_odd = _dot(jnp.where(off_odd, take(a), 0.0).astype(BF16), tb)
                    w_full = jnp.concatenate(
                        [piece for j in range(len(odd)) for piece in (zero, w_odd[j * s:(j + 1) * s, :])], axis=0)
                    t_odd = take(t)
                    t_odd = t_odd - _dot(t_odd.astype(BF16), w_full.astype(BF16))
                    new.append(jnp.concatenate(
                        [piece for j, ev in enumerate(even) for piece in (t[ev, :], t_odd[j * s:(j + 1) * s, :])],
                        axis=0))
                tinvs = new
        rhss = [jnp.concatenate([vb, kb * eg], axis=-1).astype(BF16) for vb, kb, eg in zip(vs, kbs, egs)]
        sols = [_dot(t.astype(BF16), r) for t, r in zip(tinvs, rhss)]
        sts = [s_scr[h] for h in heads]
        stbs = [st.astype(BF16) for st in sts]
        ubs = [(sol[:, :DV_C] - _dot(sol[:, DV_C:].astype(BF16), stb)).astype(BF16) for sol, stb in zip(sols, stbs)]
        qs = [q_ref[:, sl] for sl in sls]
        qks = [(_dot_nt(q, k) * decay).astype(BF16) for q, k, decay in zip(qs, ks, decays)]
        os_ = [_dot((q.astype(F32) * eg).astype(BF16), stb) + _dot(qk, ub)
               for q, eg, stb, qk, ub in zip(qs, egs, stbs, qks, ubs)]
        for h, st, kf, gcol, ub in zip(heads, sts, kfs, gcols, ubs):
            g_last = gcol[lc - 1:lc, :]
            kdec = (kf * jnp.exp(g_last - gcol)).astype(BF16)
            s_scr[h] = st * jnp.exp(g_last) + _dot_tn(kdec, ub)
        for sl, o in zip(sls, os_):
            ms = jnp.mean(o * o, axis=-1, keepdims=True)
            y = o * lax.rsqrt(ms + LN_EPS) * ng_ref[...] * _silu(z_ref[:, sl])
            o_ref[:, sl] = y.astype(o_ref.dtype)

    @pl.when(c == pl.num_programs(1) - 1)
    def _():
        sout_ref[0] = s_scr[...]


def _gdn(qkvn, proj1, ab, a_log, dt_bias, gdn_norm_g, state0, n_batch, lc, l_real):
    T = qkvn.shape[0]
    nc = T // (n_batch * lc)
    assert nc * n_batch * lc == T and (nc == 1 or l_real == lc)
    pad = lambda v: jnp.pad(v.astype(F32).reshape(1, H_C), ((0, 0), (0, 128 - H_C)))
    return pl.pallas_call(
        functools.partial(_gdn_kernel, lc=lc, l_real=l_real),
        out_shape=[jax.ShapeDtypeStruct((T, C_W), BF16),
                   jax.ShapeDtypeStruct((n_batch, H_C, DK_C, DV_C), F32)],
        grid=(n_batch, nc),
        in_specs=[pl.BlockSpec((lc, C_W), lambda b, c: (b * nc + c, 0)),
                  pl.BlockSpec((lc, C_W), lambda b, c: (b * nc + c, 1)),
                  pl.BlockSpec((lc, C_W), lambda b, c: (b * nc + c, 2)),
                  pl.BlockSpec((lc, C_W), lambda b, c: (b * nc + c, 3)),
                  pl.BlockSpec((lc, 128), lambda b, c: (b * nc + c, 0)),
                  pl.BlockSpec((1, 128), lambda b, c: (0, 0)),
                  pl.BlockSpec((1, 128), lambda b, c: (0, 0)),
                  pl.BlockSpec((1, DV_C), lambda b, c: (0, 0)),
                  pl.BlockSpec((1, H_C, DK_C, DV_C), lambda b, c: (b, 0, 0, 0))],
        out_specs=[pl.BlockSpec((lc, C_W), lambda b, c: (b * nc + c, 0)),
                   pl.BlockSpec((1, H_C, DK_C, DV_C), lambda b, c: (b, 0, 0, 0))],
        scratch_shapes=[pltpu.VMEM((H_C, DK_C, DV_C), F32)],
        compiler_params=_cparams(("parallel", "arbitrary")),
        name="gated_delta",
    )(qkvn, qkvn, qkvn, proj1, ab, pad(a_log), pad(dt_bias), gdn_norm_g.reshape(1, DV_C), state0)


def _router_kernel(x_ref, wt_ref, rt_ref, rtt_ref, cnt_ref):
    tt = x_ref.shape[0]
    logits = lax.dot_general(wt_ref[...], x_ref[...], (((1,), (1,)), ((), ())),
                             preferred_element_type=F32, precision=lax.Precision.HIGHEST)
    eid = lax.broadcasted_iota(I32, (N_EXPERTS, tt), 0).astype(F32)
    m1 = jnp.max(logits, axis=0, keepdims=True)
    e1 = jnp.min(jnp.where(logits == m1, eid, float(N_EXPERTS)), axis=0, keepdims=True)
    rest = jnp.where(eid == e1, -jnp.inf, logits)
    m2 = jnp.max(rest, axis=0, keepdims=True)
    e2 = jnp.min(jnp.where(rest == m2, eid, float(N_EXPERTS)), axis=0, keepdims=True)
    ev = jnp.exp(m2 - m1)
    g1 = 1.0 / (1.0 + ev)
    g2 = ev / (1.0 + ev)
    hit1 = eid == e1
    hit2 = eid == e2
    member = jnp.where(hit1 | hit2, 1.0, 0.0)
    before = (lax.broadcasted_iota(I32, (tt, tt), 0) < lax.broadcasted_iota(I32, (tt, tt), 1))
    rank = _dot(member.astype(BF16), jnp.where(before, 1.0, 0.0).astype(BF16))
    r1 = jnp.sum(jnp.where(hit1, rank, 0.0), axis=0, keepdims=True)
    r2 = jnp.sum(jnp.where(hit2, rank, 0.0), axis=0, keepdims=True)
    rows = jnp.concatenate([e1, e2, r1, r2, g1, g2, jnp.zeros((2, tt), F32)], axis=0)
    rt_ref[0] = rows
    rtt_ref[0] = jnp.concatenate([rows, jnp.zeros((128 - 8, tt), F32)], axis=0).T
    cnt = jnp.sum(member, axis=1, keepdims=True).astype(I32)
    cnt_ref[0] = jnp.broadcast_to(cnt, (N_EXPERTS, 128))


def _router(x, w_router):
    T = x.shape[0]
    tt = MOE_TOK_TILE
    nt = T // tt
    assert T % tt == 0
    return pl.pallas_call(
        _router_kernel,
        out_shape=[jax.ShapeDtypeStruct((nt, 8, tt), F32),
                   jax.ShapeDtypeStruct((nt, tt, 128), F32),
                   jax.ShapeDtypeStruct((nt, N_EXPERTS, 128), I32)],
        grid=(nt,),
        in_specs=[pl.BlockSpec((tt, D_MODEL), lambda i: (i, 0)),
                  pl.BlockSpec((N_EXPERTS, D_MODEL), lambda i: (0, 0))],
        out_specs=[pl.BlockSpec((1, 8, tt), lambda i: (i, 0, 0)),
                   pl.BlockSpec((1, tt, 128), lambda i: (i, 0, 0)),
                   pl.BlockSpec((1, N_EXPERTS, 128), lambda i: (i, 0, 0))],
        compiler_params=_cparams(("parallel",)),
        name="moe_router",
    )(x, w_router.T)


def _moe_tables(cnt, n_tiles, group):
    E = N_EXPERTS
    nb_max = (2 * n_tiles * MOE_TOK_TILE) // MOE_SUB + n_tiles * E
    nb_tot = -(-(nb_max + E * (group - 1)) // group) * group
    nblk = (cnt + MOE_SUB - 1) // MOE_SUB
    nbe = jnp.sum(nblk, axis=0)
    nbe_pad = (nbe + group - 1) // group * group
    ends_e = jnp.cumsum(nbe_pad)
    base = ends_e - nbe_pad
    dest0 = base[None, :] + jnp.cumsum(nblk, axis=0) - nblk
    ends_t = jnp.cumsum(nblk, axis=1)
    kk = jnp.arange(MOE_SLOTS, dtype=I32)
    slot_e = jnp.minimum(jnp.sum(ends_t[:, None, :] <= kk[None, :, None], axis=-1), E - 1).astype(I32)
    slot_valid = kk[None, :] < ends_t[:, -1:]
    start_t = jnp.take_along_axis(ends_t - nblk, slot_e, axis=1)
    slot_j = jnp.where(slot_valid, kk[None, :] - start_t, 0)
    slot_dest = jnp.where(slot_valid, jnp.take_along_axis(dest0, slot_e, axis=1) + slot_j, 0)
    npad = nbe_pad - nbe
    pm = jnp.arange(E * (group - 1), dtype=I32)
    pe, pt = pm // max(group - 1, 1), pm % max(group - 1, 1)
    pad_valid = pt < npad[pe]
    pad_dest = jnp.where(pad_valid, base[pe] + nbe[pe] + pt, 0)
    n_rt = nb_tot // group
    r0 = jnp.arange(n_rt, dtype=I32) * group
    rt_valid = r0 < ends_e[-1]
    rt_e = jnp.minimum(jnp.sum(ends_e[None, :] <= r0[:, None], axis=-1), E - 1)
    last_e = jnp.minimum(jnp.sum(ends_e <= ends_e[-1] - 1), E - 1)
    rt_e = jnp.where(rt_valid, rt_e, last_e)
    i32 = lambda a: a.astype(I32).reshape(-1)
    return dict(nb_tot=nb_tot, n_rt=n_rt, slot_e=i32(slot_e), slot_j=i32(slot_j), slot_dest=i32(slot_dest),
                slot_valid=i32(slot_valid), pad_dest=i32(pad_dest), pad_valid=i32(pad_valid),
                rt_e=i32(rt_e), rt_valid=i32(rt_valid), n_valid_rt=i32(ends_e[-1] // group))


def _moe_gather_kernel(se_ref, sj_ref, sd_ref, sv_ref, pd_ref, pv_ref, nv_ref, x_ref, rt_ref, xg_hbm,
                       buf, zbuf, sem, zsem, *, n_pad, n_rt, rows):
    i = pl.program_id(0)
    rt = rt_ref[0]
    height = MOE_SLOTS * MOE_SUB
    row = lax.broadcasted_iota(I32, (height, 1), 0)
    row_e = jnp.full((height, 1), -1.0, F32)
    row_r = (row % MOE_SUB).astype(F32)
    for k in range(MOE_SLOTS):
        idx = i * MOE_SLOTS + k
        here = (row // MOE_SUB) == k
        row_e = jnp.where(here, jnp.where(sv_ref[idx] == 1, se_ref[idx], -1).astype(F32), row_e)
        row_r = row_r + jnp.where(here, (sj_ref[idx] * MOE_SUB).astype(F32), 0.0)
    sel = ((rt[0:1, :] == row_e) & (rt[2:3, :] == row_r)) | ((rt[1:2, :] == row_e) & (rt[3:4, :] == row_r))
    buf[...] = _dot(jnp.where(sel, 1.0, 0.0).astype(BF16), x_ref[...]).astype(BF16)

    def slot_copy(k):
        dst = pl.multiple_of(sd_ref[i * MOE_SLOTS + k] * MOE_SUB, MOE_SUB)
        return pltpu.make_async_copy(buf.at[pl.ds(k * MOE_SUB, MOE_SUB)], xg_hbm.at[pl.ds(dst, MOE_SUB)], sem.at[k])

    for k in range(MOE_SLOTS):
        @pl.when(sv_ref[i * MOE_SLOTS + k] == 1)
        def _(k=k):
            slot_copy(k).start()

    for k in range(MOE_SLOTS):
        @pl.when(sv_ref[i * MOE_SLOTS + k] == 1)
        def _(k=k):
            slot_copy(k).wait()

    def pad_copy(m):
        dst = pl.multiple_of(pd_ref[m] * MOE_SUB, MOE_SUB)
        return pltpu.make_async_copy(zbuf.at[pl.ds(0, MOE_SUB)], xg_hbm.at[pl.ds(dst, MOE_SUB)], zsem.at[0])

    def tail_copy(r):
        dst = pl.multiple_of(r * rows, rows)
        return pltpu.make_async_copy(zbuf, xg_hbm.at[pl.ds(dst, rows)], zsem.at[1])

    @pl.when(i == pl.num_programs(0) - 1)
    def _():
        zbuf[...] = jnp.zeros_like(zbuf)

        def pad_start(m, carry):
            @pl.when(pv_ref[m] == 1)
            def _():
                pad_copy(m).start()
            return carry

        def pad_wait(m, carry):
            @pl.when(pv_ref[m] == 1)
            def _():
                pad_copy(m).wait()
            return carry

        def tail_start(r, carry):
            tail_copy(r).start()
            return carry

        def tail_wait(r, carry):
            tail_copy(r).wait()
            return carry

        lax.fori_loop(0, n_pad, pad_start, 0)
        lax.fori_loop(nv_ref[0], n_rt, tail_start, 0)
        lax.fori_loop(0, n_pad, pad_wait, 0)
        lax.fori_loop(nv_ref[0], n_rt, tail_wait, 0)


def _moe_gather(xb, rt, tb, rows):
    nb = tb["nb_tot"]
    tt = MOE_TOK_TILE
    nt = xb.shape[0] // tt
    n_pad = tb["pad_dest"].shape[0]
    return pl.pallas_call(
        functools.partial(_moe_gather_kernel, n_pad=n_pad, n_rt=tb["n_rt"], rows=rows),
        out_shape=jax.ShapeDtypeStruct((nb * MOE_SUB, D_MODEL), BF16),
        grid_spec=pltpu.PrefetchScalarGridSpec(
            num_scalar_prefetch=7,
            grid=(nt,),
            in_specs=[pl.BlockSpec((tt, D_MODEL), lambda i, *_: (i, 0)),
                      pl.BlockSpec((1, 8, tt), lambda i, *_: (i, 0, 0))],
            out_specs=pl.BlockSpec(memory_space=pl.ANY),
            scratch_shapes=[pltpu.VMEM((MOE_SLOTS * MOE_SUB, D_MODEL), BF16),
                            pltpu.VMEM((rows, D_MODEL), BF16),
                            pltpu.SemaphoreType.DMA((MOE_SLOTS,)),
                            pltpu.SemaphoreType.DMA((2,))]),
        compiler_params=_cparams(("arbitrary",)),
        name="moe_gather",
    )(tb["slot_e"], tb["slot_j"], tb["slot_dest"], tb["slot_valid"], tb["pad_dest"], tb["pad_valid"],
      tb["n_valid_rt"], xb, rt)


def _moe_ffn_kernel(e_ref, valid_ref, x_ref, wg_ref, wu_ref, wd_ref, y_ref, acc, *, rows):
    r = pl.program_id(0)
    f = pl.program_id(1)
    nf = pl.num_programs(1)
    ok = valid_ref[r] == 1

    @pl.when(jnp.logical_and(ok, f == 0))
    def _():
        acc[...] = jnp.zeros_like(acc)

    @pl.when(ok)
    def _():
        xb = x_ref[...]
        g = _dot(xb, wg_ref[0].astype(BF16))
        u = _dot(xb, wu_ref[0].astype(BF16))
        acc[...] += _dot((_silu(g) * u).astype(BF16), wd_ref[0].astype(BF16))

    @pl.when(jnp.logical_and(ok, f == nf - 1))
    def _():
        y_ref[...] = acc[...].astype(y_ref.dtype)

    @pl.when(jnp.logical_and(jnp.logical_not(ok), f == nf - 1))
    def _():
        y_ref[...] = jnp.zeros_like(y_ref)


def _moe_ffn(xg, w_gu_bf, w_d_bf, tb, rows):
    n_rt = tb["n_rt"]
    nf = D_FF_E // MOE_FF_BLK
    fb = MOE_FF_BLK

    def fsel(r, f, v):
        return jnp.where(v[r] == 1, f, nf - 1)

    return pl.pallas_call(
        functools.partial(_moe_ffn_kernel, rows=rows),
        out_shape=jax.ShapeDtypeStruct(xg.shape, BF16),
        grid_spec=pltpu.PrefetchScalarGridSpec(
            num_scalar_prefetch=2,
            grid=(n_rt, nf),
            in_specs=[pl.BlockSpec((rows, D_MODEL), lambda r, f, e, v: (r, 0)),
                      pl.BlockSpec((1, D_MODEL, fb), lambda r, f, e, v: (e[r], 0, fsel(r, f, v))),
                      pl.BlockSpec((1, D_MODEL, fb), lambda r, f, e, v: (e[r], 0, nf + fsel(r, f, v))),
                      pl.BlockSpec((1, fb, D_MODEL), lambda r, f, e, v: (e[r], fsel(r, f, v), 0))],
            out_specs=pl.BlockSpec((rows, D_MODEL), lambda r, f, e, v: (r, 0)),
            scratch_shapes=[pltpu.VMEM((rows, D_MODEL), F32)]),
        compiler_params=_cparams(("arbitrary", "arbitrary")),
        name="moe_grouped_ffn",
    )(tb["rt_e"], tb["rt_valid"], xg, w_gu_bf, w_gu_bf, w_d_bf)


def _moe_combine_kernel(se_ref, sj_ref, sd_ref, sv_ref, *refs):
    y_refs = refs[:MOE_SLOTS]
    rtt_ref, x_ref, g_ref, b_ref, o_ref, ybuf = refs[MOE_SLOTS:]
    i = pl.program_id(0)
    width = MOE_SLOTS * MOE_SUB
    lane = lax.broadcasted_iota(I32, (1, width), 1)
    lane_e = jnp.full((1, width), -1.0, F32)
    lane_r = (lane % MOE_SUB).astype(F32)
    for k in range(MOE_SLOTS):
        idx = i * MOE_SLOTS + k
        valid = sv_ref[idx] == 1
        yk = y_refs[k][...]
        ybuf[k * MOE_SUB:(k + 1) * MOE_SUB, :] = jnp.where(valid, yk, jnp.zeros_like(yk))
        here = (lane // MOE_SUB) == k
        lane_e = jnp.where(here, jnp.where(valid, se_ref[idx], -1).astype(F32), lane_e)
        lane_r = lane_r + jnp.where(here, (sj_ref[idx] * MOE_SUB).astype(F32), 0.0)
    rtt = rtt_ref[0]
    a1 = jnp.where((rtt[:, 0:1] == lane_e) & (rtt[:, 2:3] == lane_r), 1.0, 0.0).astype(BF16)
    a2 = jnp.where((rtt[:, 1:2] == lane_e) & (rtt[:, 3:4] == lane_r), 1.0, 0.0).astype(BF16)
    yb = ybuf[...]
    ff = rtt[:, 4:5] * _dot(a1, yb) + rtt[:, 5:6] * _dot(a2, yb)
    o_ref[...] = _layer_norm_rows(DN_ALPHA * x_ref[...] + ff, g_ref[...], b_ref[...])


def _moe_combine(yg, rtt, x, g, b, tb):
    tt = MOE_TOK_TILE
    T = x.shape[0]

    def y_map(i, se, sj, sd, sv, *, k):
        return (sd[i * MOE_SLOTS + k], 0)

    y_specs = [pl.BlockSpec((MOE_SUB, D_MODEL), functools.partial(y_map, k=k)) for k in range(MOE_SLOTS)]
    return pl.pallas_call(
        _moe_combine_kernel,
        out_shape=jax.ShapeDtypeStruct((T, D_MODEL), F32),
        grid_spec=pltpu.PrefetchScalarGridSpec(
            num_scalar_prefetch=4,
            grid=(T // tt,),
            in_specs=y_specs + [pl.BlockSpec((1, tt, 128), lambda i, *_: (i, 0, 0)),
                                pl.BlockSpec((tt, D_MODEL), lambda i, *_: (i, 0)),
                                pl.BlockSpec((1, D_MODEL), lambda i, *_: (0, 0)),
                                pl.BlockSpec((1, D_MODEL), lambda i, *_: (0, 0))],
            out_specs=pl.BlockSpec((tt, D_MODEL), lambda i, *_: (i, 0)),
            scratch_shapes=[pltpu.VMEM((MOE_SLOTS * MOE_SUB, D_MODEL), BF16)]),
        compiler_params=_cparams(("arbitrary",)),
        name="moe_combine",
    )(tb["slot_e"], tb["slot_j"], tb["slot_dest"], tb["slot_valid"], *([yg] * MOE_SLOTS), rtt, x,
      g.reshape(1, D_MODEL), b.reshape(1, D_MODEL))


def _moe(x, xb, w_router, w_gu_bf, w_d_bf, g, b, rows):
    T = x.shape[0]
    nt = T // MOE_TOK_TILE
    rt, rtt, cnt = _router(x, w_router)
    tb = _moe_tables(cnt[:, :, 0], nt, rows // MOE_SUB)
    xg = _moe_gather(xb, rt, tb, rows)
    yg = _moe_ffn(xg, w_gu_bf, w_d_bf, tb, rows)
    return _moe_combine(yg, rtt, x, g, b, tb)


def _trunk(x, pos_rows, n_batch, mem_k, mem_v, ret_state, past_k, past_v, conv_state, gdn_state, p, cfg):
    T = x.shape[0]
    per = T // n_batch
    tm = cfg["tm"]
    tm_mm = cfg["tm_mm"]
    ln_g, ln_b = p["ln_g"], p["ln_b"]

    qka, va, ga, qb, kb, vb = _proj0(x.astype(BF16), p["w_in0"], _rope_tables(pos_rows), tm_mm)
    oa, ret_new = _retention(qka, va, ga, ret_state, p["ret_norm_g"], n_batch, cfg["ret_lc"])
    lp = p["diff_lambda"].astype(F32)
    lam = (jnp.exp(jnp.sum(lp[0] * lp[1])) - jnp.exp(jnp.sum(lp[2] * lp[3])) + DIFF_LAMBDA_INIT).reshape(1)
    if past_k is None:
        ob = _diff_prompt(qb, kb, vb, lam, p["diff_norm_g"], n_batch, per, cfg["diff_tq"])
    else:
        ob = _diff_step(qb, kb, vb, past_k, past_v, lam, p["diff_norm_g"], n_batch, per)
    x, q = _matmul_deepnorm([oa, ob], [p["w_out0"][:A_V], p["w_out0"][A_V:]], x, ln_g[0, 0], ln_b[0, 0], tm,
                            w_next=p["w_xq"][0], next_scale=X_HD ** -0.5, name="out0_deepnorm_xq")
    xa = _xattn(q, mem_k, mem_v, 0, n_batch, cfg["x_tq"], cfg["mem_split"])
    x, xb = _matmul_deepnorm([xa], [p["w_xo"][0]], x, ln_g[0, 1], ln_b[0, 1], tm, name="xo0_deepnorm")
    hmid = _swiglu_up(xb, p["w_ffn_gu"], tm, cfg["ffn_tn"])
    x, xb = _matmul_deepnorm([hmid], [p["w_ffn_d"]], x, ln_g[0, 2], ln_b[0, 2], tm, name="ffn_down_deepnorm")

    proj1 = _matmul(xb, p["w_in1_main"][None], F32, tm_mm, 1024, name="proj1")[0]
    ab = _matmul(xb, p["w_in1_ab"][None], F32, tm_mm, 128, name="proj1_gates")[0]
    qkv3 = proj1.reshape(n_batch, per, 4 * C_W)[:, :, :3 * C_W]
    conv_new = qkv3[:, per - (CONV_W - 1):, :]
    state8 = jnp.pad(conv_state.astype(F32), ((0, 0), (8 - (CONV_W - 1), 0), (0, 0)))
    qkvn = _gdn_conv(proj1, state8, p["conv_w"], n_batch, cfg["conv_tm"])
    lc = cfg["gdn_lc"]
    if per < lc:
        padrows = lambda a: jnp.pad(a.reshape(n_batch, per, a.shape[-1]),
                                    ((0, 0), (0, lc - per), (0, 0))).reshape(n_batch * lc, a.shape[-1])
        og, gdn_new = _gdn(padrows(qkvn), padrows(proj1), padrows(ab), p["a_log"], p["dt_bias"],
                           p["gdn_norm_g"], gdn_state, n_batch, lc, per)
        og = og.reshape(n_batch, lc, C_W)[:, :per].reshape(T, C_W)
    else:
        og, gdn_new = _gdn(qkvn, proj1, ab, p["a_log"], p["dt_bias"], p["gdn_norm_g"], gdn_state,
                           n_batch, lc, lc)
    x, q = _matmul_deepnorm([og], [p["w_out1"]], x, ln_g[1, 0], ln_b[1, 0], tm,
                            w_next=p["w_xq"][1], next_scale=X_HD ** -0.5, name="out1_deepnorm_xq")
    xa = _xattn(q, mem_k, mem_v, 1, n_batch, cfg["x_tq"], cfg["mem_split"])
    x, xb = _matmul_deepnorm([xa], [p["w_xo"][1]], x, ln_g[1, 1], ln_b[1, 1], tm, name="xo1_deepnorm")
    x = _moe(x, xb, p["w_router"], p["w_moe_gu"], p["w_moe_d"], ln_g[1, 2], ln_b[1, 2], cfg["moe_rows"])
    return x, kb, vb, ret_new, conv_new, gdn_new


PROMPT_CFG = dict(tm=512, tm_mm=1024, ret_lc=256, diff_tq=512, x_tq=512, ffn_tn=1408, conv_tm=512,
                  gdn_lc=256, moe_rows=1024, mem_split=False)
STEP_CFG = dict(tm=512, tm_mm=512, ret_lc=16, diff_tq=16, x_tq=16, ffn_tn=1408, conv_tm=16,
                gdn_lc=128, moe_rows=256, mem_split=True)


def kernel(x_prompt, x_sample, cache_diff_k, cache_diff_v, state_ret, state_gdn_conv, state_gdn, cache_mem_k, cache_mem_v, mem_prompt, w_in0, ret_norm_g, diff_lambda, diff_norm_g, w_out0, w_in1, conv_w, a_log, dt_bias, gdn_norm_g, w_out1, w_xq, w_xkv, w_xo, w_ffn_gu, w_ffn_d, w_router, w_moe_gu, w_moe_d, ln_g, ln_b):
    B, S, _ = x_prompt.shape
    DB, L, _ = x_sample.shape
    P = cache_diff_k.shape[1]
    bf = lambda w: w.astype(BF16)
    w_ab = jnp.pad(w_in1[:, 4 * C_W:], ((0, 0), (0, 128 - 2 * H_C)))
    p = dict(w_in0=bf(w_in0), ret_norm_g=ret_norm_g, diff_lambda=diff_lambda, diff_norm_g=diff_norm_g,
             w_out0=bf(w_out0), w_in1_main=bf(w_in1[:, :4 * C_W]), w_in1_ab=bf(w_ab), conv_w=conv_w,
             a_log=a_log, dt_bias=dt_bias, gdn_norm_g=gdn_norm_g, w_out1=bf(w_out1), w_xq=bf(w_xq),
             w_xo=bf(w_xo), w_ffn_gu=bf(w_ffn_gu), w_ffn_d=bf(w_ffn_d), w_router=w_router,
             w_moe_gu=w_moe_gu, w_moe_d=w_moe_d, ln_g=ln_g, ln_b=ln_b)

    mem = mem_prompt.reshape(B * N_MEM, D_MODEL)
    mk_p = _matmul(mem, bf(w_xkv[:, :, :D_MODEL]), F32, 1024, 1024, name="mem_k")
    mv_p = _matmul(mem, bf(w_xkv[:, :, D_MODEL:]), F32, 1024, 1024, name="mem_v")

    y_p, dk_p, dv_p, ret_p, conv_p, gdn_p = _trunk(
        x_prompt.reshape(B * S, D_MODEL), jnp.arange(S), B, mk_p, mv_p,
        jnp.zeros((B, H_A, DK_A, DV_A), F32), None, None,
        jnp.zeros((B, CONV_W - 1, 3 * C_W), F32), jnp.zeros((B, H_C, DK_C, DV_C), F32), p, PROMPT_CFG)

    pos_s = jnp.tile(P + jnp.arange(L), DB)

    def mem_rows(c):
        c = c.reshape(DEPTH, DB * N_MEM, X_HEADS, X_LANE_BLKS, 128).transpose(0, 1, 3, 2, 4)
        return c.reshape(DEPTH, DB * N_MEM * X_LANE_BLKS * X_HEADS, 128)

    y_s, dk_s, dv_s, ret_s, conv_s, gdn_s = _trunk(
        x_sample.reshape(DB * L, D_MODEL), pos_s, DB, mem_rows(cache_mem_k), mem_rows(cache_mem_v),
        state_ret.astype(F32), cache_diff_k.reshape(DB * P * H_B, DV_B), cache_diff_v.reshape(DB * P * H_B, DV_B),
        state_gdn_conv, state_gdn.astype(F32), p, STEP_CFG)

    shape5 = (DEPTH, B, N_MEM, X_HEADS, X_HD)
    return (y_p.reshape(B, S, D_MODEL), y_s.reshape(DB, L, D_MODEL),
            mk_p.reshape(shape5), mv_p.reshape(shape5),
            dk_p.reshape(B, S, H_B, 2 * DK_B), dv_p.reshape(B, S, H_B, DV_B),
            ret_p, conv_p, gdn_p,
            dk_s.reshape(DB, L, H_B, 2 * DK_B), dv_s.reshape(DB, L, H_B, DV_B),
            ret_s, conv_s, gdn_s)
```

```python
import functools
import math

import jax
import jax.numpy as jnp
from jax import lax
from jax.experimental import pallas as pl
from jax.experimental.pallas import tpu as pltpu

F32 = jnp.float32
BF16 = jnp.bfloat16
I32 = jnp.int32

D_MODEL = 1024
DEPTH = 2
CHUNK = 64
H_A, DK_A, DV_A = 4, 64, 128
RET_THETA = 10000.0
H_B, DK_B = 4, 64
DV_B = 2 * DK_B
ROT_B = DK_B // 4
ROPE_THETA = 500000.0
DIFF_LAMBDA_INIT = 0.8 - 0.6 * math.exp(-0.3 * 0)
H_C, DK_C, DV_C = 8, 128, 128
C_W = H_C * DV_C
CONV_W = 4
N_MEM = 256
X_HEADS = 4
X_HD = D_MODEL // X_HEADS
D_FF = 2816
N_EXPERTS = 8
D_FF_E = 3584
DN_ALPHA = (2 * DEPTH) ** 0.25
LN_EPS = 1e-5
A_QK = H_A * DK_A
A_V = H_A * DV_A
B_QK = H_B * 2 * DK_B
B_V = H_B * DV_B
PROJ0 = 2 * A_QK + 2 * A_V + 2 * B_QK + B_V

VMEM_LIMIT_V7X = 52 * 1024 * 1024
NEG_BIG = -1e30

MOE_TOK_TILE = 512
MOE_SUB = 32
MOE_FF_BLK = 512
MOE_SLOTS = 2 * MOE_TOK_TILE // MOE_SUB + N_EXPERTS
GDN_HEAD_GROUP = 4
CONV_ROWS = 32


def _cparams(sem):
    return pltpu.CompilerParams(dimension_semantics=sem, vmem_limit_bytes=VMEM_LIMIT_V7X)


def _dot(a, b):
    return jnp.dot(a, b, preferred_element_type=F32)


def _dot_nt(a, b):
    return lax.dot_general(a, b, (((1,), (1,)), ((), ())), preferred_element_type=F32)


def _dot_tn(a, b):
    return lax.dot_general(a, b, (((0,), (0,)), ((), ())), preferred_element_type=F32)


def _silu(x):
    return x * (1.0 / (1.0 + jnp.exp(-x)))


def _layer_norm_rows(y, g, b):
    mu = jnp.mean(y, axis=-1, keepdims=True)
    d = y - mu
    var = jnp.mean(d * d, axis=-1, keepdims=True)
    return d * lax.rsqrt(var + LN_EPS) * g + b


def _mm_kernel(x_ref, w_ref, o_ref, *, scale):
    acc = _dot(x_ref[...].astype(BF16), w_ref[0])
    if scale != 1.0:
        acc = acc * scale
    o_ref[0] = acc.astype(o_ref.dtype)


def _matmul(x, w, out_dtype, tm, tn, scale=1.0, name="matmul"):
    M, K = x.shape
    G, _, N = w.shape
    tm = min(tm, M)
    tn = min(tn, N)
    assert M % tm == 0 and N % tn == 0
    return pl.pallas_call(
        functools.partial(_mm_kernel, scale=scale),
        out_shape=jax.ShapeDtypeStruct((G, M, N), out_dtype),
        grid=(G, M // tm, N // tn),
        in_specs=[pl.BlockSpec((tm, K), lambda g, i, j: (i, 0)),
                  pl.BlockSpec((1, K, tn), lambda g, i, j: (g, 0, j))],
        out_specs=pl.BlockSpec((1, tm, tn), lambda g, i, j: (g, i, j)),
        compiler_params=_cparams(("parallel", "parallel", "arbitrary")),
        name=name,
    )(x, w)


def _mm_dn_kernel(*refs, n_in, next_scale):
    xs = refs[:n_in]
    ws = refs[n_in:2 * n_in]
    rest = refs[2 * n_in:]
    if next_scale is None:
        r_ref, g_ref, b_ref, o_ref, ob_ref = rest
    else:
        r_ref, g_ref, b_ref, wn_ref, o_ref, ob_ref = rest
    acc = DN_ALPHA * r_ref[...]
    for x_ref, w_ref in zip(xs, ws):
        acc = acc + _dot(x_ref[...].astype(BF16), w_ref[...])
    y = _layer_norm_rows(acc, g_ref[...], b_ref[...])
    o_ref[...] = y
    if next_scale is None:
        ob_ref[...] = y.astype(BF16)
    else:
        ob_ref[...] = (_dot(y.astype(BF16), wn_ref[...]) * next_scale).astype(BF16)


def _matmul_deepnorm(xs, ws, resid, g, b, tm, w_next=None, next_scale=None, name="matmul_deepnorm"):
    M = resid.shape[0]
    tm = min(tm, M)
    assert M % tm == 0
    n_in = len(xs)
    in_specs = [pl.BlockSpec((tm, x.shape[1]), lambda i: (i, 0)) for x in xs]
    in_specs += [pl.BlockSpec(w.shape, lambda i: (0, 0)) for w in ws]
    in_specs += [pl.BlockSpec((tm, D_MODEL), lambda i: (i, 0)),
                 pl.BlockSpec((1, D_MODEL), lambda i: (0, 0)),
                 pl.BlockSpec((1, D_MODEL), lambda i: (0, 0))]
    args = [*xs, *ws, resid, g.reshape(1, D_MODEL), b.reshape(1, D_MODEL)]
    if w_next is not None:
        in_specs.append(pl.BlockSpec(w_next.shape, lambda i: (0, 0)))
        args.append(w_next)
    return pl.pallas_call(
        functools.partial(_mm_dn_kernel, n_in=n_in, next_scale=next_scale),
        out_shape=[jax.ShapeDtypeStruct((M, D_MODEL), F32), jax.ShapeDtypeStruct((M, D_MODEL), BF16)],
        grid=(M // tm,),
        in_specs=in_specs,
        out_specs=[pl.BlockSpec((tm, D_MODEL), lambda i: (i, 0))] * 2,
        compiler_params=_cparams(("parallel",)),
        name=name,
    )(*args)


def _swap_halves(x, group, half):
    n = x.shape[-1]
    lane = lax.broadcasted_iota(I32, x.shape, x.ndim - 1) % group
    up = pltpu.roll(x, n - half, x.ndim - 1)
    dn = pltpu.roll(x, half, x.ndim - 1)
    return jnp.where(lane < half, up, dn)


def _store_rows(o_ref, y, by_head):
    if not by_head:
        o_ref[...] = y.astype(o_ref.dtype)
        return
    tm = y.shape[0]
    for h in range(H_B):
        o_ref[pl.ds(h, tm, stride=H_B), :] = y[:, h * DV_B:(h + 1) * DV_B].astype(o_ref.dtype)


def _proj_plain_kernel(x_ref, w_ref, o_ref, *, by_head):
    _store_rows(o_ref, _dot(x_ref[...], w_ref[...]), by_head)


def _proj_rot_kernel(x_ref, w_ref, c_ref, s_ref, o_ref, *, group, half, by_head):
    acc = _dot(x_ref[...], w_ref[...])
    _store_rows(o_ref, acc * c_ref[...] + _swap_halves(acc, group, half) * s_ref[...], by_head)


def _proj0_block(xb, w, out_dtype, tm, rot=None, by_head=False, name="proj0"):
    T = xb.shape[0]
    blk = w.shape[1]
    tm = min(tm, T)
    row = lambda i: (i, 0)
    in_specs = [pl.BlockSpec((tm, D_MODEL), row), pl.BlockSpec((D_MODEL, blk), lambda i: (0, 0))]
    args = [xb, w]
    if rot is None:
        body = functools.partial(_proj_plain_kernel, by_head=by_head)
    else:
        cos, sin, group, half = rot
        P = cos.shape[0]
        tm = min(tm, P)
        assert P % tm == 0
        npb = P // tm
        in_specs[0] = pl.BlockSpec((tm, D_MODEL), row)
        in_specs += [pl.BlockSpec((tm, blk), lambda i: (i % npb, 0))] * 2
        args += [cos, sin]
        body = functools.partial(_proj_rot_kernel, group=group, half=half, by_head=by_head)
    assert T % tm == 0
    if by_head:
        assert blk == H_B * DV_B
        out_shape = jax.ShapeDtypeStruct((T * H_B, DV_B), out_dtype)
        out_spec = pl.BlockSpec((tm * H_B, DV_B), row)
    else:
        out_shape = jax.ShapeDtypeStruct((T, blk), out_dtype)
        out_spec = pl.BlockSpec((tm, blk), row)
    return pl.pallas_call(
        body,
        out_shape=out_shape,
        grid=(T // tm,),
        in_specs=in_specs,
        out_specs=out_spec,
        compiler_params=_cparams(("parallel",)),
        name=name,
    )(*args)


def _rope_tables(pos):
    pos = pos.astype(F32)
    inv = RET_THETA ** (-jnp.arange(0, DK_A, 2, dtype=F32) / DK_A)
    ang = pos[:, None] * inv[None, :]
    c, s = jnp.cos(ang), jnp.sin(ang)
    c64 = jnp.concatenate([c, c], axis=-1)
    s64 = jnp.concatenate([-s, s], axis=-1)
    kscale = DK_A ** -0.5
    ca = jnp.concatenate([jnp.tile(c64, (1, H_A)), jnp.tile(c64, (1, H_A)) * kscale], axis=-1)
    sa = jnp.concatenate([jnp.tile(s64, (1, H_A)), jnp.tile(s64, (1, H_A)) * kscale], axis=-1)
    invb = ROPE_THETA ** (-jnp.arange(0, ROT_B, 2, dtype=F32) / ROT_B)
    angb = pos[:, None] * invb[None, :]
    cb, sb = jnp.cos(angb), jnp.sin(angb)
    rest = DK_B - ROT_B
    c64b = jnp.concatenate([cb, cb, jnp.ones((pos.shape[0], rest), F32)], axis=-1)
    s64b = jnp.concatenate([-sb, sb, jnp.zeros((pos.shape[0], rest), F32)], axis=-1)
    ck = jnp.tile(c64b, (1, 2 * H_B))
    sk = jnp.tile(s64b, (1, 2 * H_B))
    qscale = DK_B ** -0.5
    return ca, sa, ck * qscale, sk * qscale, ck, sk


def _proj0(xb, w_in0_bf, tables, tm):
    ca, sa, cq, sq, ck, sk = tables
    blk = 512
    assert PROJ0 == 6 * blk
    w = [w_in0_bf[:, j * blk:(j + 1) * blk] for j in range(6)]
    rot_a = (DK_A, DK_A // 2)
    rot_b = (DK_B, ROT_B // 2)
    qka = _proj0_block(xb, w[0], BF16, tm, (ca, sa) + rot_a, name="proj0_qk_ret")
    va = _proj0_block(xb, w[1], BF16, tm, name="proj0_v_ret")
    ga = _proj0_block(xb, w[2], F32, tm, name="proj0_gate_ret")
    qb = _proj0_block(xb, w[3], BF16, tm, (cq, sq) + rot_b, name="proj0_q_diff")
    kb = _proj0_block(xb, w[4], F32, tm, (ck, sk) + rot_b, by_head=True, name="proj0_k_diff")
    vb = _proj0_block(xb, w[5], F32, tm, by_head=True, name="proj0_v_diff")
    return qka, va, ga, qb, kb, vb


def _retention_kernel(qk_ref, v_ref, g_ref, dmat_ref, qdec_ref, kdec_ref, sdec_ref, s0_ref, ng_ref,
                      o_ref, sout_ref, s_scr):
    c = pl.program_id(1)

    @pl.when(c == 0)
    def _():
        s_scr[...] = s0_ref[0]

    for h in range(H_A):
        q = qk_ref[:, h * DK_A:(h + 1) * DK_A]
        k = qk_ref[:, A_QK + h * DK_A:A_QK + (h + 1) * DK_A]
        v = v_ref[:, h * DV_A:(h + 1) * DV_A]
        s = s_scr[h]
        scores = _dot_nt(q, k) * dmat_ref[h]
        o = _dot(scores.astype(BF16), v) + _dot(q, s.astype(BF16)) * qdec_ref[h]
        kd = (k.astype(F32) * kdec_ref[h]).astype(BF16)
        s_scr[h] = s * sdec_ref[h] + _dot_tn(kd, v)
        mu = jnp.mean(o, axis=-1, keepdims=True)
        d = o - mu
        var = jnp.mean(d * d, axis=-1, keepdims=True)
        gate = g_ref[:, h * DV_A:(h + 1) * DV_A]
        y = d * lax.rsqrt(var + LN_EPS) * ng_ref[:, h * DV_A:(h + 1) * DV_A] * _silu(gate)
        o_ref[:, h * DV_A:(h + 1) * DV_A] = y.astype(o_ref.dtype)

    @pl.when(c == pl.num_programs(1) - 1)
    def _():
        sout_ref[0] = s_scr[...]


def _retention_tables(lc):
    lg = jnp.log1p(-jnp.exp2(-5.0 - jnp.arange(H_A, dtype=F32)))
    t = jnp.arange(lc, dtype=F32)
    rel = t[:, None] - t[None, :]
    causal = rel >= 0
    dmat = jnp.where(causal, jnp.exp(lg[:, None, None] * jnp.where(causal, rel, 0.0)), 0.0)
    qdec = jnp.exp(lg[:, None] * (t[None, :] + 1.0))
    kdec = jnp.exp(lg[:, None] * (lc - 1.0 - t[None, :]))
    sdec = jnp.exp(lg * lc)
    return (dmat,
            jnp.broadcast_to(qdec[:, :, None], (H_A, lc, DV_A)),
            jnp.broadcast_to(kdec[:, :, None], (H_A, lc, DK_A)),
            jnp.broadcast_to(sdec[:, None, None], (H_A, DK_A, DV_A)))


def _retention(qka, va, ga, state0, ret_norm_g, n_batch, lc):
    T = qka.shape[0]
    nc = T // (n_batch * lc)
    assert nc * n_batch * lc == T
    dmat, qdec, kdec, sdec = _retention_tables(lc)
    row = lambda b, c: (b * nc + c, 0)
    const3 = lambda b, c: (0, 0, 0)
    return pl.pallas_call(
        _retention_kernel,
        out_shape=[jax.ShapeDtypeStruct((T, A_V), BF16),
                   jax.ShapeDtypeStruct((n_batch, H_A, DK_A, DV_A), F32)],
        grid=(n_batch, nc),
        in_specs=[pl.BlockSpec((lc, 2 * A_QK), row),
                  pl.BlockSpec((lc, A_V), row),
                  pl.BlockSpec((lc, A_V), row),
                  pl.BlockSpec((H_A, lc, lc), const3),
                  pl.BlockSpec((H_A, lc, DV_A), const3),
                  pl.BlockSpec((H_A, lc, DK_A), const3),
                  pl.BlockSpec((H_A, DK_A, DV_A), const3),
                  pl.BlockSpec((1, H_A, DK_A, DV_A), lambda b, c: (b, 0, 0, 0)),
                  pl.BlockSpec((1, A_V), lambda b, c: (0, 0))],
        out_specs=[pl.BlockSpec((lc, A_V), row),
                   pl.BlockSpec((1, H_A, DK_A, DV_A), lambda b, c: (b, 0, 0, 0))],
        scratch_shapes=[pltpu.VMEM((H_A, DK_A, DV_A), F32)],
        compiler_params=_cparams(("parallel", "arbitrary")),
        name="retention",
    )(qka, va, ga, dmat, qdec, kdec, sdec, state0, ret_norm_g.reshape(1, A_V))


def _diff_finish(acc1, l1, acc2, l2, lam, ng):
    o = acc1 * (1.0 / l1) - lam * (acc2 * (1.0 / l2))
    ms = jnp.mean(o * o, axis=-1, keepdims=True)
    return o * lax.rsqrt(ms + LN_EPS) * ng * (1.0 - DIFF_LAMBDA_INIT)


def _diff_prompt_kernel(lam_ref, q_ref, k_ref, v_ref, ng_ref, o_ref, kbf, vbf, *, tq):
    qi = pl.program_id(2)

    @pl.when(qi == 0)
    def _():
        h = pl.program_id(1)
        seq = kbf.shape[0]
        kbf[...] = k_ref[pl.ds(h, seq, stride=H_B), :].astype(BF16)
        vbf[...] = v_ref[pl.ds(h, seq, stride=H_B), :].astype(BF16)

    rc = lax.broadcasted_iota(I32, (tq, tq), 0) // CHUNK
    cc = lax.broadcasted_iota(I32, (tq, tq), 1) // CHUNK
    vis = cc <= rc
    nq = kbf.shape[0] // tq

    for j in range(nq):
        @pl.when(qi == j)
        def _(j=j):
            n_full = j * tq
            kd = kbf[n_full:n_full + tq, :]
            vd = vbf[n_full:n_full + tq, :]
            accs = []
            for c in range(2):
                sl = slice(c * DK_B, (c + 1) * DK_B)
                qc = q_ref[:, sl]
                sd = jnp.where(vis, _dot_nt(qc, kd[:, sl]), NEG_BIG)
                m = jnp.max(sd, axis=-1, keepdims=True)
                if j > 0:
                    sf = _dot_nt(qc, kbf[0:n_full, sl])
                    m = jnp.maximum(m, jnp.max(sf, axis=-1, keepdims=True))
                pd = jnp.exp(sd - m)
                l = jnp.sum(pd, axis=-1, keepdims=True)
                acc = _dot(pd.astype(BF16), vd)
                if j > 0:
                    pf = jnp.exp(sf - m)
                    l = l + jnp.sum(pf, axis=-1, keepdims=True)
                    acc = acc + _dot(pf.astype(BF16), vbf[0:n_full, :])
                accs += [acc, l]
            o_ref[...] = _diff_finish(accs[0], accs[1], accs[2], accs[3], lam_ref[0], ng_ref[...]).astype(o_ref.dtype)


def _diff_prompt(qb, kb, vb, lam, diff_norm_g, n_batch, seq, tq):
    T = qb.shape[0]
    assert seq % tq == 0 and tq % CHUNK == 0
    nq = seq // tq
    return pl.pallas_call(
        functools.partial(_diff_prompt_kernel, tq=tq),
        out_shape=jax.ShapeDtypeStruct((T, B_V), BF16),
        grid_spec=pltpu.PrefetchScalarGridSpec(
            num_scalar_prefetch=1,
            grid=(n_batch, H_B, nq),
            in_specs=[pl.BlockSpec((tq, DV_B), lambda b, h, i, lam: (b * nq + i, h)),
                      pl.BlockSpec((seq * H_B, DV_B), lambda b, h, i, lam: (b, 0)),
                      pl.BlockSpec((seq * H_B, DV_B), lambda b, h, i, lam: (b, 0)),
                      pl.BlockSpec((1, DV_B), lambda b, h, i, lam: (0, 0))],
            out_specs=pl.BlockSpec((tq, DV_B), lambda b, h, i, lam: (b * nq + i, h)),
            scratch_shapes=[pltpu.VMEM((seq, DV_B), BF16), pltpu.VMEM((seq, DV_B), BF16)]),
        compiler_params=_cparams(("parallel", "parallel", "arbitrary")),
        name="diff_attn_prompt",
    )(lam, qb, kb, vb, diff_norm_g.reshape(1, DV_B))


def _diff_step_kernel(lam_ref, q_ref, kp_ref, vp_ref, kn_ref, vn_ref, ng_ref, o_ref, *, past, ln):
    qchunk = (past + lax.broadcasted_iota(I32, (ln, 1), 0)) // CHUNK
    vis_p = (lax.broadcasted_iota(I32, (ln, past), 1) // CHUNK) <= qchunk
    vis_n = ((past + lax.broadcasted_iota(I32, (ln, ln), 1)) // CHUNK) <= qchunk
    for h in range(H_B):
        q = q_ref[:, h * DV_B:(h + 1) * DV_B]
        kp = kp_ref[pl.ds(h, past, stride=H_B), :].astype(BF16)
        vp = vp_ref[pl.ds(h, past, stride=H_B), :].astype(BF16)
        kn = kn_ref[pl.ds(h, ln, stride=H_B), :].astype(BF16)
        vn = vn_ref[pl.ds(h, ln, stride=H_B), :].astype(BF16)
        accs = []
        for c in range(2):
            sl = slice(c * DK_B, (c + 1) * DK_B)
            sp = jnp.where(vis_p, _dot_nt(q[:, sl], kp[:, sl]), NEG_BIG)
            sn = jnp.where(vis_n, _dot_nt(q[:, sl], kn[:, sl]), NEG_BIG)
            m = jnp.maximum(jnp.max(sp, axis=-1, keepdims=True), jnp.max(sn, axis=-1, keepdims=True))
            pp = jnp.exp(sp - m)
            pn = jnp.exp(sn - m)
            l = jnp.sum(pp, axis=-1, keepdims=True) + jnp.sum(pn, axis=-1, keepdims=True)
            accs += [_dot(pp.astype(BF16), vp) + _dot(pn.astype(BF16), vn), l]
        y = _diff_finish(accs[0], accs[1], accs[2], accs[3], lam_ref[0], ng_ref[...])
        o_ref[:, h * DV_B:(h + 1) * DV_B] = y.astype(o_ref.dtype)


def _diff_step(qb, kb, vb, past_k, past_v, lam, diff_norm_g, n_batch, ln):
    past = past_k.shape[0] // (n_batch * H_B)
    blk = lambda b, lam: (b, 0)
    return pl.pallas_call(
        functools.partial(_diff_step_kernel, past=past, ln=ln),
        out_shape=jax.ShapeDtypeStruct((n_batch * ln, B_V), BF16),
        grid_spec=pltpu.PrefetchScalarGridSpec(
            num_scalar_prefetch=1,
            grid=(n_batch,),
            in_specs=[pl.BlockSpec((ln, B_QK), blk),
                      pl.BlockSpec((past * H_B, DV_B), blk),
                      pl.BlockSpec((past * H_B, DV_B), blk),
                      pl.BlockSpec((ln * H_B, DV_B), blk),
                      pl.BlockSpec((ln * H_B, DV_B), blk),
                      pl.BlockSpec((1, DV_B), lambda b, lam: (0, 0))],
            out_specs=pl.BlockSpec((ln, B_V), blk)),
        compiler_params=_cparams(("parallel",)),
        name="diff_attn_step",
    )(lam, qb, past_k, past_v, kb, vb, diff_norm_g.reshape(1, DV_B))


X_LANE_BLKS = X_HD // 128


def _xattn_kernel(q_ref, mk_ref, mv_ref, o_ref, *, split):
    stride = X_LANE_BLKS * X_HEADS
    for h in range(X_HEADS):
        if split:
            pieces = [(slice(h * X_HD + j * 128, h * X_HD + (j + 1) * 128), pl.ds(j * X_HEADS + h, N_MEM, stride=stride))
                      for j in range(X_LANE_BLKS)]
            s = sum(_dot_nt(q_ref[:, cs], mk_ref[rs, :].astype(BF16)) for cs, rs in pieces)
        else:
            sl = slice(h * X_HD, (h + 1) * X_HD)
            s = _dot_nt(q_ref[:, sl], mk_ref[:, sl].astype(BF16))
        p = jnp.exp(s - jnp.max(s, axis=-1, keepdims=True))
        inv_l = 1.0 / jnp.sum(p, axis=-1, keepdims=True)
        pb = p.astype(BF16)
        if split:
            for cs, rs in pieces:
                o_ref[:, cs] = (_dot(pb, mv_ref[rs, :].astype(BF16)) * inv_l).astype(o_ref.dtype)
        else:
            o_ref[:, sl] = (_dot(pb, mv_ref[:, sl].astype(BF16)) * inv_l).astype(o_ref.dtype)


def _xattn(q, mk, mv, layer, n_batch, tq, split):
    T = q.shape[0]
    per = T // n_batch
    tq = min(tq, per)
    nt = per // tq
    mem_blk = (None, N_MEM * X_LANE_BLKS * X_HEADS, 128) if split else (None, N_MEM, D_MODEL)
    return pl.pallas_call(
        functools.partial(_xattn_kernel, split=split),
        out_shape=jax.ShapeDtypeStruct((T, D_MODEL), BF16),
        grid=(n_batch, nt),
        in_specs=[pl.BlockSpec((tq, D_MODEL), lambda b, t: (b * nt + t, 0)),
                  pl.BlockSpec(mem_blk, lambda b, t: (layer, b, 0)),
                  pl.BlockSpec(mem_blk, lambda b, t: (layer, b, 0))],
        out_specs=pl.BlockSpec((tq, D_MODEL), lambda b, t: (b * nt + t, 0)),
        compiler_params=_cparams(("parallel", "arbitrary")),
        name="mem_xattn",
    )(q, mk, mv)


def _swiglu_up_kernel(x_ref, wg_ref, wu_ref, o_ref):
    xb = x_ref[...]
    g = _dot(xb, wg_ref[...])
    u = _dot(xb, wu_ref[...])
    o_ref[...] = (_silu(g) * u).astype(o_ref.dtype)


def _swiglu_up(xb, w_gu_bf, tm, tn):
    T = xb.shape[0]
    tm = min(tm, T)
    nj = D_FF // tn
    assert D_FF % tn == 0
    return pl.pallas_call(
        _swiglu_up_kernel,
        out_shape=jax.ShapeDtypeStruct((T, D_FF), BF16),
        grid=(T // tm, nj),
        in_specs=[pl.BlockSpec((tm, D_MODEL), lambda i, j: (i, 0)),
                  pl.BlockSpec((D_MODEL, tn), lambda i, j: (0, j)),
                  pl.BlockSpec((D_MODEL, tn), lambda i, j: (0, nj + j))],
        out_specs=pl.BlockSpec((tm, tn), lambda i, j: (i, j)),
        compiler_params=_cparams(("parallel", "arbitrary")),
        name="swiglu_up",
    )(xb, w_gu_bf, w_gu_bf)


def _conv_kernel(x_ref, prev_ref, st_ref, w_ref, o_ref, buf, *, tm):
    t = pl.program_id(1)
    cb = pl.program_id(2)
    buf[0:8, :] = jnp.where(t == 0, st_ref[0], prev_ref[...])
    buf[8:8 + tm, :] = x_ref[...]
    scale = jnp.where(cb == 0, DK_C ** -0.5, 1.0)
    rc = min(CONV_ROWS, tm)
    for h in range(H_C):
        cs = slice(h * DK_C, (h + 1) * DK_C)
        w = [w_ref[i:i + 1, cs] for i in range(CONV_W)]
        for r0 in range(0, tm, rc):
            y = buf[8 + r0:8 + r0 + rc, cs] * w[CONV_W - 1]
            for i in range(CONV_W - 1):
                y = y + buf[5 + i + r0:5 + i + r0 + rc, cs] * w[i]
            y = _silu(y)
            ss = jnp.sum(y * y, axis=-1, keepdims=True)
            f = jnp.where(cb == 2, 1.0, lax.rsqrt(ss + 1e-6) * scale)
            o_ref[r0:r0 + rc, cs] = (y * f).astype(o_ref.dtype)


def _gdn_conv(proj1, conv_state8, conv_w, n_batch, tm):
    T = proj1.shape[0]
    per = T // n_batch
    tm = min(tm, per)
    nt = per // tm
    assert per % tm == 0 and tm % 8 == 0
    return pl.pallas_call(
        functools.partial(_conv_kernel, tm=tm),
        out_shape=jax.ShapeDtypeStruct((T, 3 * C_W), BF16),
        grid=(n_batch, nt, 3),
        in_specs=[pl.BlockSpec((tm, C_W), lambda b, t, c: (b * nt + t, c)),
                  pl.BlockSpec((8, C_W), lambda b, t, c: (jnp.maximum((b * nt + t) * (tm // 8) - 1, 0), c)),
                  pl.BlockSpec((1, 8, C_W), lambda b, t, c: (b, 0, c)),
                  pl.BlockSpec((CONV_W, C_W), lambda b, t, c: (0, c))],
        out_specs=pl.BlockSpec((tm, C_W), lambda b, t, c: (b * nt + t, c)),
        scratch_shapes=[pltpu.VMEM((8 + tm, C_W), F32)],
        compiler_params=_cparams(("parallel", "arbitrary", "arbitrary")),
        name="gdn_conv",
    )(proj1, proj1, conv_state8, conv_w)


def _gdn_kernel(q_ref, k_ref, v_ref, z_ref, ab_ref, alog_ref, dtb_ref, ng_ref, s0_ref,
                o_ref, sout_ref, s_scr, *, lc, l_real):
    c = pl.program_id(1)

    @pl.when(c == 0)
    def _():
        s_scr[...] = s0_ref[0]

    ri = lax.broadcasted_iota(I32, (lc, lc), 0)
    ci = lax.broadcasted_iota(I32, (lc, lc), 1)
    incl = ci <= ri
    strict = ci < ri
    eye = (ci == ri).astype(F32)
    ab = ab_ref[...]
    sp = jnp.maximum(ab + dtb_ref[...], 0.0) + jnp.log1p(jnp.exp(-jnp.abs(ab + dtb_ref[...])))
    glog = -jnp.exp(alog_ref[...]) * sp
    if l_real < lc:
        live = lax.broadcasted_iota(I32, (lc, 1), 0) < l_real
        glog = jnp.where(live, glog, 0.0)
    gcum = jnp.dot(incl.astype(F32), glog, preferred_element_type=F32, precision=lax.Precision.HIGHEST)
    gcum_t = gcum.T
    beta_all = 1.0 / (1.0 + jnp.exp(-ab))

    levels = []
    s = 1
    while s < lc:
        levels.append(((ri // (2 * s)) == (ci // (2 * s))) & (((ri // s) % 2) == 1) & (((ci // s) % 2) == 0))
        s *= 2

    group = GDN_HEAD_GROUP if lc > 128 else H_C
    for h0 in range(0, H_C, group):
        heads = range(h0, h0 + group)
        sls = [slice(h * DK_C, (h + 1) * DK_C) for h in heads]
        ks, kfs, vs, gcols, decays, egs, kbs, a_lows = [], [], [], [], [], [], [], []
        for h, sl in zip(heads, sls):
            k = k_ref[:, sl]
            v = v_ref[:, sl].astype(F32)
            if l_real < lc:
                k = jnp.where(live, k, jnp.zeros_like(k))
                v = jnp.where(live, v, 0.0)
            kf = k.astype(F32)
            gcol = gcum[:, h:h + 1]
            grow = gcum_t[h:h + 1, :]
            beta = beta_all[:, H_C + h:H_C + h + 1]
            decay = jnp.exp(jnp.where(incl, gcol - grow, NEG_BIG))
            kb = kf * beta
            ks.append(k)
            kfs.append(kf)
            gcols.append(gcol)
            decays.append(decay)
            egs.append(jnp.exp(gcol))
            kbs.append(kb)
            vs.append(v * beta)
            a_lows.append(jnp.where(strict, _dot_nt(kb.astype(BF16), k) * decay, 0.0))
        tinvs = [eye - jnp.where(levels[0], a, 0.0) for a in a_lows]
        for lvl, off in enumerate(levels[1:], start=1):
            s = 2 ** lvl
            tbs = [t.astype(BF16) for t in tinvs]
            if s < 8:
                a_offs = [jnp.where(off, a, 0.0).astype(BF16) for a in a_lows]
                ws = [_dot(a, t).astype(BF16) for a, t in zip(a_offs, tbs)]
                tinvs = [t - _dot(tb, w) for t, tb, w in zip(tinvs, tbs, ws)]
            else:
                odd = [slice((2 * j + 1) * s, (2 * j + 2) * s) for j in range(lc // (2 * s))]
                even = [slice(2 * j * s, (2 * j + 1) * s) for j in range(lc // (2 * s))]
                take = lambda x: jnp.concatenate([x[sl, :] for sl in odd], axis=0)
                rh = lax.broadcasted_iota(I32, (lc // 2, lc), 0)
                ch = lax.broadcasted_iota(I32, (lc // 2, lc), 1)
                off_odd = (ch // s) == 2 * (rh // s)
                zero = jnp.zeros((s, lc), F32)
                new = []
                for a, t, tb in zip(a_lows, tinvs, tbs):
                    w_odd = _dot(jnp.where(off_odd, take(a), 0.0).astype(BF16), tb)
                    w_full = jnp.concatenate(
                        [piece for j in range(len(odd)) for piece in (zero, w_odd[j * s:(j + 1) * s, :])], axis=0)
                    t_odd = take(t)
                    t_odd = t_odd - _dot(t_odd.astype(BF16), w_full.astype(BF16))
                    new.append(jnp.concatenate(
                        [piece for j, ev in enumerate(even) for piece in (t[ev, :], t_odd[j * s:(j + 1) * s, :])],
                        axis=0))
                tinvs = new
        rhss = [jnp.concatenate([vb, kb * eg], axis=-1).astype(BF16) for vb, kb, eg in zip(vs, kbs, egs)]
        sols = [_dot(t.astype(BF16), r) for t, r in zip(tinvs, rhss)]
        sts = [s_scr[h] for h in heads]
        stbs = [st.astype(BF16) for st in sts]
        ubs = [(sol[:, :DV_C] - _dot(sol[:, DV_C:].astype(BF16), stb)).astype(BF16) for sol, stb in zip(sols, stbs)]
        qs = [q_ref[:, sl] for sl in sls]
        qks = [(_dot_nt(q, k) * decay).astype(BF16) for q, k, decay in zip(qs, ks, decays)]
        os_ = [_dot((q.astype(F32) * eg).astype(BF16), stb) + _dot(qk, ub)
               for q, eg, stb, qk, ub in zip(qs, egs, stbs, qks, ubs)]
        for h, st, kf, gcol, ub in zip(heads, sts, kfs, gcols, ubs):
            g_last = gcol[lc - 1:lc, :]
            kdec = (kf * jnp.exp(g_last - gcol)).astype(BF16)
            s_scr[h] = st * jnp.exp(g_last) + _dot_tn(kdec, ub)
        for sl, o in zip(sls, os_):
            ms = jnp.mean(o * o, axis=-1, keepdims=True)
            y = o * lax.rsqrt(ms + LN_EPS) * ng_ref[...] * _silu(z_ref[:, sl])
            o_ref[:, sl] = y.astype(o_ref.dtype)

    @pl.when(c == pl.num_programs(1) - 1)
    def _():
        sout_ref[0] = s_scr[...]


def _gdn(qkvn, proj1, ab, a_log, dt_bias, gdn_norm_g, state0, n_batch, lc, l_real):
    T = qkvn.shape[0]
    nc = T // (n_batch * lc)
    assert nc * n_batch * lc == T and (nc == 1 or l_real == lc)
    pad = lambda v: jnp.pad(v.astype(F32).reshape(1, H_C), ((0, 0), (0, 128 - H_C)))
    return pl.pallas_call(
        functools.partial(_gdn_kernel, lc=lc, l_real=l_real),
        out_shape=[jax.ShapeDtypeStruct((T, C_W), BF16),
                   jax.ShapeDtypeStruct((n_batch, H_C, DK_C, DV_C), F32)],
        grid=(n_batch, nc),
        in_specs=[pl.BlockSpec((lc, C_W), lambda b, c: (b * nc + c, 0)),
                  pl.BlockSpec((lc, C_W), lambda b, c: (b * nc + c, 1)),
                  pl.BlockSpec((lc, C_W), lambda b, c: (b * nc + c, 2)),
                  pl.BlockSpec((lc, C_W), lambda b, c: (b * nc + c, 3)),
                  pl.BlockSpec((lc, 128), lambda b, c: (b * nc + c, 0)),
                  pl.BlockSpec((1, 128), lambda b, c: (0, 0)),
                  pl.BlockSpec((1, 128), lambda b, c: (0, 0)),
                  pl.BlockSpec((1, DV_C), lambda b, c: (0, 0)),
                  pl.BlockSpec((1, H_C, DK_C, DV_C), lambda b, c: (b, 0, 0, 0))],
        out_specs=[pl.BlockSpec((lc, C_W), lambda b, c: (b * nc + c, 0)),
                   pl.BlockSpec((1, H_C, DK_C, DV_C), lambda b, c: (b, 0, 0, 0))],
        scratch_shapes=[pltpu.VMEM((H_C, DK_C, DV_C), F32)],
        compiler_params=_cparams(("parallel", "arbitrary")),
        name="gated_delta",
    )(qkvn, qkvn, qkvn, proj1, ab, pad(a_log), pad(dt_bias), gdn_norm_g.reshape(1, DV_C), state0)


def _router_kernel(x_ref, wt_ref, rt_ref, rtt_ref, cnt_ref):
    tt = x_ref.shape[0]
    logits = lax.dot_general(wt_ref[...], x_ref[...], (((1,), (1,)), ((), ())),
                             preferred_element_type=F32, precision=lax.Precision.HIGHEST)
    eid = lax.broadcasted_iota(I32, (N_EXPERTS, tt), 0).astype(F32)
    m1 = jnp.max(logits, axis=0, keepdims=True)
    e1 = jnp.min(jnp.where(logits == m1, eid, float(N_EXPERTS)), axis=0, keepdims=True)
    rest = jnp.where(eid == e1, -jnp.inf, logits)
    m2 = jnp.max(rest, axis=0, keepdims=True)
    e2 = jnp.min(jnp.where(rest == m2, eid, float(N_EXPERTS)), axis=0, keepdims=True)
    ev = jnp.exp(m2 - m1)
    g1 = 1.0 / (1.0 + ev)
    g2 = ev / (1.0 + ev)
    hit1 = eid == e1
    hit2 = eid == e2
    member = jnp.where(hit1 | hit2, 1.0, 0.0)
    before = (lax.broadcasted_iota(I32, (tt, tt), 0) < lax.broadcasted_iota(I32, (tt, tt), 1))
    rank = _dot(member.astype(BF16), jnp.where(before, 1.0, 0.0).astype(BF16))
    r1 = jnp.sum(jnp.where(hit1, rank, 0.0), axis=0, keepdims=True)
    r2 = jnp.sum(jnp.where(hit2, rank, 0.0), axis=0, keepdims=True)
    rows = jnp.concatenate([e1, e2, r1, r2, g1, g2, jnp.zeros((2, tt), F32)], axis=0)
    rt_ref[0] = rows
    rtt_ref[0] = jnp.concatenate([rows, jnp.zeros((128 - 8, tt), F32)], axis=0).T
    cnt = jnp.sum(member, axis=1, keepdims=True).astype(I32)
    cnt_ref[0] = jnp.broadcast_to(cnt, (N_EXPERTS, 128))


def _router(x, w_router):
    T = x.shape[0]
    tt = MOE_TOK_TILE
    nt = T // tt
    assert T % tt == 0
    return pl.pallas_call(
        _router_kernel,
        out_shape=[jax.ShapeDtypeStruct((nt, 8, tt), F32),
                   jax.ShapeDtypeStruct((nt, tt, 128), F32),
                   jax.ShapeDtypeStruct((nt, N_EXPERTS, 128), I32)],
        grid=(nt,),
        in_specs=[pl.BlockSpec((tt, D_MODEL), lambda i: (i, 0)),
                  pl.BlockSpec((N_EXPERTS, D_MODEL), lambda i: (0, 0))],
        out_specs=[pl.BlockSpec((1, 8, tt), lambda i: (i, 0, 0)),
                   pl.BlockSpec((1, tt, 128), lambda i: (i, 0, 0)),
                   pl.BlockSpec((1, N_EXPERTS, 128), lambda i: (i, 0, 0))],
        compiler_params=_cparams(("parallel",)),
        name="moe_router",
    )(x, w_router.T)


def _moe_tables(cnt, n_tiles, group):
    E = N_EXPERTS
    nb_max = (2 * n_tiles * MOE_TOK_TILE) // MOE_SUB + n_tiles * E
    nb_tot = -(-(nb_max + E * (group - 1)) // group) * group
    nblk = (cnt + MOE_SUB - 1) // MOE_SUB
    nbe = jnp.sum(nblk, axis=0)
    nbe_pad = (nbe + group - 1) // group * group
    ends_e = jnp.cumsum(nbe_pad)
    base = ends_e - nbe_pad
    dest0 = base[None, :] + jnp.cumsum(nblk, axis=0) - nblk
    ends_t = jnp.cumsum(nblk, axis=1)
    kk = jnp.arange(MOE_SLOTS, dtype=I32)
    slot_e = jnp.minimum(jnp.sum(ends_t[:, None, :] <= kk[None, :, None], axis=-1), E - 1).astype(I32)
    slot_valid = kk[None, :] < ends_t[:, -1:]
    start_t = jnp.take_along_axis(ends_t - nblk, slot_e, axis=1)
    slot_j = jnp.where(slot_valid, kk[None, :] - start_t, 0)
    slot_dest = jnp.where(slot_valid, jnp.take_along_axis(dest0, slot_e, axis=1) + slot_j, 0)
    npad = nbe_pad - nbe
    pm = jnp.arange(E * (group - 1), dtype=I32)
    pe, pt = pm // max(group - 1, 1), pm % max(group - 1, 1)
    pad_valid = pt < npad[pe]
    pad_dest = jnp.where(pad_valid, base[pe] + nbe[pe] + pt, 0)
    n_rt = nb_tot // group
    r0 = jnp.arange(n_rt, dtype=I32) * group
    rt_valid = r0 < ends_e[-1]
    rt_e = jnp.minimum(jnp.sum(ends_e[None, :] <= r0[:, None], axis=-1), E - 1)
    last_e = jnp.minimum(jnp.sum(ends_e <= ends_e[-1] - 1), E - 1)
    rt_e = jnp.where(rt_valid, rt_e, last_e)
    i32 = lambda a: a.astype(I32).reshape(-1)
    return dict(nb_tot=nb_tot, n_rt=n_rt, slot_e=i32(slot_e), slot_j=i32(slot_j), slot_dest=i32(slot_dest),
                slot_valid=i32(slot_valid), pad_dest=i32(pad_dest), pad_valid=i32(pad_valid),
                rt_e=i32(rt_e), rt_valid=i32(rt_valid), n_valid_rt=i32(ends_e[-1] // group))


def _moe_gather_kernel(se_ref, sj_ref, sd_ref, sv_ref, pd_ref, pv_ref, nv_ref, x_ref, rt_ref, xg_hbm,
                       buf, zbuf, sem, zsem, *, n_pad, n_rt, rows):
    i = pl.program_id(0)
    rt = rt_ref[0]
    height = MOE_SLOTS * MOE_SUB
    row = lax.broadcasted_iota(I32, (height, 1), 0)
    row_e = jnp.full((height, 1), -1.0, F32)
    row_r = (row % MOE_SUB).astype(F32)
    for k in range(MOE_SLOTS):
        idx = i * MOE_SLOTS + k
        here = (row // MOE_SUB) == k
        row_e = jnp.where(here, jnp.where(sv_ref[idx] == 1, se_ref[idx], -1).astype(F32), row_e)
        row_r = row_r + jnp.where(here, (sj_ref[idx] * MOE_SUB).astype(F32), 0.0)
    sel = ((rt[0:1, :] == row_e) & (rt[2:3, :] == row_r)) | ((rt[1:2, :] == row_e) & (rt[3:4, :] == row_r))
    buf[...] = _dot(jnp.where(sel, 1.0, 0.0).astype(BF16), x_ref[...]).astype(BF16)

    def slot_copy(k):
        dst = pl.multiple_of(sd_ref[i * MOE_SLOTS + k] * MOE_SUB, MOE_SUB)
        return pltpu.make_async_copy(buf.at[pl.ds(k * MOE_SUB, MOE_SUB)], xg_hbm.at[pl.ds(dst, MOE_SUB)], sem.at[k])

    for k in range(MOE_SLOTS):
        @pl.when(sv_ref[i * MOE_SLOTS + k] == 1)
        def _(k=k):
            slot_copy(k).start()

    for k in range(MOE_SLOTS):
        @pl.when(sv_ref[i * MOE_SLOTS + k] == 1)
        def _(k=k):
            slot_copy(k).wait()

    def pad_copy(m):
        dst = pl.multiple_of(pd_ref[m] * MOE_SUB, MOE_SUB)
        return pltpu.make_async_copy(zbuf.at[pl.ds(0, MOE_SUB)], xg_hbm.at[pl.ds(dst, MOE_SUB)], zsem.at[0])

    def tail_copy(r):
        dst = pl.multiple_of(r * rows, rows)
        return pltpu.make_async_copy(zbuf, xg_hbm.at[pl.ds(dst, rows)], zsem.at[1])

    @pl.when(i == pl.num_programs(0) - 1)
    def _():
        zbuf[...] = jnp.zeros_like(zbuf)

        def pad_start(m, carry):
            @pl.when(pv_ref[m] == 1)
            def _():
                pad_copy(m).start()
            return carry

        def pad_wait(m, carry):
            @pl.when(pv_ref[m] == 1)
            def _():
                pad_copy(m).wait()
            return carry

        def tail_start(r, carry):
            tail_copy(r).start()
            return carry

        def tail_wait(r, carry):
            tail_copy(r).wait()
            return carry

        lax.fori_loop(0, n_pad, pad_start, 0)
        lax.fori_loop(nv_ref[0], n_rt, tail_start, 0)
        lax.fori_loop(0, n_pad, pad_wait, 0)
        lax.fori_loop(nv_ref[0], n_rt, tail_wait, 0)


def _moe_gather(xb, rt, tb, rows):
    nb = tb["nb_tot"]
    tt = MOE_TOK_TILE
    nt = xb.shape[0] // tt
    n_pad = tb["pad_dest"].shape[0]
    return pl.pallas_call(
        functools.partial(_moe_gather_kernel, n_pad=n_pad, n_rt=tb["n_rt"], rows=rows),
        out_shape=jax.ShapeDtypeStruct((nb * MOE_SUB, D_MODEL), BF16),
        grid_spec=pltpu.PrefetchScalarGridSpec(
            num_scalar_prefetch=7,
            grid=(nt,),
            in_specs=[pl.BlockSpec((tt, D_MODEL), lambda i, *_: (i, 0)),
                      pl.BlockSpec((1, 8, tt), lambda i, *_: (i, 0, 0))],
            out_specs=pl.BlockSpec(memory_space=pl.ANY),
            scratch_shapes=[pltpu.VMEM((MOE_SLOTS * MOE_SUB, D_MODEL), BF16),
                            pltpu.VMEM((rows, D_MODEL), BF16),
                            pltpu.SemaphoreType.DMA((MOE_SLOTS,)),
                            pltpu.SemaphoreType.DMA((2,))]),
        compiler_params=_cparams(("arbitrary",)),
        name="moe_gather",
    )(tb["slot_e"], tb["slot_j"], tb["slot_dest"], tb["slot_valid"], tb["pad_dest"], tb["pad_valid"],
      tb["n_valid_rt"], xb, rt)


def _moe_ffn_kernel(e_ref, valid_ref, x_ref, wg_ref, wu_ref, wd_ref, y_ref, acc, *, rows):
    r = pl.program_id(0)
    f = pl.program_id(1)
    nf = pl.num_programs(1)
    ok = valid_ref[r] == 1

    @pl.when(jnp.logical_and(ok, f == 0))
    def _():
        acc[...] = jnp.zeros_like(acc)

    @pl.when(ok)
    def _():
        xb = x_ref[...]
        g = _dot(xb, wg_ref[0].astype(BF16))
        u = _dot(xb, wu_ref[0].astype(BF16))
        acc[...] += _dot((_silu(g) * u).astype(BF16), wd_ref[0].astype(BF16))

    @pl.when(jnp.logical_and(ok, f == nf - 1))
    def _():
        y_ref[...] = acc[...].astype(y_ref.dtype)

    @pl.when(jnp.logical_and(jnp.logical_not(ok), f == nf - 1))
    def _():
        y_ref[...] = jnp.zeros_like(y_ref)


def _moe_ffn(xg, w_gu_bf, w_d_bf, tb, rows):
    n_rt = tb["n_rt"]
    nf = D_FF_E // MOE_FF_BLK
    fb = MOE_FF_BLK

    def fsel(r, f, v):
        return jnp.where(v[r] == 1, f, nf - 1)

    return pl.pallas_call(
        functools.partial(_moe_ffn_kernel, rows=rows),
        out_shape=jax.ShapeDtypeStruct(xg.shape, BF16),
        grid_spec=pltpu.PrefetchScalarGridSpec(
            num_scalar_prefetch=2,
            grid=(n_rt, nf),
            in_specs=[pl.BlockSpec((rows, D_MODEL), lambda r, f, e, v: (r, 0)),
                      pl.BlockSpec((1, D_MODEL, fb), lambda r, f, e, v: (e[r], 0, fsel(r, f, v))),
                      pl.BlockSpec((1, D_MODEL, fb), lambda r, f, e, v: (e[r], 0, nf + fsel(r, f, v))),
                      pl.BlockSpec((1, fb, D_MODEL), lambda r, f, e, v: (e[r], fsel(r, f, v), 0))],
            out_specs=pl.BlockSpec((rows, D_MODEL), lambda r, f, e, v: (r, 0)),
            scratch_shapes=[pltpu.VMEM((rows, D_MODEL), F32)]),
        compiler_params=_cparams(("arbitrary", "arbitrary")),
        name="moe_grouped_ffn",
    )(tb["rt_e"], tb["rt_valid"], xg, w_gu_bf, w_gu_bf, w_d_bf)


def _moe_combine_kernel(se_ref, sj_ref, sd_ref, sv_ref, *refs):
    y_refs = refs[:MOE_SLOTS]
    rtt_ref, x_ref, g_ref, b_ref, o_ref, ybuf = refs[MOE_SLOTS:]
    i = pl.program_id(0)
    width = MOE_SLOTS * MOE_SUB
    lane = lax.broadcasted_iota(I32, (1, width), 1)
    lane_e = jnp.full((1, width), -1.0, F32)
    lane_r = (lane % MOE_SUB).astype(F32)
    for k in range(MOE_SLOTS):
        idx = i * MOE_SLOTS + k
        valid = sv_ref[idx] == 1
        yk = y_refs[k][...]
        ybuf[k * MOE_SUB:(k + 1) * MOE_SUB, :] = jnp.where(valid, yk, jnp.zeros_like(yk))
        here = (lane // MOE_SUB) == k
        lane_e = jnp.where(here, jnp.where(valid, se_ref[idx], -1).astype(F32), lane_e)
        lane_r = lane_r + jnp.where(here, (sj_ref[idx] * MOE_SUB).astype(F32), 0.0)
    rtt = rtt_ref[0]
    a1 = jnp.where((rtt[:, 0:1] == lane_e) & (rtt[:, 2:3] == lane_r), 1.0, 0.0).astype(BF16)
    a2 = jnp.where((rtt[:, 1:2] == lane_e) & (rtt[:, 3:4] == lane_r), 1.0, 0.0).astype(BF16)
    yb = ybuf[...]
    ff = rtt[:, 4:5] * _dot(a1, yb) + rtt[:, 5:6] * _dot(a2, yb)
    o_ref[...] = _layer_norm_rows(DN_ALPHA * x_ref[...] + ff, g_ref[...], b_ref[...])


def _moe_combine(yg, rtt, x, g, b, tb):
    tt = MOE_TOK_TILE
    T = x.shape[0]

    def y_map(i, se, sj, sd, sv, *, k):
        return (sd[i * MOE_SLOTS + k], 0)

    y_specs = [pl.BlockSpec((MOE_SUB, D_MODEL), functools.partial(y_map, k=k)) for k in range(MOE_SLOTS)]
    return pl.pallas_call(
        _moe_combine_kernel,
        out_shape=jax.ShapeDtypeStruct((T, D_MODEL), F32),
        grid_spec=pltpu.PrefetchScalarGridSpec(
            num_scalar_prefetch=4,
            grid=(T // tt,),
            in_specs=y_specs + [pl.BlockSpec((1, tt, 128), lambda i, *_: (i, 0, 0)),
                                pl.BlockSpec((tt, D_MODEL), lambda i, *_: (i, 0)),
                                pl.BlockSpec((1, D_MODEL), lambda i, *_: (0, 0)),
                                pl.BlockSpec((1, D_MODEL), lambda i, *_: (0, 0))],
            out_specs=pl.BlockSpec((tt, D_MODEL), lambda i, *_: (i, 0)),
            scratch_shapes=[pltpu.VMEM((MOE_SLOTS * MOE_SUB, D_MODEL), BF16)]),
        compiler_params=_cparams(("arbitrary",)),
        name="moe_combine",
    )(tb["slot_e"], tb["slot_j"], tb["slot_dest"], tb["slot_valid"], *([yg] * MOE_SLOTS), rtt, x,
      g.reshape(1, D_MODEL), b.reshape(1, D_MODEL))


def _moe(x, xb, w_router, w_gu_bf, w_d_bf, g, b, rows):
    T = x.shape[0]
    nt = T // MOE_TOK_TILE
    rt, rtt, cnt = _router(x, w_router)
    tb = _moe_tables(cnt[:, :, 0], nt, rows // MOE_SUB)
    xg = _moe_gather(xb, rt, tb, rows)
    yg = _moe_ffn(xg, w_gu_bf, w_d_bf, tb, rows)
    return _moe_combine(yg, rtt, x, g, b, tb)


def _trunk(x, pos_rows, n_batch, mem_k, mem_v, ret_state, past_k, past_v, conv_state, gdn_state, p, cfg):
    T = x.shape[0]
    per = T // n_batch
    tm = cfg["tm"]
    tm_mm = cfg["tm_mm"]
    ln_g, ln_b = p["ln_g"], p["ln_b"]

    qka, va, ga, qb, kb, vb = _proj0(x.astype(BF16), p["w_in0"], _rope_tables(pos_rows), tm_mm)
    oa, ret_new = _retention(qka, va, ga, ret_state, p["ret_norm_g"], n_batch, cfg["ret_lc"])
    lp = p["diff_lambda"].astype(F32)
    lam = (jnp.exp(jnp.sum(lp[0] * lp[1])) - jnp.exp(jnp.sum(lp[2] * lp[3])) + DIFF_LAMBDA_INIT).reshape(1)
    if past_k is None:
        ob = _diff_prompt(qb, kb, vb, lam, p["diff_norm_g"], n_batch, per, cfg["diff_tq"])
    else:
        ob = _diff_step(qb, kb, vb, past_k, past_v, lam, p["diff_norm_g"], n_batch, per)
    x, q = _matmul_deepnorm([oa, ob], [p["w_out0"][:A_V], p["w_out0"][A_V:]], x, ln_g[0, 0], ln_b[0, 0], tm,
                            w_next=p["w_xq"][0], next_scale=X_HD ** -0.5, name="out0_deepnorm_xq")
    xa = _xattn(q, mem_k, mem_v, 0, n_batch, cfg["x_tq"], cfg["mem_split"])
    x, xb = _matmul_deepnorm([xa], [p["w_xo"][0]], x, ln_g[0, 1], ln_b[0, 1], tm, name="xo0_deepnorm")
    hmid = _swiglu_up(xb, p["w_ffn_gu"], cfg["ffn_tm"], cfg["ffn_tn"])
    x, xb = _matmul_deepnorm([hmid], [p["w_ffn_d"]], x, ln_g[0, 2], ln_b[0, 2], tm, name="ffn_down_deepnorm")

    proj1 = _matmul(xb, p["w_in1_main"][None], F32, tm_mm, 1024, name="proj1")[0]
    ab = _matmul(xb, p["w_in1_ab"][None], F32, tm_mm, 128, name="proj1_gates")[0]
    qkv3 = proj1.reshape(n_batch, per, 4 * C_W)[:, :, :3 * C_W]
    conv_new = qkv3[:, per - (CONV_W - 1):, :]
    state8 = jnp.pad(conv_state.astype(F32), ((0, 0), (8 - (CONV_W - 1), 0), (0, 0)))
    qkvn = _gdn_conv(proj1, state8, p["conv_w"], n_batch, cfg["conv_tm"])
    lc = cfg["gdn_lc"]
    if per < lc:
        padrows = lambda a: jnp.pad(a.reshape(n_batch, per, a.shape[-1]),
                                    ((0, 0), (0, lc - per), (0, 0))).reshape(n_batch * lc, a.shape[-1])
        og, gdn_new = _gdn(padrows(qkvn), padrows(proj1), padrows(ab), p["a_log"], p["dt_bias"],
                           p["gdn_norm_g"], gdn_state, n_batch, lc, per)
        og = og.reshape(n_batch, lc, C_W)[:, :per].reshape(T, C_W)
    else:
        og, gdn_new = _gdn(qkvn, proj1, ab, p["a_log"], p["dt_bias"], p["gdn_norm_g"], gdn_state,
                           n_batch, lc, lc)
    x, q = _matmul_deepnorm([og], [p["w_out1"]], x, ln_g[1, 0], ln_b[1, 0], tm,
                            w_next=p["w_xq"][1], next_scale=X_HD ** -0.5, name="out1_deepnorm_xq")
    xa = _xattn(q, mem_k, mem_v, 1, n_batch, cfg["x_tq"], cfg["mem_split"])
    x, xb = _matmul_deepnorm([xa], [p["w_xo"][1]], x, ln_g[1, 1], ln_b[1, 1], tm, name="xo1_deepnorm")
    x = _moe(x, xb, p["w_router"], p["w_moe_gu"], p["w_moe_d"], ln_g[1, 2], ln_b[1, 2], cfg["moe_rows"])
    return x, kb, vb, ret_new, conv_new, gdn_new


PROMPT_CFG = dict(tm=512, tm_mm=1024, ret_lc=256, diff_tq=512, x_tq=512, ffn_tm=256, ffn_tn=2816, conv_tm=512,
                  gdn_lc=256, moe_rows=1024, mem_split=False)
STEP_CFG = dict(tm=512, tm_mm=512, ret_lc=16, diff_tq=16, x_tq=16, ffn_tm=256, ffn_tn=2816, conv_tm=16,
                gdn_lc=64, moe_rows=256, mem_split=True)


def kernel(x_prompt, x_sample, cache_diff_k, cache_diff_v, state_ret, state_gdn_conv, state_gdn, cache_mem_k, cache_mem_v, mem_prompt, w_in0, ret_norm_g, diff_lambda, diff_norm_g, w_out0, w_in1, conv_w, a_log, dt_bias, gdn_norm_g, w_out1, w_xq, w_xkv, w_xo, w_ffn_gu, w_ffn_d, w_router, w_moe_gu, w_moe_d, ln_g, ln_b):
    B, S, _ = x_prompt.shape
    DB, L, _ = x_sample.shape
    P = cache_diff_k.shape[1]
    bf = lambda w: w.astype(BF16)
    w_ab = jnp.pad(w_in1[:, 4 * C_W:], ((0, 0), (0, 128 - 2 * H_C)))
    p = dict(w_in0=bf(w_in0), ret_norm_g=ret_norm_g, diff_lambda=diff_lambda, diff_norm_g=diff_norm_g,
             w_out0=bf(w_out0), w_in1_main=bf(w_in1[:, :4 * C_W]), w_in1_ab=bf(w_ab), conv_w=conv_w,
             a_log=a_log, dt_bias=dt_bias, gdn_norm_g=gdn_norm_g, w_out1=bf(w_out1), w_xq=bf(w_xq),
             w_xo=bf(w_xo), w_ffn_gu=bf(w_ffn_gu), w_ffn_d=bf(w_ffn_d), w_router=w_router,
             w_moe_gu=w_moe_gu, w_moe_d=w_moe_d, ln_g=ln_g, ln_b=ln_b)

    mem = mem_prompt.reshape(B * N_MEM, D_MODEL)
    mk_p = _matmul(mem, bf(w_xkv[:, :, :D_MODEL]), F32, 1024, 1024, name="mem_k")
    mv_p = _matmul(mem, bf(w_xkv[:, :, D_MODEL:]), F32, 1024, 1024, name="mem_v")

    y_p, dk_p, dv_p, ret_p, conv_p, gdn_p = _trunk(
        x_prompt.reshape(B * S, D_MODEL), jnp.arange(S), B, mk_p, mv_p,
        jnp.zeros((B, H_A, DK_A, DV_A), F32), None, None,
        jnp.zeros((B, CONV_W - 1, 3 * C_W), F32), jnp.zeros((B, H_C, DK_C, DV_C), F32), p, PROMPT_CFG)

    pos_s = jnp.tile(P + jnp.arange(L), DB)

    def mem_rows(c):
        c = c.reshape(DEPTH, DB * N_MEM, X_HEADS, X_LANE_BLKS, 128).transpose(0, 1, 3, 2, 4)
        return c.reshape(DEPTH, DB * N_MEM * X_LANE_BLKS * X_HEADS, 128)

    y_s, dk_s, dv_s, ret_s, conv_s, gdn_s = _trunk(
        x_sample.reshape(DB * L, D_MODEL), pos_s, DB, mem_rows(cache_mem_k), mem_rows(cache_mem_v),
        state_ret.astype(F32), cache_diff_k.reshape(DB * P * H_B, DV_B), cache_diff_v.reshape(DB * P * H_B, DV_B),
        state_gdn_conv, state_gdn.astype(F32), p, STEP_CFG)

    shape5 = (DEPTH, B, N_MEM, X_HEADS, X_HD)
    return (y_p.reshape(B, S, D_MODEL), y_s.reshape(DB, L, D_MODEL),
            mk_p.reshape(shape5), mv_p.reshape(shape5),
            dk_p.reshape(B, S, H_B, 2 * DK_B), dv_p.reshape(B, S, H_B, DV_B),
            ret_p, conv_p, gdn_p,
            dk_s.reshape(DB, L, H_B, 2 * DK_B), dv_s.reshape(DB, L, H_B, DV_B),
            ret_s, conv_s, gdn_s)
```

```python
import functools
import math

import jax
import jax.numpy as jnp
from jax import lax
from jax.experimental import pallas as pl
from jax.experimental.pallas import tpu as pltpu

F32 = jnp.float32
BF16 = jnp.bfloat16
I32 = jnp.int32

D_MODEL = 1024
DEPTH = 2
CHUNK = 64
H_A, DK_A, DV_A = 4, 64, 128
RET_THETA = 10000.0
H_B, DK_B = 4, 64
DV_B = 2 * DK_B
ROT_B = DK_B // 4
ROPE_THETA = 500000.0
DIFF_LAMBDA_INIT = 0.8 - 0.6 * math.exp(-0.3 * 0)
H_C, DK_C, DV_C = 8, 128, 128
C_W = H_C * DV_C
CONV_W = 4
N_MEM = 256
X_HEADS = 4
X_HD = D_MODEL // X_HEADS
D_FF = 2816
N_EXPERTS = 8
D_FF_E = 3584
DN_ALPHA = (2 * DEPTH) ** 0.25
LN_EPS = 1e-5
A_QK = H_A * DK_A
A_V = H_A * DV_A
B_QK = H_B * 2 * DK_B
B_V = H_B * DV_B
PROJ0 = 2 * A_QK + 2 * A_V + 2 * B_QK + B_V

VMEM_LIMIT_V7X = 52 * 1024 * 1024
NEG_BIG = -1e30

MOE_TOK_TILE = 512
MOE_SUB = 32
MOE_FF_BLK = 512
MOE_SLOTS = 2 * MOE_TOK_TILE // MOE_SUB + N_EXPERTS
GDN_HEAD_GROUP = 4
CONV_ROWS = 32


def _cparams(sem):
    return pltpu.CompilerParams(dimension_semantics=sem, vmem_limit_bytes=VMEM_LIMIT_V7X)


def _dot(a, b):
    return jnp.dot(a, b, preferred_element_type=F32)


def _dot_nt(a, b):
    return lax.dot_general(a, b, (((1,), (1,)), ((), ())), preferred_element_type=F32)


def _dot_tn(a, b):
    return lax.dot_general(a, b, (((0,), (0,)), ((), ())), preferred_element_type=F32)


def _silu(x):
    return x * (1.0 / (1.0 + jnp.exp(-x)))


def _layer_norm_rows(y, g, b):
    mu = jnp.mean(y, axis=-1, keepdims=True)
    d = y - mu
    var = jnp.mean(d * d, axis=-1, keepdims=True)
    return d * lax.rsqrt(var + LN_EPS) * g + b


def _mm_kernel(x_ref, w_ref, o_ref, *, scale):
    acc = _dot(x_ref[...].astype(BF16), w_ref[0])
    if scale != 1.0:
        acc = acc * scale
    o_ref[0] = acc.astype(o_ref.dtype)


def _matmul(x, w, out_dtype, tm, tn, scale=1.0, name="matmul"):
    M, K = x.shape
    G, _, N = w.shape
    tm = min(tm, M)
    tn = min(tn, N)
    assert M % tm == 0 and N % tn == 0
    return pl.pallas_call(
        functools.partial(_mm_kernel, scale=scale),
        out_shape=jax.ShapeDtypeStruct((G, M, N), out_dtype),
        grid=(G, M // tm, N // tn),
        in_specs=[pl.BlockSpec((tm, K), lambda g, i, j: (i, 0)),
                  pl.BlockSpec((1, K, tn), lambda g, i, j: (g, 0, j))],
        out_specs=pl.BlockSpec((1, tm, tn), lambda g, i, j: (g, i, j)),
        compiler_params=_cparams(("parallel", "parallel", "arbitrary")),
        name=name,
    )(x, w)


def _mm_dn_kernel(*refs, n_in, next_scale):
    xs = refs[:n_in]
    ws = refs[n_in:2 * n_in]
    rest = refs[2 * n_in:]
    if next_scale is None:
        r_ref, g_ref, b_ref, o_ref, ob_ref = rest
    else:
        r_ref, g_ref, b_ref, wn_ref, o_ref, ob_ref = rest
    acc = DN_ALPHA * r_ref[...]
    for x_ref, w_ref in zip(xs, ws):
        acc = acc + _dot(x_ref[...].astype(BF16), w_ref[...])
    y = _layer_norm_rows(acc, g_ref[...], b_ref[...])
    o_ref[...] = y
    if next_scale is None:
        ob_ref[...] = y.astype(BF16)
    else:
        ob_ref[...] = (_dot(y.astype(BF16), wn_ref[...]) * next_scale).astype(BF16)


def _matmul_deepnorm(xs, ws, resid, g, b, tm, w_next=None, next_scale=None, name="matmul_deepnorm"):
    M = resid.shape[0]
    tm = min(tm, M)
    assert M % tm == 0
    n_in = len(xs)
    in_specs = [pl.BlockSpec((tm, x.shape[1]), lambda i: (i, 0)) for x in xs]
    in_specs += [pl.BlockSpec(w.shape, lambda i: (0, 0)) for w in ws]
    in_specs += [pl.BlockSpec((tm, D_MODEL), lambda i: (i, 0)),
                 pl.BlockSpec((1, D_MODEL), lambda i: (0, 0)),
                 pl.BlockSpec((1, D_MODEL), lambda i: (0, 0))]
    args = [*xs, *ws, resid, g.reshape(1, D_MODEL), b.reshape(1, D_MODEL)]
    if w_next is not None:
        in_specs.append(pl.BlockSpec(w_next.shape, lambda i: (0, 0)))
        args.append(w_next)
    return pl.pallas_call(
        functools.partial(_mm_dn_kernel, n_in=n_in, next_scale=next_scale),
        out_shape=[jax.ShapeDtypeStruct((M, D_MODEL), F32), jax.ShapeDtypeStruct((M, D_MODEL), BF16)],
        grid=(M // tm,),
        in_specs=in_specs,
        out_specs=[pl.BlockSpec((tm, D_MODEL), lambda i: (i, 0))] * 2,
        compiler_params=_cparams(("parallel",)),
        name=name,
    )(*args)


def _swap_halves(x, group, half):
    n = x.shape[-1]
    lane = lax.broadcasted_iota(I32, x.shape, x.ndim - 1) % group
    up = pltpu.roll(x, n - half, x.ndim - 1)
    dn = pltpu.roll(x, half, x.ndim - 1)
    return jnp.where(lane < half, up, dn)


def _store_rows(o_ref, y, by_head):
    if not by_head:
        o_ref[...] = y.astype(o_ref.dtype)
        return
    tm = y.shape[0]
    for h in range(H_B):
        o_ref[pl.ds(h, tm, stride=H_B), :] = y[:, h * DV_B:(h + 1) * DV_B].astype(o_ref.dtype)


def _proj_plain_kernel(x_ref, w_ref, o_ref, *, by_head):
    _store_rows(o_ref, _dot(x_ref[...], w_ref[...]), by_head)


def _proj_rot_kernel(x_ref, w_ref, c_ref, s_ref, o_ref, *, group, half, by_head):
    acc = _dot(x_ref[...], w_ref[...])
    _store_rows(o_ref, acc * c_ref[...] + _swap_halves(acc, group, half) * s_ref[...], by_head)


def _proj0_block(xb, w, out_dtype, tm, rot=None, by_head=False, name="proj0"):
    T = xb.shape[0]
    blk = w.shape[1]
    tm = min(tm, T)
    row = lambda i: (i, 0)
    in_specs = [pl.BlockSpec((tm, D_MODEL), row), pl.BlockSpec((D_MODEL, blk), lambda i: (0, 0))]
    args = [xb, w]
    if rot is None:
        body = functools.partial(_proj_plain_kernel, by_head=by_head)
    else:
        cos, sin, group, half = rot
        P = cos.shape[0]
        tm = min(tm, P)
        assert P % tm == 0
        npb = P // tm
        in_specs[0] = pl.BlockSpec((tm, D_MODEL), row)
        in_specs += [pl.BlockSpec((tm, blk), lambda i: (i % npb, 0))] * 2
        args += [cos, sin]
        body = functools.partial(_proj_rot_kernel, group=group, half=half, by_head=by_head)
    assert T % tm == 0
    if by_head:
        assert blk == H_B * DV_B
        out_shape = jax.ShapeDtypeStruct((T * H_B, DV_B), out_dtype)
        out_spec = pl.BlockSpec((tm * H_B, DV_B), row)
    else:
        out_shape = jax.ShapeDtypeStruct((T, blk), out_dtype)
        out_spec = pl.BlockSpec((tm, blk), row)
    return pl.pallas_call(
        body,
        out_shape=out_shape,
        grid=(T // tm,),
        in_specs=in_specs,
        out_specs=out_spec,
        compiler_params=_cparams(("parallel",)),
        name=name,
    )(*args)


def _rope_tables(pos):
    pos = pos.astype(F32)
    inv = RET_THETA ** (-jnp.arange(0, DK_A, 2, dtype=F32) / DK_A)
    ang = pos[:, None] * inv[None, :]
    c, s = jnp.cos(ang), jnp.sin(ang)
    c64 = jnp.concatenate([c, c], axis=-1)
    s64 = jnp.concatenate([-s, s], axis=-1)
    kscale = DK_A ** -0.5
    ca = jnp.concatenate([jnp.tile(c64, (1, H_A)), jnp.tile(c64, (1, H_A)) * kscale], axis=-1)
    sa = jnp.concatenate([jnp.tile(s64, (1, H_A)), jnp.tile(s64, (1, H_A)) * kscale], axis=-1)
    invb = ROPE_THETA ** (-jnp.arange(0, ROT_B, 2, dtype=F32) / ROT_B)
    angb = pos[:, None] * invb[None, :]
    cb, sb = jnp.cos(angb), jnp.sin(angb)
    rest = DK_B - ROT_B
    c64b = jnp.concatenate([cb, cb, jnp.ones((pos.shape[0], rest), F32)], axis=-1)
    s64b = jnp.concatenate([-sb, sb, jnp.zeros((pos.shape[0], rest), F32)], axis=-1)
    ck = jnp.tile(c64b, (1, 2 * H_B))
    sk = jnp.tile(s64b, (1, 2 * H_B))
    qscale = DK_B ** -0.5
    return ca, sa, ck * qscale, sk * qscale, ck, sk


def _proj0(xb, w_in0_bf, tables, tm):
    ca, sa, cq, sq, ck, sk = tables
    blk = 512
    assert PROJ0 == 6 * blk
    w = [w_in0_bf[:, j * blk:(j + 1) * blk] for j in range(6)]
    rot_a = (DK_A, DK_A // 2)
    rot_b = (DK_B, ROT_B // 2)
    qka = _proj0_block(xb, w[0], BF16, tm, (ca, sa) + rot_a, name="proj0_qk_ret")
    va = _proj0_block(xb, w[1], BF16, tm, name="proj0_v_ret")
    ga = _proj0_block(xb, w[2], F32, tm, name="proj0_gate_ret")
    qb = _proj0_block(xb, w[3], BF16, tm, (cq, sq) + rot_b, name="proj0_q_diff")
    kb = _proj0_block(xb, w[4], F32, tm, (ck, sk) + rot_b, by_head=True, name="proj0_k_diff")
    vb = _proj0_block(xb, w[5], F32, tm, by_head=True, name="proj0_v_diff")
    return qka, va, ga, qb, kb, vb


def _retention_kernel(qk_ref, v_ref, g_ref, dmat_ref, qdec_ref, kdec_ref, sdec_ref, s0_ref, ng_ref,
                      o_ref, sout_ref, s_scr):
    c = pl.program_id(1)

    @pl.when(c == 0)
    def _():
        s_scr[...] = s0_ref[0]

    for h in range(H_A):
        q = qk_ref[:, h * DK_A:(h + 1) * DK_A]
        k = qk_ref[:, A_QK + h * DK_A:A_QK + (h + 1) * DK_A]
        v = v_ref[:, h * DV_A:(h + 1) * DV_A]
        s = s_scr[h]
        scores = _dot_nt(q, k) * dmat_ref[h]
        o = _dot(scores.astype(BF16), v) + _dot(q, s.astype(BF16)) * qdec_ref[h]
        kd = (k.astype(F32) * kdec_ref[h]).astype(BF16)
        s_scr[h] = s * sdec_ref[h] + _dot_tn(kd, v)
        mu = jnp.mean(o, axis=-1, keepdims=True)
        d = o - mu
        var = jnp.mean(d * d, axis=-1, keepdims=True)
        gate = g_ref[:, h * DV_A:(h + 1) * DV_A]
        y = d * lax.rsqrt(var + LN_EPS) * ng_ref[:, h * DV_A:(h + 1) * DV_A] * _silu(gate)
        o_ref[:, h * DV_A:(h + 1) * DV_A] = y.astype(o_ref.dtype)

    @pl.when(c == pl.num_programs(1) - 1)
    def _():
        sout_ref[0] = s_scr[...]


def _retention_tables(lc):
    lg = jnp.log1p(-jnp.exp2(-5.0 - jnp.arange(H_A, dtype=F32)))
    t = jnp.arange(lc, dtype=F32)
    rel = t[:, None] - t[None, :]
    causal = rel >= 0
    dmat = jnp.where(causal, jnp.exp(lg[:, None, None] * jnp.where(causal, rel, 0.0)), 0.0)
    qdec = jnp.exp(lg[:, None] * (t[None, :] + 1.0))
    kdec = jnp.exp(lg[:, None] * (lc - 1.0 - t[None, :]))
    sdec = jnp.exp(lg * lc)
    return (dmat,
            jnp.broadcast_to(qdec[:, :, None], (H_A, lc, DV_A)),
            jnp.broadcast_to(kdec[:, :, None], (H_A, lc, DK_A)),
            jnp.broadcast_to(sdec[:, None, None], (H_A, DK_A, DV_A)))


def _retention(qka, va, ga, state0, ret_norm_g, n_batch, lc):
    T = qka.shape[0]
    nc = T // (n_batch * lc)
    assert nc * n_batch * lc == T
    dmat, qdec, kdec, sdec = _retention_tables(lc)
    row = lambda b, c: (b * nc + c, 0)
    const3 = lambda b, c: (0, 0, 0)
    return pl.pallas_call(
        _retention_kernel,
        out_shape=[jax.ShapeDtypeStruct((T, A_V), BF16),
                   jax.ShapeDtypeStruct((n_batch, H_A, DK_A, DV_A), F32)],
        grid=(n_batch, nc),
        in_specs=[pl.BlockSpec((lc, 2 * A_QK), row),
                  pl.BlockSpec((lc, A_V), row),
                  pl.BlockSpec((lc, A_V), row),
                  pl.BlockSpec((H_A, lc, lc), const3),
                  pl.BlockSpec((H_A, lc, DV_A), const3),
                  pl.BlockSpec((H_A, lc, DK_A), const3),
                  pl.BlockSpec((H_A, DK_A, DV_A), const3),
                  pl.BlockSpec((1, H_A, DK_A, DV_A), lambda b, c: (b, 0, 0, 0)),
                  pl.BlockSpec((1, A_V), lambda b, c: (0, 0))],
        out_specs=[pl.BlockSpec((lc, A_V), row),
                   pl.BlockSpec((1, H_A, DK_A, DV_A), lambda b, c: (b, 0, 0, 0))],
        scratch_shapes=[pltpu.VMEM((H_A, DK_A, DV_A), F32)],
        compiler_params=_cparams(("parallel", "arbitrary")),
        name="retention",
    )(qka, va, ga, dmat, qdec, kdec, sdec, state0, ret_norm_g.reshape(1, A_V))


def _diff_finish(acc1, l1, acc2, l2, lam, ng):
    o = acc1 * (1.0 / l1) - lam * (acc2 * (1.0 / l2))
    ms = jnp.mean(o * o, axis=-1, keepdims=True)
    return o * lax.rsqrt(ms + LN_EPS) * ng * (1.0 - DIFF_LAMBDA_INIT)


def _diff_prompt_kernel(lam_ref, q_ref, k_ref, v_ref, ng_ref, o_ref, kbf, vbf, *, tq):
    qi = pl.program_id(2)

    @pl.when(qi == 0)
    def _():
        h = pl.program_id(1)
        seq = kbf.shape[0]
        kbf[...] = k_ref[pl.ds(h, seq, stride=H_B), :].astype(BF16)
        vbf[...] = v_ref[pl.ds(h, seq, stride=H_B), :].astype(BF16)

    rc = lax.broadcasted_iota(I32, (tq, tq), 0) // CHUNK
    cc = lax.broadcasted_iota(I32, (tq, tq), 1) // CHUNK
    vis = cc <= rc
    nq = kbf.shape[0] // tq

    for j in range(nq):
        @pl.when(qi == j)
        def _(j=j):
            n_full = j * tq
            kd = kbf[n_full:n_full + tq, :]
            vd = vbf[n_full:n_full + tq, :]
            accs = []
            for c in range(2):
                sl = slice(c * DK_B, (c + 1) * DK_B)
                qc = q_ref[:, sl]
                sd = jnp.where(vis, _dot_nt(qc, kd[:, sl]), NEG_BIG)
                m = jnp.max(sd, axis=-1, keepdims=True)
                if j > 0:
                    sf = _dot_nt(qc, kbf[0:n_full, sl])
                    m = jnp.maximum(m, jnp.max(sf, axis=-1, keepdims=True))
                pd = jnp.exp(sd - m)
                l = jnp.sum(pd, axis=-1, keepdims=True)
                acc = _dot(pd.astype(BF16), vd)
                if j > 0:
                    pf = jnp.exp(sf - m)
                    l = l + jnp.sum(pf, axis=-1, keepdims=True)
                    acc = acc + _dot(pf.astype(BF16), vbf[0:n_full, :])
                accs += [acc, l]
            o_ref[...] = _diff_finish(accs[0], accs[1], accs[2], accs[3], lam_ref[0], ng_ref[...]).astype(o_ref.dtype)


def _diff_prompt(qb, kb, vb, lam, diff_norm_g, n_batch, seq, tq):
    T = qb.shape[0]
    assert seq % tq == 0 and tq % CHUNK == 0
    nq = seq // tq
    return pl.pallas_call(
        functools.partial(_diff_prompt_kernel, tq=tq),
        out_shape=jax.ShapeDtypeStruct((T, B_V), BF16),
        grid_spec=pltpu.PrefetchScalarGridSpec(
            num_scalar_prefetch=1,
            grid=(n_batch, H_B, nq),
            in_specs=[pl.BlockSpec((tq, DV_B), lambda b, h, i, lam: (b * nq + i, h)),
                      pl.BlockSpec((seq * H_B, DV_B), lambda b, h, i, lam: (b, 0)),
                      pl.BlockSpec((seq * H_B, DV_B), lambda b, h, i, lam: (b, 0)),
                      pl.BlockSpec((1, DV_B), lambda b, h, i, lam: (0, 0))],
            out_specs=pl.BlockSpec((tq, DV_B), lambda b, h, i, lam: (b * nq + i, h)),
            scratch_shapes=[pltpu.VMEM((seq, DV_B), BF16), pltpu.VMEM((seq, DV_B), BF16)]),
        compiler_params=_cparams(("parallel", "parallel", "arbitrary")),
        name="diff_attn_prompt",
    )(lam, qb, kb, vb, diff_norm_g.reshape(1, DV_B))


def _diff_step_kernel(lam_ref, q_ref, kp_ref, vp_ref, kn_ref, vn_ref, ng_ref, o_ref, *, past, ln):
    qchunk = (past + lax.broadcasted_iota(I32, (ln, 1), 0)) // CHUNK
    vis_p = (lax.broadcasted_iota(I32, (ln, past), 1) // CHUNK) <= qchunk
    vis_n = ((past + lax.broadcasted_iota(I32, (ln, ln), 1)) // CHUNK) <= qchunk
    for h in range(H_B):
        q = q_ref[:, h * DV_B:(h + 1) * DV_B]
        kp = kp_ref[pl.ds(h, past, stride=H_B), :].astype(BF16)
        vp = vp_ref[pl.ds(h, past, stride=H_B), :].astype(BF16)
        kn = kn_ref[pl.ds(h, ln, stride=H_B), :].astype(BF16)
        vn = vn_ref[pl.ds(h, ln, stride=H_B), :].astype(BF16)
        accs = []
        for c in range(2):
            sl = slice(c * DK_B, (c + 1) * DK_B)
            sp = jnp.where(vis_p, _dot_nt(q[:, sl], kp[:, sl]), NEG_BIG)
            sn = jnp.where(vis_n, _dot_nt(q[:, sl], kn[:, sl]), NEG_BIG)
            m = jnp.maximum(jnp.max(sp, axis=-1, keepdims=True), jnp.max(sn, axis=-1, keepdims=True))
            pp = jnp.exp(sp - m)
            pn = jnp.exp(sn - m)
            l = jnp.sum(pp, axis=-1, keepdims=True) + jnp.sum(pn, axis=-1, keepdims=True)
            accs += [_dot(pp.astype(BF16), vp) + _dot(pn.astype(BF16), vn), l]
        y = _diff_finish(accs[0], accs[1], accs[2], accs[3], lam_ref[0], ng_ref[...])
        o_ref[:, h * DV_B:(h + 1) * DV_B] = y.astype(o_ref.dtype)


def _diff_step(qb, kb, vb, past_k, past_v, lam, diff_norm_g, n_batch, ln):
    past = past_k.shape[0] // (n_batch * H_B)
    blk = lambda b, lam: (b, 0)
    return pl.pallas_call(
        functools.partial(_diff_step_kernel, past=past, ln=ln),
        out_shape=jax.ShapeDtypeStruct((n_batch * ln, B_V), BF16),
        grid_spec=pltpu.PrefetchScalarGridSpec(
            num_scalar_prefetch=1,
            grid=(n_batch,),
            in_specs=[pl.BlockSpec((ln, B_QK), blk),
                      pl.BlockSpec((past * H_B, DV_B), blk),
                      pl.BlockSpec((past * H_B, DV_B), blk),
                      pl.BlockSpec((ln * H_B, DV_B), blk),
                      pl.BlockSpec((ln * H_B, DV_B), blk),
                      pl.BlockSpec((1, DV_B), lambda b, lam: (0, 0))],
            out_specs=pl.BlockSpec((ln, B_V), blk)),
        compiler_params=_cparams(("parallel",)),
        name="diff_attn_step",
    )(lam, qb, past_k, past_v, kb, vb, diff_norm_g.reshape(1, DV_B))


X_LANE_BLKS = X_HD // 128


def _xattn_kernel(q_ref, mk_ref, mv_ref, o_ref, *, split):
    stride = X_LANE_BLKS * X_HEADS
    for h in range(X_HEADS):
        if split:
            pieces = [(slice(h * X_HD + j * 128, h * X_HD + (j + 1) * 128), pl.ds(j * X_HEADS + h, N_MEM, stride=stride))
                      for j in range(X_LANE_BLKS)]
            s = sum(_dot_nt(q_ref[:, cs], mk_ref[rs, :].astype(BF16)) for cs, rs in pieces)
        else:
            sl = slice(h * X_HD, (h + 1) * X_HD)
            s = _dot_nt(q_ref[:, sl], mk_ref[:, sl].astype(BF16))
        p = jnp.exp(s - jnp.max(s, axis=-1, keepdims=True))
        inv_l = 1.0 / jnp.sum(p, axis=-1, keepdims=True)
        pb = p.astype(BF16)
        if split:
            for cs, rs in pieces:
                o_ref[:, cs] = (_dot(pb, mv_ref[rs, :].astype(BF16)) * inv_l).astype(o_ref.dtype)
        else:
            o_ref[:, sl] = (_dot(pb, mv_ref[:, sl].astype(BF16)) * inv_l).astype(o_ref.dtype)


def _xattn(q, mk, mv, layer, n_batch, tq, split):
    T = q.shape[0]
    per = T // n_batch
    tq = min(tq, per)
    nt = per // tq
    mem_blk = (None, N_MEM * X_LANE_BLKS * X_HEADS, 128) if split else (None, N_MEM, D_MODEL)
    return pl.pallas_call(
        functools.partial(_xattn_kernel, split=split),
        out_shape=jax.ShapeDtypeStruct((T, D_MODEL), BF16),
        grid=(n_batch, nt),
        in_specs=[pl.BlockSpec((tq, D_MODEL), lambda b, t: (b * nt + t, 0)),
                  pl.BlockSpec(mem_blk, lambda b, t: (layer, b, 0)),
                  pl.BlockSpec(mem_blk, lambda b, t: (layer, b, 0))],
        out_specs=pl.BlockSpec((tq, D_MODEL), lambda b, t: (b * nt + t, 0)),
        compiler_params=_cparams(("parallel", "arbitrary")),
        name="mem_xattn",
    )(q, mk, mv)


def _swiglu_up_kernel(x_ref, wg_ref, wu_ref, o_ref):
    xb = x_ref[...]
    g = _dot(xb, wg_ref[...])
    u = _dot(xb, wu_ref[...])
    o_ref[...] = (_silu(g) * u).astype(o_ref.dtype)


def _swiglu_up(xb, w_gu_bf, tm, tn):
    T = xb.shape[0]
    tm = min(tm, T)
    nj = D_FF // tn
    assert D_FF % tn == 0
    return pl.pallas_call(
        _swiglu_up_kernel,
        out_shape=jax.ShapeDtypeStruct((T, D_FF), BF16),
        grid=(T // tm, nj),
        in_specs=[pl.BlockSpec((tm, D_MODEL), lambda i, j: (i, 0)),
                  pl.BlockSpec((D_MODEL, tn), lambda i, j: (0, j)),
                  pl.BlockSpec((D_MODEL, tn), lambda i, j: (0, nj + j))],
        out_specs=pl.BlockSpec((tm, tn), lambda i, j: (i, j)),
        compiler_params=_cparams(("parallel", "arbitrary")),
        name="swiglu_up",
    )(xb, w_gu_bf, w_gu_bf)


def _conv_kernel(x_ref, prev_ref, st_ref, w_ref, o_ref, buf, *, tm):
    t = pl.program_id(1)
    cb = pl.program_id(2)
    buf[0:8, :] = jnp.where(t == 0, st_ref[0], prev_ref[...])
    buf[8:8 + tm, :] = x_ref[...]
    scale = jnp.where(cb == 0, DK_C ** -0.5, 1.0)
    rc = min(CONV_ROWS, tm)
    for h in range(H_C):
        cs = slice(h * DK_C, (h + 1) * DK_C)
        w = [w_ref[i:i + 1, cs] for i in range(CONV_W)]
        for r0 in range(0, tm, rc):
            y = buf[8 + r0:8 + r0 + rc, cs] * w[CONV_W - 1]
            for i in range(CONV_W - 1):
                y = y + buf[5 + i + r0:5 + i + r0 + rc, cs] * w[i]
            y = _silu(y)
            ss = jnp.sum(y * y, axis=-1, keepdims=True)
            f = jnp.where(cb == 2, 1.0, lax.rsqrt(ss + 1e-6) * scale)
            o_ref[r0:r0 + rc, cs] = (y * f).astype(o_ref.dtype)


def _gdn_conv(proj1, conv_state8, conv_w, n_batch, tm):
    T = proj1.shape[0]
    per = T // n_batch
    tm = min(tm, per)
    nt = per // tm
    assert per % tm == 0 and tm % 8 == 0
    return pl.pallas_call(
        functools.partial(_conv_kernel, tm=tm),
        out_shape=jax.ShapeDtypeStruct((T, 3 * C_W), BF16),
        grid=(n_batch, nt, 3),
        in_specs=[pl.BlockSpec((tm, C_W), lambda b, t, c: (b * nt + t, c)),
                  pl.BlockSpec((8, C_W), lambda b, t, c: (jnp.maximum((b * nt + t) * (tm // 8) - 1, 0), c)),
                  pl.BlockSpec((1, 8, C_W), lambda b, t, c: (b, 0, c)),
                  pl.BlockSpec((CONV_W, C_W), lambda b, t, c: (0, c))],
        out_specs=pl.BlockSpec((tm, C_W), lambda b, t, c: (b * nt + t, c)),
        scratch_shapes=[pltpu.VMEM((8 + tm, C_W), F32)],
        compiler_params=_cparams(("parallel", "arbitrary", "arbitrary")),
        name="gdn_conv",
    )(proj1, proj1, conv_state8, conv_w)


def _gdn_kernel(q_ref, k_ref, v_ref, z_ref, ab_ref, alog_ref, dtb_ref, ng_ref, s0_ref,
                o_ref, sout_ref, s_scr, *, lc, l_real):
    c = pl.program_id(1)

    @pl.when(c == 0)
    def _():
        s_scr[...] = s0_ref[0]

    ri = lax.broadcasted_iota(I32, (lc, lc), 0)
    ci = lax.broadcasted_iota(I32, (lc, lc), 1)
    incl = ci <= ri
    strict = ci < ri
    eye = (ci == ri).astype(F32)
    ab = ab_ref[...]
    sp = jnp.maximum(ab + dtb_ref[...], 0.0) + jnp.log1p(jnp.exp(-jnp.abs(ab + dtb_ref[...])))
    glog = -jnp.exp(alog_ref[...]) * sp
    if l_real < lc:
        live = lax.broadcasted_iota(I32, (lc, 1), 0) < l_real
        glog = jnp.where(live, glog, 0.0)
    gcum = jnp.dot(incl.astype(F32), glog, preferred_element_type=F32, precision=lax.Precision.HIGHEST)
    gcum_t = gcum.T
    beta_all = 1.0 / (1.0 + jnp.exp(-ab))

    levels = []
    s = 1
    while s < lc:
        levels.append(((ri // (2 * s)) == (ci // (2 * s))) & (((ri // s) % 2) == 1) & (((ci // s) % 2) == 0))
        s *= 2

    group = GDN_HEAD_GROUP if lc > 128 else H_C
    for h0 in range(0, H_C, group):
        heads = range(h0, h0 + group)
        sls = [slice(h * DK_C, (h + 1) * DK_C) for h in heads]
        ks, kfs, vs, gcols, decays, egs, kbs, a_lows = [], [], [], [], [], [], [], []
        for h, sl in zip(heads, sls):
            k = k_ref[:, sl]
            v = v_ref[:, sl].astype(F32)
            if l_real < lc:
                k = jnp.where(live, k, jnp.zeros_like(k))
                v = jnp.where(live, v, 0.0)
            kf = k.astype(F32)
            gcol = gcum[:, h:h + 1]
            grow = gcum_t[h:h + 1, :]
            beta = beta_all[:, H_C + h:H_C + h + 1]
            decay = jnp.exp(jnp.where(incl, gcol - grow, NEG_BIG))
            kb = kf * beta
            ks.append(k)
            kfs.append(kf)
            gcols.append(gcol)
            decays.append(decay)
            egs.append(jnp.exp(gcol))
            kbs.append(kb)
            vs.append(v * beta)
            a_lows.append(jnp.where(strict, _dot_nt(kb.astype(BF16), k) * decay, 0.0))
        tinvs = [eye - jnp.where(levels[0], a, 0.0) for a in a_lows]
        for lvl, off in enumerate(levels[1:], start=1):
            s = 2 ** lvl
            tbs = [t.astype(BF16) for t in tinvs]
            if s < 8:
                a_offs = [jnp.where(off, a, 0.0).astype(BF16) for a in a_lows]
                ws = [_dot(a, t).astype(BF16) for a, t in zip(a_offs, tbs)]
                tinvs = [t - _dot(tb, w) for t, tb, w in zip(tinvs, tbs, ws)]
            else:
                odd = [slice((2 * j + 1) * s, (2 * j + 2) * s) for j in range(lc // (2 * s))]
                even = [slice(2 * j * s, (2 * j + 1) * s) for j in range(lc // (2 * s))]
                take = lambda x: jnp.concatenate([x[sl, :] for sl in odd], axis=0)
                rh = lax.broadcasted_iota(I32, (lc // 2, lc), 0)
                ch = lax.broadcasted_iota(I32, (lc // 2, lc), 1)
                off_odd = (ch // s) == 2 * (rh // s)
                zero = jnp.zeros((s, lc), F32)
                new = []
                for a, t, tb in zip(a_lows, tinvs, tbs):
                    w_odd = _dot(jnp.where(off_odd, take(a), 0.0).astype(BF16), tb)
                    w_full = jnp.concatenate(
                        [piece for j in range(len(odd)) for piece in (zero, w_odd[j * s:(j + 1) * s, :])], axis=0)
                    t_odd = take(t)
                    t_odd = t_odd - _dot(t_odd.astype(BF16), w_full.astype(BF16))
                    new.append(jnp.concatenate(
                        [piece for j, ev in enumerate(even) for piece in (t[ev, :], t_odd[j * s:(j + 1) * s, :])],
                        axis=0))
                tinvs = new
        rhss = [jnp.concatenate([vb, kb * eg], axis=-1).astype(BF16) for vb, kb, eg in zip(vs, kbs, egs)]
        sols = [_dot(t.astype(BF16), r) for t, r in zip(tinvs, rhss)]
        sts = [s_scr[h] for h in heads]
        stbs = [st.astype(BF16) for st in sts]
        ubs = [(sol[:, :DV_C] - _dot(sol[:, DV_C:].astype(BF16), stb)).astype(BF16) for sol, stb in zip(sols, stbs)]
        qs = [q_ref[:, sl] for sl in sls]
        qks = [(_dot_nt(q, k) * decay).astype(BF16) for q, k, decay in zip(qs, ks, decays)]
        os_ = [_dot((q.astype(F32) * eg).astype(BF16), stb) + _dot(qk, ub)
               for q, eg, stb, qk, ub in zip(qs, egs, stbs, qks, ubs)]
        for h, st, kf, gcol, ub in zip(heads, sts, kfs, gcols, ubs):
            g_last = gcol[lc - 1:lc, :]
            kdec = (kf * jnp.exp(g_last - gcol)).astype(BF16)
            s_scr[h] = st * jnp.exp(g_last) + _dot_tn(kdec, ub)
        for sl, o in zip(sls, os_):
            ms = jnp.mean(o * o, axis=-1, keepdims=True)
            y = o * lax.rsqrt(ms + LN_EPS) * ng_ref[...] * _silu(z_ref[:, sl])
            o_ref[:, sl] = y.astype(o_ref.dtype)

    @pl.when(c == pl.num_programs(1) - 1)
    def _():
        sout_ref[0] = s_scr[...]


def _gdn(qkvn, proj1, ab, a_log, dt_bias, gdn_norm_g, state0, n_batch, lc, l_real):
    T = qkvn.shape[0]
    nc = T // (n_batch * lc)
    assert nc * n_batch * lc == T and (nc == 1 or l_real == lc)
    pad = lambda v: jnp.pad(v.astype(F32).reshape(1, H_C), ((0, 0), (0, 128 - H_C)))
    return pl.pallas_call(
        functools.partial(_gdn_kernel, lc=lc, l_real=l_real),
        out_shape=[jax.ShapeDtypeStruct((T, C_W), BF16),
                   jax.ShapeDtypeStruct((n_batch, H_C, DK_C, DV_C), F32)],
        grid=(n_batch, nc),
        in_specs=[pl.BlockSpec((lc, C_W), lambda b, c: (b * nc + c, 0)),
                  pl.BlockSpec((lc, C_W), lambda b, c: (b * nc + c, 1)),
                  pl.BlockSpec((lc, C_W), lambda b, c: (b * nc + c, 2)),
                  pl.BlockSpec((lc, C_W), lambda b, c: (b * nc + c, 3)),
                  pl.BlockSpec((lc, 128), lambda b, c: (b * nc + c, 0)),
                  pl.BlockSpec((1, 128), lambda b, c: (0, 0)),
                  pl.BlockSpec((1, 128), lambda b, c: (0, 0)),
                  pl.BlockSpec((1, DV_C), lambda b, c: (0, 0)),
                  pl.BlockSpec((1, H_C, DK_C, DV_C), lambda b, c: (b, 0, 0, 0))],
        out_specs=[pl.BlockSpec((lc, C_W), lambda b, c: (b * nc + c, 0)),
                   pl.BlockSpec((1, H_C, DK_C, DV_C), lambda b, c: (b, 0, 0, 0))],
        scratch_shapes=[pltpu.VMEM((H_C, DK_C, DV_C), F32)],
        compiler_params=_cparams(("parallel", "arbitrary")),
        name="gated_delta",
    )(qkvn, qkvn, qkvn, proj1, ab, pad(a_log), pad(dt_bias), gdn_norm_g.reshape(1, DV_C), state0)


def _router_kernel(x_ref, wt_ref, rt_ref, rtt_ref, cnt_ref):
    tt = x_ref.shape[0]
    logits = lax.dot_general(wt_ref[...], x_ref[...], (((1,), (1,)), ((), ())),
                             preferred_element_type=F32, precision=lax.Precision.HIGHEST)
    eid = lax.broadcasted_iota(I32, (N_EXPERTS, tt), 0).astype(F32)
    m1 = jnp.max(logits, axis=0, keepdims=True)
    e1 = jnp.min(jnp.where(logits == m1, eid, float(N_EXPERTS)), axis=0, keepdims=True)
    rest = jnp.where(eid == e1, -jnp.inf, logits)
    m2 = jnp.max(rest, axis=0, keepdims=True)
    e2 = jnp.min(jnp.where(rest == m2, eid, float(N_EXPERTS)), axis=0, keepdims=True)
    ev = jnp.exp(m2 - m1)
    g1 = 1.0 / (1.0 + ev)
    g2 = ev / (1.0 + ev)
    hit1 = eid == e1
    hit2 = eid == e2
    member = jnp.where(hit1 | hit2, 1.0, 0.0)
    before = (lax.broadcasted_iota(I32, (tt, tt), 0) < lax.broadcasted_iota(I32, (tt, tt), 1))
    rank = _dot(member.astype(BF16), jnp.where(before, 1.0, 0.0).astype(BF16))
    r1 = jnp.sum(jnp.where(hit1, rank, 0.0), axis=0, keepdims=True)
    r2 = jnp.sum(jnp.where(hit2, rank, 0.0), axis=0, keepdims=True)
    rows = jnp.concatenate([e1, e2, r1, r2, g1, g2, jnp.zeros((2, tt), F32)], axis=0)
    rt_ref[0] = rows
    rtt_ref[0] = jnp.concatenate([rows, jnp.zeros((128 - 8, tt), F32)], axis=0).T
    cnt = jnp.sum(member, axis=1, keepdims=True).astype(I32)
    cnt_ref[0] = jnp.broadcast_to(cnt, (N_EXPERTS, 128))


def _router(x, w_router):
    T = x.shape[0]
    tt = MOE_TOK_TILE
    nt = T // tt
    assert T % tt == 0
    return pl.pallas_call(
        _router_kernel,
        out_shape=[jax.ShapeDtypeStruct((nt, 8, tt), F32),
                   jax.ShapeDtypeStruct((nt, tt, 128), F32),
                   jax.ShapeDtypeStruct((nt, N_EXPERTS, 128), I32)],
        grid=(nt,),
        in_specs=[pl.BlockSpec((tt, D_MODEL), lambda i: (i, 0)),
                  pl.BlockSpec((N_EXPERTS, D_MODEL), lambda i: (0, 0))],
        out_specs=[pl.BlockSpec((1, 8, tt), lambda i: (i, 0, 0)),
                   pl.BlockSpec((1, tt, 128), lambda i: (i, 0, 0)),
                   pl.BlockSpec((1, N_EXPERTS, 128), lambda i: (i, 0, 0))],
        compiler_params=_cparams(("parallel",)),
        name="moe_router",
    )(x, w_router.T)


def _moe_tables(cnt, n_tiles, group):
    E = N_EXPERTS
    nb_max = (2 * n_tiles * MOE_TOK_TILE) // MOE_SUB + n_tiles * E
    nb_tot = -(-(nb_max + E * (group - 1)) // group) * group
    nblk = (cnt + MOE_SUB - 1) // MOE_SUB
    nbe = jnp.sum(nblk, axis=0)
    nbe_pad = (nbe + group - 1) // group * group
    ends_e = jnp.cumsum(nbe_pad)
    base = ends_e - nbe_pad
    dest0 = base[None, :] + jnp.cumsum(nblk, axis=0) - nblk
    ends_t = jnp.cumsum(nblk, axis=1)
    kk = jnp.arange(MOE_SLOTS, dtype=I32)
    slot_e = jnp.minimum(jnp.sum(ends_t[:, None, :] <= kk[None, :, None], axis=-1), E - 1).astype(I32)
    slot_valid = kk[None, :] < ends_t[:, -1:]
    start_t = jnp.take_along_axis(ends_t - nblk, slot_e, axis=1)
    slot_j = jnp.where(slot_valid, kk[None, :] - start_t, 0)
    slot_dest = jnp.where(slot_valid, jnp.take_along_axis(dest0, slot_e, axis=1) + slot_j, 0)
    npad = nbe_pad - nbe
    pm = jnp.arange(E * (group - 1), dtype=I32)
    pe, pt = pm // max(group - 1, 1), pm % max(group - 1, 1)
    pad_valid = pt < npad[pe]
    pad_dest = jnp.where(pad_valid, base[pe] + nbe[pe] + pt, 0)
    n_rt = nb_tot // group
    r0 = jnp.arange(n_rt, dtype=I32) * group
    rt_valid = r0 < ends_e[-1]
    rt_e = jnp.minimum(jnp.sum(ends_e[None, :] <= r0[:, None], axis=-1), E - 1)
    last_e = jnp.minimum(jnp.sum(ends_e <= ends_e[-1] - 1), E - 1)
    rt_e = jnp.where(rt_valid, rt_e, last_e)
    i32 = lambda a: a.astype(I32).reshape(-1)
    return dict(nb_tot=nb_tot, n_rt=n_rt, slot_e=i32(slot_e), slot_j=i32(slot_j), slot_dest=i32(slot_dest),
                slot_valid=i32(slot_valid), pad_dest=i32(pad_dest), pad_valid=i32(pad_valid),
                rt_e=i32(rt_e), rt_valid=i32(rt_valid), n_valid_rt=i32(ends_e[-1] // group))


def _moe_gather_kernel(se_ref, sj_ref, sd_ref, sv_ref, pd_ref, pv_ref, nv_ref, x_ref, rt_ref, xg_hbm,
                       buf, zbuf, sem, zsem, *, n_pad, n_rt, rows):
    i = pl.program_id(0)
    rt = rt_ref[0]
    height = MOE_SLOTS * MOE_SUB
    row = lax.broadcasted_iota(I32, (height, 1), 0)
    row_e = jnp.full((height, 1), -1.0, F32)
    row_r = (row % MOE_SUB).astype(F32)
    for k in range(MOE_SLOTS):
        idx = i * MOE_SLOTS + k
        here = (row // MOE_SUB) == k
        row_e = jnp.where(here, jnp.where(sv_ref[idx] == 1, se_ref[idx], -1).astype(F32), row_e)
        row_r = row_r + jnp.where(here, (sj_ref[idx] * MOE_SUB).astype(F32), 0.0)
    sel = ((rt[0:1, :] == row_e) & (rt[2:3, :] == row_r)) | ((rt[1:2, :] == row_e) & (rt[3:4, :] == row_r))
    buf[...] = _dot(jnp.where(sel, 1.0, 0.0).astype(BF16), x_ref[...]).astype(BF16)

    def slot_copy(k):
        dst = pl.multiple_of(sd_ref[i * MOE_SLOTS + k] * MOE_SUB, MOE_SUB)
        return pltpu.make_async_copy(buf.at[pl.ds(k * MOE_SUB, MOE_SUB)], xg_hbm.at[pl.ds(dst, MOE_SUB)], sem.at[k])

    for k in range(MOE_SLOTS):
        @pl.when(sv_ref[i * MOE_SLOTS + k] == 1)
        def _(k=k):
            slot_copy(k).start()

    for k in range(MOE_SLOTS):
        @pl.when(sv_ref[i * MOE_SLOTS + k] == 1)
        def _(k=k):
            slot_copy(k).wait()

    def pad_copy(m):
        dst = pl.multiple_of(pd_ref[m] * MOE_SUB, MOE_SUB)
        return pltpu.make_async_copy(zbuf.at[pl.ds(0, MOE_SUB)], xg_hbm.at[pl.ds(dst, MOE_SUB)], zsem.at[0])

    def tail_copy(r):
        dst = pl.multiple_of(r * rows, rows)
        return pltpu.make_async_copy(zbuf, xg_hbm.at[pl.ds(dst, rows)], zsem.at[1])

    @pl.when(i == pl.num_programs(0) - 1)
    def _():
        zbuf[...] = jnp.zeros_like(zbuf)

        def pad_start(m, carry):
            @pl.when(pv_ref[m] == 1)
            def _():
                pad_copy(m).start()
            return carry

        def pad_wait(m, carry):
            @pl.when(pv_ref[m] == 1)
            def _():
                pad_copy(m).wait()
            return carry

        def tail_start(r, carry):
            tail_copy(r).start()
            return carry

        def tail_wait(r, carry):
            tail_copy(r).wait()
            return carry

        lax.fori_loop(0, n_pad, pad_start, 0)
        lax.fori_loop(nv_ref[0], n_rt, tail_start, 0)
        lax.fori_loop(0, n_pad, pad_wait, 0)
        lax.fori_loop(nv_ref[0], n_rt, tail_wait, 0)


def _moe_gather(xb, rt, tb, rows):
    nb = tb["nb_tot"]
    tt = MOE_TOK_TILE
    nt = xb.shape[0] // tt
    n_pad = tb["pad_dest"].shape[0]
    return pl.pallas_call(
        functools.partial(_moe_gather_kernel, n_pad=n_pad, n_rt=tb["n_rt"], rows=rows),
        out_shape=jax.ShapeDtypeStruct((nb * MOE_SUB, D_MODEL), BF16),
        grid_spec=pltpu.PrefetchScalarGridSpec(
            num_scalar_prefetch=7,
            grid=(nt,),
            in_specs=[pl.BlockSpec((tt, D_MODEL), lambda i, *_: (i, 0)),
                      pl.BlockSpec((1, 8, tt), lambda i, *_: (i, 0, 0))],
            out_specs=pl.BlockSpec(memory_space=pl.ANY),
            scratch_shapes=[pltpu.VMEM((MOE_SLOTS * MOE_SUB, D_MODEL), BF16),
                            pltpu.VMEM((rows, D_MODEL), BF16),
                            pltpu.SemaphoreType.DMA((MOE_SLOTS,)),
                            pltpu.SemaphoreType.DMA((2,))]),
        compiler_params=_cparams(("arbitrary",)),
        name="moe_gather",
    )(tb["slot_e"], tb["slot_j"], tb["slot_dest"], tb["slot_valid"], tb["pad_dest"], tb["pad_valid"],
      tb["n_valid_rt"], xb, rt)


def _moe_ffn_kernel(e_ref, valid_ref, x_ref, wg_ref, wu_ref, wd_ref, y_ref, acc, *, rows):
    r = pl.program_id(0)
    f = pl.program_id(1)
    nf = pl.num_programs(1)
    ok = valid_ref[r] == 1

    @pl.when(jnp.logical_and(ok, f == 0))
    def _():
        acc[...] = jnp.zeros_like(acc)

    @pl.when(ok)
    def _():
        xb = x_ref[...]
        g = _dot(xb, wg_ref[0].astype(BF16))
        u = _dot(xb, wu_ref[0].astype(BF16))
        acc[...] += _dot((_silu(g) * u).astype(BF16), wd_ref[0].astype(BF16))

    @pl.when(jnp.logical_and(ok, f == nf - 1))
    def _():
        y_ref[...] = acc[...].astype(y_ref.dtype)

    @pl.when(jnp.logical_and(jnp.logical_not(ok), f == nf - 1))
    def _():
        y_ref[...] = jnp.zeros_like(y_ref)


def _moe_ffn(xg, w_gu_bf, w_d_bf, tb, rows):
    n_rt = tb["n_rt"]
    nf = D_FF_E // MOE_FF_BLK
    fb = MOE_FF_BLK

    def fsel(r, f, v):
        return jnp.where(v[r] == 1, f, nf - 1)

    return pl.pallas_call(
        functools.partial(_moe_ffn_kernel, rows=rows),
        out_shape=jax.ShapeDtypeStruct(xg.shape, BF16),
        grid_spec=pltpu.PrefetchScalarGridSpec(
            num_scalar_prefetch=2,
            grid=(n_rt, nf),
            in_specs=[pl.BlockSpec((rows, D_MODEL), lambda r, f, e, v: (r, 0)),
                      pl.BlockSpec((1, D_MODEL, fb), lambda r, f, e, v: (e[r], 0, fsel(r, f, v))),
                      pl.BlockSpec((1, D_MODEL, fb), lambda r, f, e, v: (e[r], 0, nf + fsel(r, f, v))),
                      pl.BlockSpec((1, fb, D_MODEL), lambda r, f, e, v: (e[r], fsel(r, f, v), 0))],
            out_specs=pl.BlockSpec((rows, D_MODEL), lambda r, f, e, v: (r, 0)),
            scratch_shapes=[pltpu.VMEM((rows, D_MODEL), F32)]),
        compiler_params=_cparams(("arbitrary", "arbitrary")),
        name="moe_grouped_ffn",
    )(tb["rt_e"], tb["rt_valid"], xg, w_gu_bf, w_gu_bf, w_d_bf)


def _moe_combine_kernel(se_ref, sj_ref, sd_ref, sv_ref, *refs):
    y_refs = refs[:MOE_SLOTS]
    rtt_ref, x_ref, g_ref, b_ref, o_ref, ybuf = refs[MOE_SLOTS:]
    i = pl.program_id(0)
    width = MOE_SLOTS * MOE_SUB
    lane = lax.broadcasted_iota(I32, (1, width), 1)
    lane_e = jnp.full((1, width), -1.0, F32)
    lane_r = (lane % MOE_SUB).astype(F32)
    for k in range(MOE_SLOTS):
        idx = i * MOE_SLOTS + k
        valid = sv_ref[idx] == 1
        yk = y_refs[k][...]
        ybuf[k * MOE_SUB:(k + 1) * MOE_SUB, :] = jnp.where(valid, yk, jnp.zeros_like(yk))
        here = (lane // MOE_SUB) == k
        lane_e = jnp.where(here, jnp.where(valid, se_ref[idx], -1).astype(F32), lane_e)
        lane_r = lane_r + jnp.where(here, (sj_ref[idx] * MOE_SUB).astype(F32), 0.0)
    rtt = rtt_ref[0]
    a1 = jnp.where((rtt[:, 0:1] == lane_e) & (rtt[:, 2:3] == lane_r), 1.0, 0.0).astype(BF16)
    a2 = jnp.where((rtt[:, 1:2] == lane_e) & (rtt[:, 3:4] == lane_r), 1.0, 0.0).astype(BF16)
    yb = ybuf[...]
    ff = rtt[:, 4:5] * _dot(a1, yb) + rtt[:, 5:6] * _dot(a2, yb)
    o_ref[...] = _layer_norm_rows(DN_ALPHA * x_ref[...] + ff, g_ref[...], b_ref[...])


def _moe_combine(yg, rtt, x, g, b, tb):
    tt = MOE_TOK_TILE
    T = x.shape[0]

    def y_map(i, se, sj, sd, sv, *, k):
        return (sd[i * MOE_SLOTS + k], 0)

    y_specs = [pl.BlockSpec((MOE_SUB, D_MODEL), functools.partial(y_map, k=k)) for k in range(MOE_SLOTS)]
    return pl.pallas_call(
        _moe_combine_kernel,
        out_shape=jax.ShapeDtypeStruct((T, D_MODEL), F32),
        grid_spec=pltpu.PrefetchScalarGridSpec(
            num_scalar_prefetch=4,
            grid=(T // tt,),
            in_specs=y_specs + [pl.BlockSpec((1, tt, 128), lambda i, *_: (i, 0, 0)),
                                pl.BlockSpec((tt, D_MODEL), lambda i, *_: (i, 0)),
                                pl.BlockSpec((1, D_MODEL), lambda i, *_: (0, 0)),
                                pl.BlockSpec((1, D_MODEL), lambda i, *_: (0, 0))],
            out_specs=pl.BlockSpec((tt, D_MODEL), lambda i, *_: (i, 0)),
            scratch_shapes=[pltpu.VMEM((MOE_SLOTS * MOE_SUB, D_MODEL), BF16)]),
        compiler_params=_cparams(("arbitrary",)),
        name="moe_combine",
    )(tb["slot_e"], tb["slot_j"], tb["slot_dest"], tb["slot_valid"], *([yg] * MOE_SLOTS), rtt, x,
      g.reshape(1, D_MODEL), b.reshape(1, D_MODEL))


def _moe(x, xb, w_router, w_gu_bf, w_d_bf, g, b, rows):
    T = x.shape[0]
    nt = T // MOE_TOK_TILE
    rt, rtt, cnt = _router(x, w_router)
    tb = _moe_tables(cnt[:, :, 0], nt, rows // MOE_SUB)
    xg = _moe_gather(xb, rt, tb, rows)
    yg = _moe_ffn(xg, w_gu_bf, w_d_bf, tb, rows)
    return _moe_combine(yg, rtt, x, g, b, tb)


def _trunk(x, pos_rows, n_batch, mem_k, mem_v, ret_state, past_k, past_v, conv_state, gdn_state, p, cfg):
    T = x.shape[0]
    per = T // n_batch
    tm = cfg["tm"]
    tm_mm = cfg["tm_mm"]
    ln_g, ln_b = p["ln_g"], p["ln_b"]

    qka, va, ga, qb, kb, vb = _proj0(x.astype(BF16), p["w_in0"], _rope_tables(pos_rows), tm_mm)
    oa, ret_new = _retention(qka, va, ga, ret_state, p["ret_norm_g"], n_batch, cfg["ret_lc"])
    lp = p["diff_lambda"].astype(F32)
    lam = (jnp.exp(jnp.sum(lp[0] * lp[1])) - jnp.exp(jnp.sum(lp[2] * lp[3])) + DIFF_LAMBDA_INIT).reshape(1)
    if past_k is None:
        ob = _diff_prompt(qb, kb, vb, lam, p["diff_norm_g"], n_batch, per, cfg["diff_tq"])
    else:
        ob = _diff_step(qb, kb, vb, past_k, past_v, lam, p["diff_norm_g"], n_batch, per)
    tm_dn = cfg["tm_dn"]
    x, q = _matmul_deepnorm([oa, ob], [p["w_out0"][:A_V], p["w_out0"][A_V:]], x, ln_g[0, 0], ln_b[0, 0], tm_dn,
                            w_next=p["w_xq"][0], next_scale=X_HD ** -0.5, name="out0_deepnorm_xq")
    xa = _xattn(q, mem_k, mem_v, 0, n_batch, cfg["x_tq"], cfg["mem_split"])
    x, xb = _matmul_deepnorm([xa], [p["w_xo"][0]], x, ln_g[0, 1], ln_b[0, 1], tm_dn, name="xo0_deepnorm")
    hmid = _swiglu_up(xb, p["w_ffn_gu"], cfg["ffn_tm"], cfg["ffn_tn"])
    x, xb = _matmul_deepnorm([hmid], [p["w_ffn_d"]], x, ln_g[0, 2], ln_b[0, 2], tm, name="ffn_down_deepnorm")

    proj1 = _matmul(xb, p["w_in1_main"][None], F32, tm_mm, 1024, name="proj1")[0]
    ab = _matmul(xb, p["w_in1_ab"][None], F32, tm_mm, 128, name="proj1_gates")[0]
    qkv3 = proj1.reshape(n_batch, per, 4 * C_W)[:, :, :3 * C_W]
    conv_new = qkv3[:, per - (CONV_W - 1):, :]
    state8 = jnp.pad(conv_state.astype(F32), ((0, 0), (8 - (CONV_W - 1), 0), (0, 0)))
    qkvn = _gdn_conv(proj1, state8, p["conv_w"], n_batch, cfg["conv_tm"])
    lc = cfg["gdn_lc"]
    if per < lc:
        padrows = lambda a: jnp.pad(a.reshape(n_batch, per, a.shape[-1]),
                                    ((0, 0), (0, lc - per), (0, 0))).reshape(n_batch * lc, a.shape[-1])
        og, gdn_new = _gdn(padrows(qkvn), padrows(proj1), padrows(ab), p["a_log"], p["dt_bias"],
                           p["gdn_norm_g"], gdn_state, n_batch, lc, per)
        og = og.reshape(n_batch, lc, C_W)[:, :per].reshape(T, C_W)
    else:
        og, gdn_new = _gdn(qkvn, proj1, ab, p["a_log"], p["dt_bias"], p["gdn_norm_g"], gdn_state,
                           n_batch, lc, lc)
    x, q = _matmul_deepnorm([og], [p["w_out1"]], x, ln_g[1, 0], ln_b[1, 0], tm_dn,
                            w_next=p["w_xq"][1], next_scale=X_HD ** -0.5, name="out1_deepnorm_xq")
    xa = _xattn(q, mem_k, mem_v, 1, n_batch, cfg["x_tq"], cfg["mem_split"])
    x, xb = _matmul_deepnorm([xa], [p["w_xo"][1]], x, ln_g[1, 1], ln_b[1, 1], tm_dn, name="xo1_deepnorm")
    x = _moe(x, xb, p["w_router"], p["w_moe_gu"], p["w_moe_d"], ln_g[1, 2], ln_b[1, 2], cfg["moe_rows"])
    return x, kb, vb, ret_new, conv_new, gdn_new


PROMPT_CFG = dict(tm=512, tm_mm=1024, tm_dn=1024, ret_lc=256, diff_tq=512, x_tq=512, ffn_tm=256, ffn_tn=2816, conv_tm=512,
                  gdn_lc=256, moe_rows=1024, mem_split=False)
STEP_CFG = dict(tm=512, tm_mm=512, tm_dn=512, ret_lc=16, diff_tq=16, x_tq=16, ffn_tm=256, ffn_tn=2816, conv_tm=16,
                gdn_lc=64, moe_rows=256, mem_split=True)


def kernel(x_prompt, x_sample, cache_diff_k, cache_diff_v, state_ret, state_gdn_conv, state_gdn, cache_mem_k, cache_mem_v, mem_prompt, w_in0, ret_norm_g, diff_lambda, diff_norm_g, w_out0, w_in1, conv_w, a_log, dt_bias, gdn_norm_g, w_out1, w_xq, w_xkv, w_xo, w_ffn_gu, w_ffn_d, w_router, w_moe_gu, w_moe_d, ln_g, ln_b):
    B, S, _ = x_prompt.shape
    DB, L, _ = x_sample.shape
    P = cache_diff_k.shape[1]
    bf = lambda w: w.astype(BF16)
    w_ab = jnp.pad(w_in1[:, 4 * C_W:], ((0, 0), (0, 128 - 2 * H_C)))
    p = dict(w_in0=bf(w_in0), ret_norm_g=ret_norm_g, diff_lambda=diff_lambda, diff_norm_g=diff_norm_g,
             w_out0=bf(w_out0), w_in1_main=bf(w_in1[:, :4 * C_W]), w_in1_ab=bf(w_ab), conv_w=conv_w,
             a_log=a_log, dt_bias=dt_bias, gdn_norm_g=gdn_norm_g, w_out1=bf(w_out1), w_xq=bf(w_xq),
             w_xo=bf(w_xo), w_ffn_gu=bf(w_ffn_gu), w_ffn_d=bf(w_ffn_d), w_router=w_router,
             w_moe_gu=w_moe_gu, w_moe_d=w_moe_d, ln_g=ln_g, ln_b=ln_b)

    mem = mem_prompt.reshape(B * N_MEM, D_MODEL)
    mk_p = _matmul(mem, bf(w_xkv[:, :, :D_MODEL]), F32, 1024, 1024, name="mem_k")
    mv_p = _matmul(mem, bf(w_xkv[:, :, D_MODEL:]), F32, 1024, 1024, name="mem_v")

    y_p, dk_p, dv_p, ret_p, conv_p, gdn_p = _trunk(
        x_prompt.reshape(B * S, D_MODEL), jnp.arange(S), B, mk_p, mv_p,
        jnp.zeros((B, H_A, DK_A, DV_A), F32), None, None,
        jnp.zeros((B, CONV_W - 1, 3 * C_W), F32), jnp.zeros((B, H_C, DK_C, DV_C), F32), p, PROMPT_CFG)

    pos_s = jnp.tile(P + jnp.arange(L), DB)

    def mem_rows(c):
        c = c.reshape(DEPTH, DB * N_MEM, X_HEADS, X_LANE_BLKS, 128).transpose(0, 1, 3, 2, 4)
        return c.reshape(DEPTH, DB * N_MEM * X_LANE_BLKS * X_HEADS, 128)

    y_s, dk_s, dv_s, ret_s, conv_s, gdn_s = _trunk(
        x_sample.reshape(DB * L, D_MODEL), pos_s, DB, mem_rows(cache_mem_k), mem_rows(cache_mem_v),
        state_ret.astype(F32), cache_diff_k.reshape(DB * P * H_B, DV_B), cache_diff_v.reshape(DB * P * H_B, DV_B),
        state_gdn_conv, state_gdn.astype(F32), p, STEP_CFG)

    shape5 = (DEPTH, B, N_MEM, X_HEADS, X_HD)
    return (y_p.reshape(B, S, D_MODEL), y_s.reshape(DB, L, D_MODEL),
            mk_p.reshape(shape5), mv_p.reshape(shape5),
            dk_p.reshape(B, S, H_B, 2 * DK_B), dv_p.reshape(B, S, H_B, DV_B),
            ret_p, conv_p, gdn_p,
            dk_s.reshape(DB, L, H_B, 2 * DK_B), dv_s.reshape(DB, L, H_B, DV_B),
            ret_s, conv_s, gdn_s)
```

```python
import functools
import math

import jax
import jax.numpy as jnp
from jax import lax
from jax.experimental import pallas as pl
from jax.experimental.pallas import tpu as pltpu

F32 = jnp.float32
BF16 = jnp.bfloat16
I32 = jnp.int32

D_MODEL = 1024
DEPTH = 2
CHUNK = 64
H_A, DK_A, DV_A = 4, 64, 128
RET_THETA = 10000.0
H_B, DK_B = 4, 64
DV_B = 2 * DK_B
ROT_B = DK_B // 4
ROPE_THETA = 500000.0
DIFF_LAMBDA_INIT = 0.8 - 0.6 * math.exp(-0.3 * 0)
H_C, DK_C, DV_C = 8, 128, 128
C_W = H_C * DV_C
CONV_W = 4
N_MEM = 256
X_HEADS = 4
X_HD = D_MODEL // X_HEADS
D_FF = 2816
N_EXPERTS = 8
D_FF_E = 3584
DN_ALPHA = (2 * DEPTH) ** 0.25
LN_EPS = 1e-5
A_QK = H_A * DK_A
A_V = H_A * DV_A
B_QK = H_B * 2 * DK_B
B_V = H_B * DV_B
PROJ0 = 2 * A_QK + 2 * A_V + 2 * B_QK + B_V

VMEM_LIMIT_V7X = 52 * 1024 * 1024
NEG_BIG = -1e30

MOE_TOK_TILE = 512
MOE_SUB = 32
MOE_FF_BLK = 512
MOE_SLOTS = 2 * MOE_TOK_TILE // MOE_SUB + N_EXPERTS
GDN_HEAD_GROUP = 4
CONV_ROWS = 32


def _cparams(sem):
    return pltpu.CompilerParams(dimension_semantics=sem, vmem_limit_bytes=VMEM_LIMIT_V7X)


def _dot(a, b):
    return jnp.dot(a, b, preferred_element_type=F32)


def _dot_nt(a, b):
    return lax.dot_general(a, b, (((1,), (1,)), ((), ())), preferred_element_type=F32)


def _dot_tn(a, b):
    return lax.dot_general(a, b, (((0,), (0,)), ((), ())), preferred_element_type=F32)


def _silu(x):
    return x * (1.0 / (1.0 + jnp.exp(-x)))


def _layer_norm_rows(y, g, b):
    mu = jnp.mean(y, axis=-1, keepdims=True)
    d = y - mu
    var = jnp.mean(d * d, axis=-1, keepdims=True)
    return d * lax.rsqrt(var + LN_EPS) * g + b


def _mm_kernel(x_ref, w_ref, o_ref, *, scale):
    acc = _dot(x_ref[...].astype(BF16), w_ref[0])
    if scale != 1.0:
        acc = acc * scale
    o_ref[0] = acc.astype(o_ref.dtype)


def _matmul(x, w, out_dtype, tm, tn, scale=1.0, name="matmul"):
    M, K = x.shape
    G, _, N = w.shape
    tm = min(tm, M)
    tn = min(tn, N)
    assert M % tm == 0 and N % tn == 0
    return pl.pallas_call(
        functools.partial(_mm_kernel, scale=scale),
        out_shape=jax.ShapeDtypeStruct((G, M, N), out_dtype),
        grid=(G, M // tm, N // tn),
        in_specs=[pl.BlockSpec((tm, K), lambda g, i, j: (i, 0)),
                  pl.BlockSpec((1, K, tn), lambda g, i, j: (g, 0, j))],
        out_specs=pl.BlockSpec((1, tm, tn), lambda g, i, j: (g, i, j)),
        compiler_params=_cparams(("parallel", "parallel", "arbitrary")),
        name=name,
    )(x, w)


def _mm_dn_kernel(*refs, n_in, next_scale):
    xs = refs[:n_in]
    ws = refs[n_in:2 * n_in]
    rest = refs[2 * n_in:]
    if next_scale is None:
        r_ref, g_ref, b_ref, o_ref, ob_ref = rest
    else:
        r_ref, g_ref, b_ref, wn_ref, o_ref, ob_ref = rest
    acc = DN_ALPHA * r_ref[...]
    for x_ref, w_ref in zip(xs, ws):
        acc = acc + _dot(x_ref[...].astype(BF16), w_ref[...])
    y = _layer_norm_rows(acc, g_ref[...], b_ref[...])
    o_ref[...] = y
    if next_scale is None:
        ob_ref[...] = y.astype(BF16)
    else:
        ob_ref[...] = (_dot(y.astype(BF16), wn_ref[...]) * next_scale).astype(BF16)


def _matmul_deepnorm(xs, ws, resid, g, b, tm, w_next=None, next_scale=None, name="matmul_deepnorm"):
    M = resid.shape[0]
    tm = min(tm, M)
    assert M % tm == 0
    n_in = len(xs)
    in_specs = [pl.BlockSpec((tm, x.shape[1]), lambda i: (i, 0)) for x in xs]
    in_specs += [pl.BlockSpec(w.shape, lambda i: (0, 0)) for w in ws]
    in_specs += [pl.BlockSpec((tm, D_MODEL), lambda i: (i, 0)),
                 pl.BlockSpec((1, D_MODEL), lambda i: (0, 0)),
                 pl.BlockSpec((1, D_MODEL), lambda i: (0, 0))]
    args = [*xs, *ws, resid, g.reshape(1, D_MODEL), b.reshape(1, D_MODEL)]
    if w_next is not None:
        in_specs.append(pl.BlockSpec(w_next.shape, lambda i: (0, 0)))
        args.append(w_next)
    return pl.pallas_call(
        functools.partial(_mm_dn_kernel, n_in=n_in, next_scale=next_scale),
        out_shape=[jax.ShapeDtypeStruct((M, D_MODEL), F32), jax.ShapeDtypeStruct((M, D_MODEL), BF16)],
        grid=(M // tm,),
        in_specs=in_specs,
        out_specs=[pl.BlockSpec((tm, D_MODEL), lambda i: (i, 0))] * 2,
        compiler_params=_cparams(("parallel",)),
        name=name,
    )(*args)


def _swap_halves(x, group, half):
    n = x.shape[-1]
    lane = lax.broadcasted_iota(I32, x.shape, x.ndim - 1) % group
    up = pltpu.roll(x, n - half, x.ndim - 1)
    dn = pltpu.roll(x, half, x.ndim - 1)
    return jnp.where(lane < half, up, dn)


def _store_rows(o_ref, y, by_head):
    if not by_head:
        o_ref[...] = y.astype(o_ref.dtype)
        return
    tm = y.shape[0]
    for h in range(H_B):
        o_ref[pl.ds(h, tm, stride=H_B), :] = y[:, h * DV_B:(h + 1) * DV_B].astype(o_ref.dtype)


def _proj0_kernel(*refs, kinds):
    x_ref = refs[0]
    n_out = len(kinds)
    ins, outs = refs[1:len(refs) - n_out], refs[len(refs) - n_out:]
    xb = x_ref[...]
    k = 0
    for (rot, by_head), o_ref in zip(kinds, outs):
        acc = _dot(xb, ins[k][...])
        k += 1
        if rot is not None:
            acc = acc * ins[k][...] + _swap_halves(acc, rot[0], rot[1]) * ins[k + 1][...]
            k += 2
        _store_rows(o_ref, acc, by_head)


def _proj0_call(xb, groups, tm, name):
    T = xb.shape[0]
    tm = min(tm, T)
    for g in groups:
        if g["rot"] is not None:
            tm = min(tm, g["rot"][0].shape[0])
    assert T % tm == 0
    row = lambda i: (i, 0)
    in_specs = [pl.BlockSpec((tm, D_MODEL), row)]
    args = [xb]
    kinds, out_shapes, out_specs = [], [], []
    for g in groups:
        blk = g["w"].shape[1]
        in_specs.append(pl.BlockSpec((D_MODEL, blk), lambda i: (0, 0)))
        args.append(g["w"])
        if g["rot"] is None:
            kinds.append((None, g["by_head"]))
        else:
            cos, sin, lane_group, half = g["rot"]
            assert cos.shape[0] % tm == 0
            npb = cos.shape[0] // tm
            in_specs += [pl.BlockSpec((tm, blk), lambda i, npb=npb: (i % npb, 0))] * 2
            args += [cos, sin]
            kinds.append(((lane_group, half), g["by_head"]))
        if g["by_head"]:
            assert blk == H_B * DV_B
            out_shapes.append(jax.ShapeDtypeStruct((T * H_B, DV_B), g["dtype"]))
            out_specs.append(pl.BlockSpec((tm * H_B, DV_B), row))
        else:
            out_shapes.append(jax.ShapeDtypeStruct((T, blk), g["dtype"]))
            out_specs.append(pl.BlockSpec((tm, blk), row))
    return pl.pallas_call(
        functools.partial(_proj0_kernel, kinds=tuple(kinds)),
        out_shape=out_shapes,
        grid=(T // tm,),
        in_specs=in_specs,
        out_specs=out_specs,
        compiler_params=_cparams(("parallel",)),
        name=name,
    )(*args)


def _rope_tables(pos):
    pos = pos.astype(F32)
    inv = RET_THETA ** (-jnp.arange(0, DK_A, 2, dtype=F32) / DK_A)
    ang = pos[:, None] * inv[None, :]
    c, s = jnp.cos(ang), jnp.sin(ang)
    c64 = jnp.concatenate([c, c], axis=-1)
    s64 = jnp.concatenate([-s, s], axis=-1)
    kscale = DK_A ** -0.5
    ca = jnp.concatenate([jnp.tile(c64, (1, H_A)), jnp.tile(c64, (1, H_A)) * kscale], axis=-1)
    sa = jnp.concatenate([jnp.tile(s64, (1, H_A)), jnp.tile(s64, (1, H_A)) * kscale], axis=-1)
    invb = ROPE_THETA ** (-jnp.arange(0, ROT_B, 2, dtype=F32) / ROT_B)
    angb = pos[:, None] * invb[None, :]
    cb, sb = jnp.cos(angb), jnp.sin(angb)
    rest = DK_B - ROT_B
    c64b = jnp.concatenate([cb, cb, jnp.ones((pos.shape[0], rest), F32)], axis=-1)
    s64b = jnp.concatenate([-sb, sb, jnp.zeros((pos.shape[0], rest), F32)], axis=-1)
    ck = jnp.tile(c64b, (1, 2 * H_B))
    sk = jnp.tile(s64b, (1, 2 * H_B))
    qscale = DK_B ** -0.5
    return ca, sa, ck * qscale, sk * qscale, ck, sk


def _proj0(xb, w_in0_bf, tables, tm):
    ca, sa, cq, sq, ck, sk = tables
    blk = 512
    assert PROJ0 == 6 * blk
    w = [w_in0_bf[:, j * blk:(j + 1) * blk] for j in range(6)]
    rot_a = (DK_A, DK_A // 2)
    rot_b = (DK_B, ROT_B // 2)
    grp = lambda j, dtype, rot=None, by_head=False: dict(w=w[j], dtype=dtype, rot=rot, by_head=by_head)
    qka, va = _proj0_call(xb, [grp(0, BF16, (ca, sa) + rot_a), grp(1, BF16)], tm, "proj0_ret_qkv")
    ga, vb = _proj0_call(xb, [grp(2, F32), grp(5, F32, by_head=True)], tm, "proj0_gate_v")
    qb, kb = _proj0_call(xb, [grp(3, BF16, (cq, sq) + rot_b), grp(4, F32, (ck, sk) + rot_b, by_head=True)],
                         tm, "proj0_diff_qk")
    return qka, va, ga, qb, kb, vb


def _retention_kernel(qk_ref, v_ref, g_ref, dmat_ref, qdec_ref, kdec_ref, sdec_ref, s0_ref, ng_ref,
                      o_ref, sout_ref, s_scr):
    c = pl.program_id(1)

    @pl.when(c == 0)
    def _():
        s_scr[...] = s0_ref[0]

    for h in range(H_A):
        q = qk_ref[:, h * DK_A:(h + 1) * DK_A]
        k = qk_ref[:, A_QK + h * DK_A:A_QK + (h + 1) * DK_A]
        v = v_ref[:, h * DV_A:(h + 1) * DV_A]
        s = s_scr[h]
        scores = _dot_nt(q, k) * dmat_ref[h]
        o = _dot(scores.astype(BF16), v) + _dot(q, s.astype(BF16)) * qdec_ref[h]
        kd = (k.astype(F32) * kdec_ref[h]).astype(BF16)
        s_scr[h] = s * sdec_ref[h] + _dot_tn(kd, v)
        mu = jnp.mean(o, axis=-1, keepdims=True)
        d = o - mu
        var = jnp.mean(d * d, axis=-1, keepdims=True)
        gate = g_ref[:, h * DV_A:(h + 1) * DV_A]
        y = d * lax.rsqrt(var + LN_EPS) * ng_ref[:, h * DV_A:(h + 1) * DV_A] * _silu(gate)
        o_ref[:, h * DV_A:(h + 1) * DV_A] = y.astype(o_ref.dtype)

    @pl.when(c == pl.num_programs(1) - 1)
    def _():
        sout_ref[0] = s_scr[...]


def _retention_tables(lc):
    lg = jnp.log1p(-jnp.exp2(-5.0 - jnp.arange(H_A, dtype=F32)))
    t = jnp.arange(lc, dtype=F32)
    rel = t[:, None] - t[None, :]
    causal = rel >= 0
    dmat = jnp.where(causal, jnp.exp(lg[:, None, None] * jnp.where(causal, rel, 0.0)), 0.0)
    qdec = jnp.exp(lg[:, None] * (t[None, :] + 1.0))
    kdec = jnp.exp(lg[:, None] * (lc - 1.0 - t[None, :]))
    sdec = jnp.exp(lg * lc)
    return (dmat,
            jnp.broadcast_to(qdec[:, :, None], (H_A, lc, DV_A)),
            jnp.broadcast_to(kdec[:, :, None], (H_A, lc, DK_A)),
            jnp.broadcast_to(sdec[:, None, None], (H_A, DK_A, DV_A)))


def _retention(qka, va, ga, state0, ret_norm_g, n_batch, lc):
    T = qka.shape[0]
    nc = T // (n_batch * lc)
    assert nc * n_batch * lc == T
    dmat, qdec, kdec, sdec = _retention_tables(lc)
    row = lambda b, c: (b * nc + c, 0)
    const3 = lambda b, c: (0, 0, 0)
    return pl.pallas_call(
        _retention_kernel,
        out_shape=[jax.ShapeDtypeStruct((T, A_V), BF16),
                   jax.ShapeDtypeStruct((n_batch, H_A, DK_A, DV_A), F32)],
        grid=(n_batch, nc),
        in_specs=[pl.BlockSpec((lc, 2 * A_QK), row),
                  pl.BlockSpec((lc, A_V), row),
                  pl.BlockSpec((lc, A_V), row),
                  pl.BlockSpec((H_A, lc, lc), const3),
                  pl.BlockSpec((H_A, lc, DV_A), const3),
                  pl.BlockSpec((H_A, lc, DK_A), const3),
                  pl.BlockSpec((H_A, DK_A, DV_A), const3),
                  pl.BlockSpec((1, H_A, DK_A, DV_A), lambda b, c: (b, 0, 0, 0)),
                  pl.BlockSpec((1, A_V), lambda b, c: (0, 0))],
        out_specs=[pl.BlockSpec((lc, A_V), row),
                   pl.BlockSpec((1, H_A, DK_A, DV_A), lambda b, c: (b, 0, 0, 0))],
        scratch_shapes=[pltpu.VMEM((H_A, DK_A, DV_A), F32)],
        compiler_params=_cparams(("parallel", "arbitrary")),
        name="retention",
    )(qka, va, ga, dmat, qdec, kdec, sdec, state0, ret_norm_g.reshape(1, A_V))


def _diff_finish(acc1, l1, acc2, l2, lam, ng):
    o = acc1 * (1.0 / l1) - lam * (acc2 * (1.0 / l2))
    ms = jnp.mean(o * o, axis=-1, keepdims=True)
    return o * lax.rsqrt(ms + LN_EPS) * ng * (1.0 - DIFF_LAMBDA_INIT)


def _diff_prompt_kernel(lam_ref, q_ref, k_ref, v_ref, ng_ref, o_ref, kbf, vbf, *, tq):
    qi = pl.program_id(2)

    @pl.when(qi == 0)
    def _():
        h = pl.program_id(1)
        seq = kbf.shape[0]
        kbf[...] = k_ref[pl.ds(h, seq, stride=H_B), :].astype(BF16)
        vbf[...] = v_ref[pl.ds(h, seq, stride=H_B), :].astype(BF16)

    rc = lax.broadcasted_iota(I32, (tq, tq), 0) // CHUNK
    cc = lax.broadcasted_iota(I32, (tq, tq), 1) // CHUNK
    vis = cc <= rc
    nq = kbf.shape[0] // tq

    for j in range(nq):
        @pl.when(qi == j)
        def _(j=j):
            n_full = j * tq
            kd = kbf[n_full:n_full + tq, :]
            vd = vbf[n_full:n_full + tq, :]
            accs = []
            for c in range(2):
                sl = slice(c * DK_B, (c + 1) * DK_B)
                qc = q_ref[:, sl]
                sd = jnp.where(vis, _dot_nt(qc, kd[:, sl]), NEG_BIG)
                m = jnp.max(sd, axis=-1, keepdims=True)
                if j > 0:
                    sf = _dot_nt(qc, kbf[0:n_full, sl])
                    m = jnp.maximum(m, jnp.max(sf, axis=-1, keepdims=True))
                pd = jnp.exp(sd - m)
                l = jnp.sum(pd, axis=-1, keepdims=True)
                acc = _dot(pd.astype(BF16), vd)
                if j > 0:
                    pf = jnp.exp(sf - m)
                    l = l + jnp.sum(pf, axis=-1, keepdims=True)
                    acc = acc + _dot(pf.astype(BF16), vbf[0:n_full, :])
                accs += [acc, l]
            o_ref[...] = _diff_finish(accs[0], accs[1], accs[2], accs[3], lam_ref[0], ng_ref[...]).astype(o_ref.dtype)


def _diff_prompt(qb, kb, vb, lam, diff_norm_g, n_batch, seq, tq):
    T = qb.shape[0]
    assert seq % tq == 0 and tq % CHUNK == 0
    nq = seq // tq
    return pl.pallas_call(
        functools.partial(_diff_prompt_kernel, tq=tq),
        out_shape=jax.ShapeDtypeStruct((T, B_V), BF16),
        grid_spec=pltpu.PrefetchScalarGridSpec(
            num_scalar_prefetch=1,
            grid=(n_batch, H_B, nq),
            in_specs=[pl.BlockSpec((tq, DV_B), lambda b, h, i, lam: (b * nq + i, h)),
                      pl.BlockSpec((seq * H_B, DV_B), lambda b, h, i, lam: (b, 0)),
                      pl.BlockSpec((seq * H_B, DV_B), lambda b, h, i, lam: (b, 0)),
                      pl.BlockSpec((1, DV_B), lambda b, h, i, lam: (0, 0))],
            out_specs=pl.BlockSpec((tq, DV_B), lambda b, h, i, lam: (b * nq + i, h)),
            scratch_shapes=[pltpu.VMEM((seq, DV_B), BF16), pltpu.VMEM((seq, DV_B), BF16)]),
        compiler_params=_cparams(("parallel", "parallel", "arbitrary")),
        name="diff_attn_prompt",
    )(lam, qb, kb, vb, diff_norm_g.reshape(1, DV_B))


def _diff_step_kernel(lam_ref, q_ref, kp_ref, vp_ref, kn_ref, vn_ref, ng_ref, o_ref, *, past, ln):
    qchunk = (past + lax.broadcasted_iota(I32, (ln, 1), 0)) // CHUNK
    vis_p = (lax.broadcasted_iota(I32, (ln, past), 1) // CHUNK) <= qchunk
    vis_n = ((past + lax.broadcasted_iota(I32, (ln, ln), 1)) // CHUNK) <= qchunk
    for h in range(H_B):
        q = q_ref[:, h * DV_B:(h + 1) * DV_B]
        kp = kp_ref[pl.ds(h, past, stride=H_B), :].astype(BF16)
        vp = vp_ref[pl.ds(h, past, stride=H_B), :].astype(BF16)
        kn = kn_ref[pl.ds(h, ln, stride=H_B), :].astype(BF16)
        vn = vn_ref[pl.ds(h, ln, stride=H_B), :].astype(BF16)
        accs = []
        for c in range(2):
            sl = slice(c * DK_B, (c + 1) * DK_B)
            sp = jnp.where(vis_p, _dot_nt(q[:, sl], kp[:, sl]), NEG_BIG)
            sn = jnp.where(vis_n, _dot_nt(q[:, sl], kn[:, sl]), NEG_BIG)
            m = jnp.maximum(jnp.max(sp, axis=-1, keepdims=True), jnp.max(sn, axis=-1, keepdims=True))
            pp = jnp.exp(sp - m)
            pn = jnp.exp(sn - m)
            l = jnp.sum(pp, axis=-1, keepdims=True) + jnp.sum(pn, axis=-1, keepdims=True)
            accs += [_dot(pp.astype(BF16), vp) + _dot(pn.astype(BF16), vn), l]
        y = _diff_finish(accs[0], accs[1], accs[2], accs[3], lam_ref[0], ng_ref[...])
        o_ref[:, h * DV_B:(h + 1) * DV_B] = y.astype(o_ref.dtype)


def _diff_step(qb, kb, vb, past_k, past_v, lam, diff_norm_g, n_batch, ln):
    past = past_k.shape[0] // (n_batch * H_B)
    blk = lambda b, lam: (b, 0)
    return pl.pallas_call(
        functools.partial(_diff_step_kernel, past=past, ln=ln),
        out_shape=jax.ShapeDtypeStruct((n_batch * ln, B_V), BF16),
        grid_spec=pltpu.PrefetchScalarGridSpec(
            num_scalar_prefetch=1,
            grid=(n_batch,),
            in_specs=[pl.BlockSpec((ln, B_QK), blk),
                      pl.BlockSpec((past * H_B, DV_B), blk),
                      pl.BlockSpec((past * H_B, DV_B), blk),
                      pl.BlockSpec((ln * H_B, DV_B), blk),
                      pl.BlockSpec((ln * H_B, DV_B), blk),
                      pl.BlockSpec((1, DV_B), lambda b, lam: (0, 0))],
            out_specs=pl.BlockSpec((ln, B_V), blk)),
        compiler_params=_cparams(("parallel",)),
        name="diff_attn_step",
    )(lam, qb, past_k, past_v, kb, vb, diff_norm_g.reshape(1, DV_B))


X_LANE_BLKS = X_HD // 128


def _xattn_kernel(q_ref, mk_ref, mv_ref, o_ref, *, split):
    stride = X_LANE_BLKS * X_HEADS
    for h in range(X_HEADS):
        if split:
            pieces = [(slice(h * X_HD + j * 128, h * X_HD + (j + 1) * 128), pl.ds(j * X_HEADS + h, N_MEM, stride=stride))
                      for j in range(X_LANE_BLKS)]
            s = sum(_dot_nt(q_ref[:, cs], mk_ref[rs, :].astype(BF16)) for cs, rs in pieces)
        else:
            sl = slice(h * X_HD, (h + 1) * X_HD)
            s = _dot_nt(q_ref[:, sl], mk_ref[:, sl].astype(BF16))
        p = jnp.exp(s - jnp.max(s, axis=-1, keepdims=True))
        inv_l = 1.0 / jnp.sum(p, axis=-1, keepdims=True)
        pb = p.astype(BF16)
        if split:
            for cs, rs in pieces:
                o_ref[:, cs] = (_dot(pb, mv_ref[rs, :].astype(BF16)) * inv_l).astype(o_ref.dtype)
        else:
            o_ref[:, sl] = (_dot(pb, mv_ref[:, sl].astype(BF16)) * inv_l).astype(o_ref.dtype)


def _xattn(q, mk, mv, layer, n_batch, tq, split):
    T = q.shape[0]
    per = T // n_batch
    tq = min(tq, per)
    nt = per // tq
    mem_blk = (None, N_MEM * X_LANE_BLKS * X_HEADS, 128) if split else (None, N_MEM, D_MODEL)
    return pl.pallas_call(
        functools.partial(_xattn_kernel, split=split),
        out_shape=jax.ShapeDtypeStruct((T, D_MODEL), BF16),
        grid=(n_batch, nt),
        in_specs=[pl.BlockSpec((tq, D_MODEL), lambda b, t: (b * nt + t, 0)),
                  pl.BlockSpec(mem_blk, lambda b, t: (layer, b, 0)),
                  pl.BlockSpec(mem_blk, lambda b, t: (layer, b, 0))],
        out_specs=pl.BlockSpec((tq, D_MODEL), lambda b, t: (b * nt + t, 0)),
        compiler_params=_cparams(("parallel", "arbitrary")),
        name="mem_xattn",
    )(q, mk, mv)


def _swiglu_up_kernel(x_ref, wg_ref, wu_ref, o_ref):
    xb = x_ref[...]
    g = _dot(xb, wg_ref[...])
    u = _dot(xb, wu_ref[...])
    o_ref[...] = (_silu(g) * u).astype(o_ref.dtype)


def _swiglu_up(xb, w_gu_bf, tm, tn):
    T = xb.shape[0]
    tm = min(tm, T)
    nj = D_FF // tn
    assert D_FF % tn == 0
    return pl.pallas_call(
        _swiglu_up_kernel,
        out_shape=jax.ShapeDtypeStruct((T, D_FF), BF16),
        grid=(T // tm, nj),
        in_specs=[pl.BlockSpec((tm, D_MODEL), lambda i, j: (i, 0)),
                  pl.BlockSpec((D_MODEL, tn), lambda i, j: (0, j)),
                  pl.BlockSpec((D_MODEL, tn), lambda i, j: (0, nj + j))],
        out_specs=pl.BlockSpec((tm, tn), lambda i, j: (i, j)),
        compiler_params=_cparams(("parallel", "arbitrary")),
        name="swiglu_up",
    )(xb, w_gu_bf, w_gu_bf)


def _conv_kernel(x_ref, prev_ref, st_ref, w_ref, o_ref, buf, *, tm):
    t = pl.program_id(1)
    cb = pl.program_id(2)
    buf[0:8, :] = jnp.where(t == 0, st_ref[0], prev_ref[...])
    buf[8:8 + tm, :] = x_ref[...]
    scale = jnp.where(cb == 0, DK_C ** -0.5, 1.0)
    rc = min(CONV_ROWS, tm)
    for h in range(H_C):
        cs = slice(h * DK_C, (h + 1) * DK_C)
        w = [w_ref[i:i + 1, cs] for i in range(CONV_W)]
        for r0 in range(0, tm, rc):
            y = buf[8 + r0:8 + r0 + rc, cs] * w[CONV_W - 1]
            for i in range(CONV_W - 1):
                y = y + buf[5 + i + r0:5 + i + r0 + rc, cs] * w[i]
            y = _silu(y)
            ss = jnp.sum(y * y, axis=-1, keepdims=True)
            f = jnp.where(cb == 2, 1.0, lax.rsqrt(ss + 1e-6) * scale)
            o_ref[r0:r0 + rc, cs] = (y * f).astype(o_ref.dtype)


def _gdn_conv(proj1, conv_state8, conv_w, n_batch, tm):
    T = proj1.shape[0]
    per = T // n_batch
    tm = min(tm, per)
    nt = per // tm
    assert per % tm == 0 and tm % 8 == 0
    return pl.pallas_call(
        functools.partial(_conv_kernel, tm=tm),
        out_shape=jax.ShapeDtypeStruct((T, 3 * C_W), BF16),
        grid=(n_batch, nt, 3),
        in_specs=[pl.BlockSpec((tm, C_W), lambda b, t, c: (b * nt + t, c)),
                  pl.BlockSpec((8, C_W), lambda b, t, c: (jnp.maximum((b * nt + t) * (tm // 8) - 1, 0), c)),
                  pl.BlockSpec((1, 8, C_W), lambda b, t, c: (b, 0, c)),
                  pl.BlockSpec((CONV_W, C_W), lambda b, t, c: (0, c))],
        out_specs=pl.BlockSpec((tm, C_W), lambda b, t, c: (b * nt + t, c)),
        scratch_shapes=[pltpu.VMEM((8 + tm, C_W), F32)],
        compiler_params=_cparams(("parallel", "arbitrary", "arbitrary")),
        name="gdn_conv",
    )(proj1, proj1, conv_state8, conv_w)


def _gdn_kernel(q_ref, k_ref, v_ref, z_ref, ab_ref, alog_ref, dtb_ref, ng_ref, s0_ref,
                o_ref, sout_ref, s_scr, *, lc, l_real):
    c = pl.program_id(1)

    @pl.when(c == 0)
    def _():
        s_scr[...] = s0_ref[0]

    ri = lax.broadcasted_iota(I32, (lc, lc), 0)
    ci = lax.broadcasted_iota(I32, (lc, lc), 1)
    incl = ci <= ri
    strict = ci < ri
    eye = (ci == ri).astype(F32)
    ab = ab_ref[...]
    sp = jnp.maximum(ab + dtb_ref[...], 0.0) + jnp.log1p(jnp.exp(-jnp.abs(ab + dtb_ref[...])))
    glog = -jnp.exp(alog_ref[...]) * sp
    if l_real < lc:
        live = lax.broadcasted_iota(I32, (lc, 1), 0) < l_real
        glog = jnp.where(live, glog, 0.0)
    gcum = jnp.dot(incl.astype(F32), glog, preferred_element_type=F32, precision=lax.Precision.HIGHEST)
    gcum_t = gcum.T
    beta_all = 1.0 / (1.0 + jnp.exp(-ab))

    levels = []
    s = 1
    while s < lc:
        levels.append(((ri // (2 * s)) == (ci // (2 * s))) & (((ri // s) % 2) == 1) & (((ci // s) % 2) == 0))
        s *= 2

    group = GDN_HEAD_GROUP if lc > 128 else H_C
    for h0 in range(0, H_C, group):
        heads = range(h0, h0 + group)
        sls = [slice(h * DK_C, (h + 1) * DK_C) for h in heads]
        ks, kfs, vs, gcols, decays, egs, kbs, a_lows = [], [], [], [], [], [], [], []
        for h, sl in zip(heads, sls):
            k = k_ref[:, sl]
            v = v_ref[:, sl].astype(F32)
            if l_real < lc:
                k = jnp.where(live, k, jnp.zeros_like(k))
                v = jnp.where(live, v, 0.0)
            kf = k.astype(F32)
            gcol = gcum[:, h:h + 1]
            grow = gcum_t[h:h + 1, :]
            beta = beta_all[:, H_C + h:H_C + h + 1]
            decay = jnp.exp(jnp.where(incl, gcol - grow, NEG_BIG))
            kb = kf * beta
            ks.append(k)
            kfs.append(kf)
            gcols.append(gcol)
            decays.append(decay)
            egs.append(jnp.exp(gcol))
            kbs.append(kb)
            vs.append(v * beta)
            a_lows.append(jnp.where(strict, _dot_nt(kb.astype(BF16), k) * decay, 0.0))
        tinvs = [eye - jnp.where(levels[0], a, 0.0) for a in a_lows]
        for lvl, off in enumerate(levels[1:], start=1):
            s = 2 ** lvl
            tbs = [t.astype(BF16) for t in tinvs]
            if s < 8:
                a_offs = [jnp.where(off, a, 0.0).astype(BF16) for a in a_lows]
                ws = [_dot(a, t).astype(BF16) for a, t in zip(a_offs, tbs)]
                tinvs = [t - _dot(tb, w) for t, tb, w in zip(tinvs, tbs, ws)]
            else:
                odd = [slice((2 * j + 1) * s, (2 * j + 2) * s) for j in range(lc // (2 * s))]
                even = [slice(2 * j * s, (2 * j + 1) * s) for j in range(lc // (2 * s))]
                take = lambda x: jnp.concatenate([x[sl, :] for sl in odd], axis=0)
                rh = lax.broadcasted_iota(I32, (lc // 2, lc), 0)
                ch = lax.broadcasted_iota(I32, (lc // 2, lc), 1)
                off_odd = (ch // s) == 2 * (rh // s)
                zero = jnp.zeros((s, lc), F32)
                new = []
                for a, t, tb in zip(a_lows, tinvs, tbs):
                    w_odd = _dot(jnp.where(off_odd, take(a), 0.0).astype(BF16), tb)
                    w_full = jnp.concatenate(
                        [piece for j in range(len(odd)) for piece in (zero, w_odd[j * s:(j + 1) * s, :])], axis=0)
                    t_odd = take(t)
                    t_odd = t_odd - _dot(t_odd.astype(BF16), w_full.astype(BF16))
                    new.append(jnp.concatenate(
                        [piece for j, ev in enumerate(even) for piece in (t[ev, :], t_odd[j * s:(j + 1) * s, :])],
                        axis=0))
                tinvs = new
        rhss = [jnp.concatenate([vb, kb * eg], axis=-1).astype(BF16) for vb, kb, eg in zip(vs, kbs, egs)]
        sols = [_dot(t.astype(BF16), r) for t, r in zip(tinvs, rhss)]
        sts = [s_scr[h] for h in heads]
        stbs = [st.astype(BF16) for st in sts]
        ubs = [(sol[:, :DV_C] - _dot(sol[:, DV_C:].astype(BF16), stb)).astype(BF16) for sol, stb in zip(sols, stbs)]
        qs = [q_ref[:, sl] for sl in sls]
        qks = [(_dot_nt(q, k) * decay).astype(BF16) for q, k, decay in zip(qs, ks, decays)]
        os_ = [_dot((q.astype(F32) * eg).astype(BF16), stb) + _dot(qk, ub)
               for q, eg, stb, qk, ub in zip(qs, egs, stbs, qks, ubs)]
        for h, st, kf, gcol, ub in zip(heads, sts, kfs, gcols, ubs):
            g_last = gcol[lc - 1:lc, :]
            kdec = (kf * jnp.exp(g_last - gcol)).astype(BF16)
            s_scr[h] = st * jnp.exp(g_last) + _dot_tn(kdec, ub)
        for sl, o in zip(sls, os_):
            ms = jnp.mean(o * o, axis=-1, keepdims=True)
            y = o * lax.rsqrt(ms + LN_EPS) * ng_ref[...] * _silu(z_ref[:, sl])
            o_ref[:, sl] = y.astype(o_ref.dtype)

    @pl.when(c == pl.num_programs(1) - 1)
    def _():
        sout_ref[0] = s_scr[...]


def _gdn(qkvn, proj1, ab, a_log, dt_bias, gdn_norm_g, state0, n_batch, lc, l_real):
    T = qkvn.shape[0]
    nc = T // (n_batch * lc)
    assert nc * n_batch * lc == T and (nc == 1 or l_real == lc)
    pad = lambda v: jnp.pad(v.astype(F32).reshape(1, H_C), ((0, 0), (0, 128 - H_C)))
    return pl.pallas_call(
        functools.partial(_gdn_kernel, lc=lc, l_real=l_real),
        out_shape=[jax.ShapeDtypeStruct((T, C_W), BF16),
                   jax.ShapeDtypeStruct((n_batch, H_C, DK_C, DV_C), F32)],
        grid=(n_batch, nc),
        in_specs=[pl.BlockSpec((lc, C_W), lambda b, c: (b * nc + c, 0)),
                  pl.BlockSpec((lc, C_W), lambda b, c: (b * nc + c, 1)),
                  pl.BlockSpec((lc, C_W), lambda b, c: (b * nc + c, 2)),
                  pl.BlockSpec((lc, C_W), lambda b, c: (b * nc + c, 3)),
                  pl.BlockSpec((lc, 128), lambda b, c: (b * nc + c, 0)),
                  pl.BlockSpec((1, 128), lambda b, c: (0, 0)),
                  pl.BlockSpec((1, 128), lambda b, c: (0, 0)),
                  pl.BlockSpec((1, DV_C), lambda b, c: (0, 0)),
                  pl.BlockSpec((1, H_C, DK_C, DV_C), lambda b, c: (b, 0, 0, 0))],
        out_specs=[pl.BlockSpec((lc, C_W), lambda b, c: (b * nc + c, 0)),
                   pl.BlockSpec((1, H_C, DK_C, DV_C), lambda b, c: (b, 0, 0, 0))],
        scratch_shapes=[pltpu.VMEM((H_C, DK_C, DV_C), F32)],
        compiler_params=_cparams(("parallel", "arbitrary")),
        name="gated_delta",
    )(qkvn, qkvn, qkvn, proj1, ab, pad(a_log), pad(dt_bias), gdn_norm_g.reshape(1, DV_C), state0)


def _router_kernel(x_ref, wt_ref, rt_ref, rtt_ref, cnt_ref):
    tt = x_ref.shape[0]
    logits = lax.dot_general(wt_ref[...], x_ref[...], (((1,), (1,)), ((), ())),
                             preferred_element_type=F32, precision=lax.Precision.HIGHEST)
    eid = lax.broadcasted_iota(I32, (N_EXPERTS, tt), 0).astype(F32)
    m1 = jnp.max(logits, axis=0, keepdims=True)
    e1 = jnp.min(jnp.where(logits == m1, eid, float(N_EXPERTS)), axis=0, keepdims=True)
    rest = jnp.where(eid == e1, -jnp.inf, logits)
    m2 = jnp.max(rest, axis=0, keepdims=True)
    e2 = jnp.min(jnp.where(rest == m2, eid, float(N_EXPERTS)), axis=0, keepdims=True)
    ev = jnp.exp(m2 - m1)
    g1 = 1.0 / (1.0 + ev)
    g2 = ev / (1.0 + ev)
    hit1 = eid == e1
    hit2 = eid == e2
    member = jnp.where(hit1 | hit2, 1.0, 0.0)
    before = (lax.broadcasted_iota(I32, (tt, tt), 0) < lax.broadcasted_iota(I32, (tt, tt), 1))
    rank = _dot(member.astype(BF16), jnp.where(before, 1.0, 0.0).astype(BF16))
    r1 = jnp.sum(jnp.where(hit1, rank, 0.0), axis=0, keepdims=True)
    r2 = jnp.sum(jnp.where(hit2, rank, 0.0), axis=0, keepdims=True)
    rows = jnp.concatenate([e1, e2, r1, r2, g1, g2, jnp.zeros((2, tt), F32)], axis=0)
    rt_ref[0] = rows
    rtt_ref[0] = jnp.concatenate([rows, jnp.zeros((128 - 8, tt), F32)], axis=0).T
    cnt = jnp.sum(member, axis=1, keepdims=True).astype(I32)
    cnt_ref[0] = jnp.broadcast_to(cnt, (N_EXPERTS, 128))


def _router(x, w_router):
    T = x.shape[0]
    tt = MOE_TOK_TILE
    nt = T // tt
    assert T % tt == 0
    return pl.pallas_call(
        _router_kernel,
        out_shape=[jax.ShapeDtypeStruct((nt, 8, tt), F32),
                   jax.ShapeDtypeStruct((nt, tt, 128), F32),
                   jax.ShapeDtypeStruct((nt, N_EXPERTS, 128), I32)],
        grid=(nt,),
        in_specs=[pl.BlockSpec((tt, D_MODEL), lambda i: (i, 0)),
                  pl.BlockSpec((N_EXPERTS, D_MODEL), lambda i: (0, 0))],
        out_specs=[pl.BlockSpec((1, 8, tt), lambda i: (i, 0, 0)),
                   pl.BlockSpec((1, tt, 128), lambda i: (i, 0, 0)),
                   pl.BlockSpec((1, N_EXPERTS, 128), lambda i: (i, 0, 0))],
        compiler_params=_cparams(("parallel",)),
        name="moe_router",
    )(x, w_router.T)


def _moe_tables(cnt, n_tiles, group):
    E = N_EXPERTS
    nb_max = (2 * n_tiles * MOE_TOK_TILE) // MOE_SUB + n_tiles * E
    nb_tot = -(-(nb_max + E * (group - 1)) // group) * group
    nblk = (cnt + MOE_SUB - 1) // MOE_SUB
    nbe = jnp.sum(nblk, axis=0)
    nbe_pad = (nbe + group - 1) // group * group
    ends_e = jnp.cumsum(nbe_pad)
    base = ends_e - nbe_pad
    dest0 = base[None, :] + jnp.cumsum(nblk, axis=0) - nblk
    ends_t = jnp.cumsum(nblk, axis=1)
    kk = jnp.arange(MOE_SLOTS, dtype=I32)
    slot_e = jnp.minimum(jnp.sum(ends_t[:, None, :] <= kk[None, :, None], axis=-1), E - 1).astype(I32)
    slot_valid = kk[None, :] < ends_t[:, -1:]
    start_t = jnp.take_along_axis(ends_t - nblk, slot_e, axis=1)
    slot_j = jnp.where(slot_valid, kk[None, :] - start_t, 0)
    slot_dest = jnp.where(slot_valid, jnp.take_along_axis(dest0, slot_e, axis=1) + slot_j, 0)
    npad = nbe_pad - nbe
    pm = jnp.arange(E * (group - 1), dtype=I32)
    pe, pt = pm // max(group - 1, 1), pm % max(group - 1, 1)
    pad_valid = pt < npad[pe]
    pad_dest = jnp.where(pad_valid, base[pe] + nbe[pe] + pt, 0)
    n_rt = nb_tot // group
    r0 = jnp.arange(n_rt, dtype=I32) * group
    rt_valid = r0 < ends_e[-1]
    rt_e = jnp.minimum(jnp.sum(ends_e[None, :] <= r0[:, None], axis=-1), E - 1)
    last_e = jnp.minimum(jnp.sum(ends_e <= ends_e[-1] - 1), E - 1)
    rt_e = jnp.where(rt_valid, rt_e, last_e)
    i32 = lambda a: a.astype(I32).reshape(-1)
    return dict(nb_tot=nb_tot, n_rt=n_rt, slot_e=i32(slot_e), slot_j=i32(slot_j), slot_dest=i32(slot_dest),
                slot_valid=i32(slot_valid), pad_dest=i32(pad_dest), pad_valid=i32(pad_valid),
                rt_e=i32(rt_e), rt_valid=i32(rt_valid), n_valid_rt=i32(ends_e[-1] // group))


def _moe_gather_kernel(se_ref, sj_ref, sd_ref, sv_ref, pd_ref, pv_ref, nv_ref, x_ref, rt_ref, xg_hbm,
                       buf, zbuf, sem, zsem, *, n_pad, n_rt, rows):
    i = pl.program_id(0)
    rt = rt_ref[0]
    height = MOE_SLOTS * MOE_SUB
    row = lax.broadcasted_iota(I32, (height, 1), 0)
    row_e = jnp.full((height, 1), -1.0, F32)
    row_r = (row % MOE_SUB).astype(F32)
    for k in range(MOE_SLOTS):
        idx = i * MOE_SLOTS + k
        here = (row // MOE_SUB) == k
        row_e = jnp.where(here, jnp.where(sv_ref[idx] == 1, se_ref[idx], -1).astype(F32), row_e)
        row_r = row_r + jnp.where(here, (sj_ref[idx] * MOE_SUB).astype(F32), 0.0)
    sel = ((rt[0:1, :] == row_e) & (rt[2:3, :] == row_r)) | ((rt[1:2, :] == row_e) & (rt[3:4, :] == row_r))
    buf[...] = _dot(jnp.where(sel, 1.0, 0.0).astype(BF16), x_ref[...]).astype(BF16)

    def slot_copy(k):
        dst = pl.multiple_of(sd_ref[i * MOE_SLOTS + k] * MOE_SUB, MOE_SUB)
        return pltpu.make_async_copy(buf.at[pl.ds(k * MOE_SUB, MOE_SUB)], xg_hbm.at[pl.ds(dst, MOE_SUB)], sem.at[k])

    for k in range(MOE_SLOTS):
        @pl.when(sv_ref[i * MOE_SLOTS + k] == 1)
        def _(k=k):
            slot_copy(k).start()

    for k in range(MOE_SLOTS):
        @pl.when(sv_ref[i * MOE_SLOTS + k] == 1)
        def _(k=k):
            slot_copy(k).wait()

    def pad_copy(m):
        dst = pl.multiple_of(pd_ref[m] * MOE_SUB, MOE_SUB)
        return pltpu.make_async_copy(zbuf.at[pl.ds(0, MOE_SUB)], xg_hbm.at[pl.ds(dst, MOE_SUB)], zsem.at[0])

    def tail_copy(r):
        dst = pl.multiple_of(r * rows, rows)
        return pltpu.make_async_copy(zbuf, xg_hbm.at[pl.ds(dst, rows)], zsem.at[1])

    @pl.when(i == pl.num_programs(0) - 1)
    def _():
        zbuf[...] = jnp.zeros_like(zbuf)

        def pad_start(m, carry):
            @pl.when(pv_ref[m] == 1)
            def _():
                pad_copy(m).start()
            return carry

        def pad_wait(m, carry):
            @pl.when(pv_ref[m] == 1)
            def _():
                pad_copy(m).wait()
            return carry

        def tail_start(r, carry):
            tail_copy(r).start()
            return carry

        def tail_wait(r, carry):
            tail_copy(r).wait()
            return carry

        lax.fori_loop(0, n_pad, pad_start, 0)
        lax.fori_loop(nv_ref[0], n_rt, tail_start, 0)
        lax.fori_loop(0, n_pad, pad_wait, 0)
        lax.fori_loop(nv_ref[0], n_rt, tail_wait, 0)


def _moe_gather(xb, rt, tb, rows):
    nb = tb["nb_tot"]
    tt = MOE_TOK_TILE
    nt = xb.shape[0] // tt
    n_pad = tb["pad_dest"].shape[0]
    return pl.pallas_call(
        functools.partial(_moe_gather_kernel, n_pad=n_pad, n_rt=tb["n_rt"], rows=rows),
        out_shape=jax.ShapeDtypeStruct((nb * MOE_SUB, D_MODEL), BF16),
        grid_spec=pltpu.PrefetchScalarGridSpec(
            num_scalar_prefetch=7,
            grid=(nt,),
            in_specs=[pl.BlockSpec((tt, D_MODEL), lambda i, *_: (i, 0)),
                      pl.BlockSpec((1, 8, tt), lambda i, *_: (i, 0, 0))],
            out_specs=pl.BlockSpec(memory_space=pl.ANY),
            scratch_shapes=[pltpu.VMEM((MOE_SLOTS * MOE_SUB, D_MODEL), BF16),
                            pltpu.VMEM((rows, D_MODEL), BF16),
                            pltpu.SemaphoreType.DMA((MOE_SLOTS,)),
                            pltpu.SemaphoreType.DMA((2,))]),
        compiler_params=_cparams(("arbitrary",)),
        name="moe_gather",
    )(tb["slot_e"], tb["slot_j"], tb["slot_dest"], tb["slot_valid"], tb["pad_dest"], tb["pad_valid"],
      tb["n_valid_rt"], xb, rt)


def _moe_ffn_kernel(e_ref, valid_ref, x_ref, wg_ref, wu_ref, wd_ref, y_ref, acc, *, rows):
    r = pl.program_id(0)
    f = pl.program_id(1)
    nf = pl.num_programs(1)
    ok = valid_ref[r] == 1

    def partial_out():
        xb = x_ref[...]
        g = _dot(xb, wg_ref[0].astype(BF16))
        u = _dot(xb, wu_ref[0].astype(BF16))
        return _dot((_silu(g) * u).astype(BF16), wd_ref[0].astype(BF16))

    @pl.when(jnp.logical_and(ok, f == 0))
    def _():
        acc[...] = partial_out()

    @pl.when(jnp.logical_and(ok, jnp.logical_and(f > 0, f < nf - 1)))
    def _():
        acc[...] += partial_out()

    @pl.when(jnp.logical_and(ok, f == nf - 1))
    def _():
        y_ref[...] = (acc[...] + partial_out()).astype(y_ref.dtype)

    @pl.when(jnp.logical_and(jnp.logical_not(ok), f == nf - 1))
    def _():
        y_ref[...] = jnp.zeros_like(y_ref)


def _moe_ffn(xg, w_gu_bf, w_d_bf, tb, rows):
    n_rt = tb["n_rt"]
    nf = D_FF_E // MOE_FF_BLK
    fb = MOE_FF_BLK

    def fsel(r, f, v):
        return jnp.where(v[r] == 1, f, nf - 1)

    return pl.pallas_call(
        functools.partial(_moe_ffn_kernel, rows=rows),
        out_shape=jax.ShapeDtypeStruct(xg.shape, BF16),
        grid_spec=pltpu.PrefetchScalarGridSpec(
            num_scalar_prefetch=2,
            grid=(n_rt, nf),
            in_specs=[pl.BlockSpec((rows, D_MODEL), lambda r, f, e, v: (r, 0)),
                      pl.BlockSpec((1, D_MODEL, fb), lambda r, f, e, v: (e[r], 0, fsel(r, f, v))),
                      pl.BlockSpec((1, D_MODEL, fb), lambda r, f, e, v: (e[r], 0, nf + fsel(r, f, v))),
                      pl.BlockSpec((1, fb, D_MODEL), lambda r, f, e, v: (e[r], fsel(r, f, v), 0))],
            out_specs=pl.BlockSpec((rows, D_MODEL), lambda r, f, e, v: (r, 0)),
            scratch_shapes=[pltpu.VMEM((rows, D_MODEL), F32)]),
        compiler_params=_cparams(("arbitrary", "arbitrary")),
        name="moe_grouped_ffn",
    )(tb["rt_e"], tb["rt_valid"], xg, w_gu_bf, w_gu_bf, w_d_bf)


def _moe_combine_kernel(se_ref, sj_ref, sd_ref, sv_ref, *refs):
    y_refs = refs[:MOE_SLOTS]
    rtt_ref, x_ref, g_ref, b_ref, o_ref, ybuf = refs[MOE_SLOTS:]
    i = pl.program_id(0)
    width = MOE_SLOTS * MOE_SUB
    lane = lax.broadcasted_iota(I32, (1, width), 1)
    lane_e = jnp.full((1, width), -1.0, F32)
    lane_r = (lane % MOE_SUB).astype(F32)
    for k in range(MOE_SLOTS):
        idx = i * MOE_SLOTS + k
        valid = sv_ref[idx] == 1
        yk = y_refs[k][...]
        ybuf[k * MOE_SUB:(k + 1) * MOE_SUB, :] = jnp.where(valid, yk, jnp.zeros_like(yk))
        here = (lane // MOE_SUB) == k
        lane_e = jnp.where(here, jnp.where(valid, se_ref[idx], -1).astype(F32), lane_e)
        lane_r = lane_r + jnp.where(here, (sj_ref[idx] * MOE_SUB).astype(F32), 0.0)
    rtt = rtt_ref[0]
    a1 = jnp.where((rtt[:, 0:1] == lane_e) & (rtt[:, 2:3] == lane_r), 1.0, 0.0).astype(BF16)
    a2 = jnp.where((rtt[:, 1:2] == lane_e) & (rtt[:, 3:4] == lane_r), 1.0, 0.0).astype(BF16)
    yb = ybuf[...]
    ff = rtt[:, 4:5] * _dot(a1, yb) + rtt[:, 5:6] * _dot(a2, yb)
    o_ref[...] = _layer_norm_rows(DN_ALPHA * x_ref[...] + ff, g_ref[...], b_ref[...])


def _moe_combine(yg, rtt, x, g, b, tb):
    tt = MOE_TOK_TILE
    T = x.shape[0]

    def y_map(i, se, sj, sd, sv, *, k):
        return (sd[i * MOE_SLOTS + k], 0)

    y_specs = [pl.BlockSpec((MOE_SUB, D_MODEL), functools.partial(y_map, k=k)) for k in range(MOE_SLOTS)]
    return pl.pallas_call(
        _moe_combine_kernel,
        out_shape=jax.ShapeDtypeStruct((T, D_MODEL), F32),
        grid_spec=pltpu.PrefetchScalarGridSpec(
            num_scalar_prefetch=4,
            grid=(T // tt,),
            in_specs=y_specs + [pl.BlockSpec((1, tt, 128), lambda i, *_: (i, 0, 0)),
                                pl.BlockSpec((tt, D_MODEL), lambda i, *_: (i, 0)),
                                pl.BlockSpec((1, D_MODEL), lambda i, *_: (0, 0)),
                                pl.BlockSpec((1, D_MODEL), lambda i, *_: (0, 0))],
            out_specs=pl.BlockSpec((tt, D_MODEL), lambda i, *_: (i, 0)),
            scratch_shapes=[pltpu.VMEM((MOE_SLOTS * MOE_SUB, D_MODEL), BF16)]),
        compiler_params=_cparams(("arbitrary",)),
        name="moe_combine",
    )(tb["slot_e"], tb["slot_j"], tb["slot_dest"], tb["slot_valid"], *([yg] * MOE_SLOTS), rtt, x,
      g.reshape(1, D_MODEL), b.reshape(1, D_MODEL))


def _moe(x, xb, w_router, w_gu_bf, w_d_bf, g, b, rows):
    T = x.shape[0]
    nt = T // MOE_TOK_TILE
    rt, rtt, cnt = _router(x, w_router)
    tb = _moe_tables(cnt[:, :, 0], nt, rows // MOE_SUB)
    xg = _moe_gather(xb, rt, tb, rows)
    yg = _moe_ffn(xg, w_gu_bf, w_d_bf, tb, rows)
    return _moe_combine(yg, rtt, x, g, b, tb)


def _trunk(x, pos_rows, n_batch, mem_k, mem_v, ret_state, past_k, past_v, conv_state, gdn_state, p, cfg):
    T = x.shape[0]
    per = T // n_batch
    tm = cfg["tm"]
    tm_mm = cfg["tm_mm"]
    ln_g, ln_b = p["ln_g"], p["ln_b"]

    qka, va, ga, qb, kb, vb = _proj0(x.astype(BF16), p["w_in0"], _rope_tables(pos_rows), tm_mm)
    oa, ret_new = _retention(qka, va, ga, ret_state, p["ret_norm_g"], n_batch, cfg["ret_lc"])
    lp = p["diff_lambda"].astype(F32)
    lam = (jnp.exp(jnp.sum(lp[0] * lp[1])) - jnp.exp(jnp.sum(lp[2] * lp[3])) + DIFF_LAMBDA_INIT).reshape(1)
    if past_k is None:
        ob = _diff_prompt(qb, kb, vb, lam, p["diff_norm_g"], n_batch, per, cfg["diff_tq"])
    else:
        ob = _diff_step(qb, kb, vb, past_k, past_v, lam, p["diff_norm_g"], n_batch, per)
    tm_dn = cfg["tm_dn"]
    x, q = _matmul_deepnorm([oa, ob], [p["w_out0"][:A_V], p["w_out0"][A_V:]], x, ln_g[0, 0], ln_b[0, 0], tm_dn,
                            w_next=p["w_xq"][0], next_scale=X_HD ** -0.5, name="out0_deepnorm_xq")
    xa = _xattn(q, mem_k, mem_v, 0, n_batch, cfg["x_tq"], cfg["mem_split"])
    x, xb = _matmul_deepnorm([xa], [p["w_xo"][0]], x, ln_g[0, 1], ln_b[0, 1], tm_dn, name="xo0_deepnorm")
    hmid = _swiglu_up(xb, p["w_ffn_gu"], cfg["ffn_tm"], cfg["ffn_tn"])
    x, xb = _matmul_deepnorm([hmid], [p["w_ffn_d"]], x, ln_g[0, 2], ln_b[0, 2], tm, name="ffn_down_deepnorm")

    proj1 = _matmul(xb, p["w_in1_main"][None], F32, tm_mm, 1024, name="proj1")[0]
    ab = _matmul(xb, p["w_in1_ab"][None], F32, tm_mm, 128, name="proj1_gates")[0]
    qkv3 = proj1.reshape(n_batch, per, 4 * C_W)[:, :, :3 * C_W]
    conv_new = qkv3[:, per - (CONV_W - 1):, :]
    state8 = jnp.pad(conv_state.astype(F32), ((0, 0), (8 - (CONV_W - 1), 0), (0, 0)))
    qkvn = _gdn_conv(proj1, state8, p["conv_w"], n_batch, cfg["conv_tm"])
    lc = cfg["gdn_lc"]
    if per < lc:
        padrows = lambda a: jnp.pad(a.reshape(n_batch, per, a.shape[-1]),
                                    ((0, 0), (0, lc - per), (0, 0))).reshape(n_batch * lc, a.shape[-1])
        og, gdn_new = _gdn(padrows(qkvn), padrows(proj1), padrows(ab), p["a_log"], p["dt_bias"],
                           p["gdn_norm_g"], gdn_state, n_batch, lc, per)
        og = og.reshape(n_batch, lc, C_W)[:, :per].reshape(T, C_W)
    else:
        og, gdn_new = _gdn(qkvn, proj1, ab, p["a_log"], p["dt_bias"], p["gdn_norm_g"], gdn_state,
                           n_batch, lc, lc)
    x, q = _matmul_deepnorm([og], [p["w_out1"]], x, ln_g[1, 0], ln_b[1, 0], tm_dn,
                            w_next=p["w_xq"][1], next_scale=X_HD ** -0.5, name="out1_deepnorm_xq")
    xa = _xattn(q, mem_k, mem_v, 1, n_batch, cfg["x_tq"], cfg["mem_split"])
    x, xb = _matmul_deepnorm([xa], [p["w_xo"][1]], x, ln_g[1, 1], ln_b[1, 1], tm_dn, name="xo1_deepnorm")
    x = _moe(x, xb, p["w_router"], p["w_moe_gu"], p["w_moe_d"], ln_g[1, 2], ln_b[1, 2], cfg["moe_rows"])
    return x, kb, vb, ret_new, conv_new, gdn_new


PROMPT_CFG = dict(tm=512, tm_mm=1024, tm_dn=1024, ret_lc=256, diff_tq=512, x_tq=512, ffn_tm=256, ffn_tn=2816, conv_tm=512,
                  gdn_lc=256, moe_rows=1024, mem_split=False)
STEP_CFG = dict(tm=512, tm_mm=512, tm_dn=512, ret_lc=16, diff_tq=16, x_tq=16, ffn_tm=256, ffn_tn=2816, conv_tm=16,
                gdn_lc=64, moe_rows=256, mem_split=True)


def kernel(x_prompt, x_sample, cache_diff_k, cache_diff_v, state_ret, state_gdn_conv, state_gdn, cache_mem_k, cache_mem_v, mem_prompt, w_in0, ret_norm_g, diff_lambda, diff_norm_g, w_out0, w_in1, conv_w, a_log, dt_bias, gdn_norm_g, w_out1, w_xq, w_xkv, w_xo, w_ffn_gu, w_ffn_d, w_router, w_moe_gu, w_moe_d, ln_g, ln_b):
    B, S, _ = x_prompt.shape
    DB, L, _ = x_sample.shape
    P = cache_diff_k.shape[1]
    bf = lambda w: w.astype(BF16)
    w_ab = jnp.pad(w_in1[:, 4 * C_W:], ((0, 0), (0, 128 - 2 * H_C)))
    p = dict(w_in0=bf(w_in0), ret_norm_g=ret_norm_g, diff_lambda=diff_lambda, diff_norm_g=diff_norm_g,
             w_out0=bf(w_out0), w_in1_main=bf(w_in1[:, :4 * C_W]), w_in1_ab=bf(w_ab), conv_w=conv_w,
             a_log=a_log, dt_bias=dt_bias, gdn_norm_g=gdn_norm_g, w_out1=bf(w_out1), w_xq=bf(w_xq),
             w_xo=bf(w_xo), w_ffn_gu=bf(w_ffn_gu), w_ffn_d=bf(w_ffn_d), w_router=w_router,
             w_moe_gu=w_moe_gu, w_moe_d=w_moe_d, ln_g=ln_g, ln_b=ln_b)

    mem = mem_prompt.reshape(B * N_MEM, D_MODEL)
    mk_p = _matmul(mem, bf(w_xkv[:, :, :D_MODEL]), F32, 1024, 1024, name="mem_k")
    mv_p = _matmul(mem, bf(w_xkv[:, :, D_MODEL:]), F32, 1024, 1024, name="mem_v")

    y_p, dk_p, dv_p, ret_p, conv_p, gdn_p = _trunk(
        x_prompt.reshape(B * S, D_MODEL), jnp.arange(S), B, mk_p, mv_p,
        jnp.zeros((B, H_A, DK_A, DV_A), F32), None, None,
        jnp.zeros((B, CONV_W - 1, 3 * C_W), F32), jnp.zeros((B, H_C, DK_C, DV_C), F32), p, PROMPT_CFG)

    pos_s = jnp.tile(P + jnp.arange(L), DB)

    def mem_rows(c):
        c = c.reshape(DEPTH, DB * N_MEM, X_HEADS, X_LANE_BLKS, 128).transpose(0, 1, 3, 2, 4)
        return c.reshape(DEPTH, DB * N_MEM * X_LANE_BLKS * X_HEADS, 128)

    y_s, dk_s, dv_s, ret_s, conv_s, gdn_s = _trunk(
        x_sample.reshape(DB * L, D_MODEL), pos_s, DB, mem_rows(cache_mem_k), mem_rows(cache_mem_v),
        state_ret.astype(F32), cache_diff_k.reshape(DB * P * H_B, DV_B), cache_diff_v.reshape(DB * P * H_B, DV_B),
        state_gdn_conv, state_gdn.astype(F32), p, STEP_CFG)

    shape5 = (DEPTH, B, N_MEM, X_HEADS, X_HD)
    return (y_p.reshape(B, S, D_MODEL), y_s.reshape(DB, L, D_MODEL),
            mk_p.reshape(shape5), mv_p.reshape(shape5),
            dk_p.reshape(B, S, H_B, 2 * DK_B), dv_p.reshape(B, S, H_B, DV_B),
            ret_p, conv_p, gdn_p,
            dk_s.reshape(DB, L, H_B, 2 * DK_B), dv_s.reshape(DB, L, H_B, DV_B),
            ret_s, conv_s, gdn_s)
```

```python
import functools
import math

import jax
import jax.numpy as jnp
from jax import lax
from jax.experimental import pallas as pl
from jax.experimental.pallas import tpu as pltpu

F32 = jnp.float32
BF16 = jnp.bfloat16
I32 = jnp.int32

D_MODEL = 1024
DEPTH = 2
CHUNK = 64
H_A, DK_A, DV_A = 4, 64, 128
RET_THETA = 10000.0
H_B, DK_B = 4, 64
DV_B = 2 * DK_B
ROT_B = DK_B // 4
ROPE_THETA = 500000.0
DIFF_LAMBDA_INIT = 0.8 - 0.6 * math.exp(-0.3 * 0)
H_C, DK_C, DV_C = 8, 128, 128
C_W = H_C * DV_C
CONV_W = 4
N_MEM = 256
X_HEADS = 4
X_HD = D_MODEL // X_HEADS
D_FF = 2816
N_EXPERTS = 8
D_FF_E = 3584
DN_ALPHA = (2 * DEPTH) ** 0.25
LN_EPS = 1e-5
A_QK = H_A * DK_A
A_V = H_A * DV_A
B_QK = H_B * 2 * DK_B
B_V = H_B * DV_B
PROJ0 = 2 * A_QK + 2 * A_V + 2 * B_QK + B_V

VMEM_LIMIT_V7X = 52 * 1024 * 1024
NEG_BIG = -1e30

MOE_TOK_TILE = 512
MOE_SUB = 32
MOE_FF_BLK = 512
MOE_SLOTS = 2 * MOE_TOK_TILE // MOE_SUB + N_EXPERTS
GDN_HEAD_GROUP = 8
CONV_ROWS = 32


def _cparams(sem):
    return pltpu.CompilerParams(dimension_semantics=sem, vmem_limit_bytes=VMEM_LIMIT_V7X)


def _dot(a, b):
    return jnp.dot(a, b, preferred_element_type=F32)


def _dot_nt(a, b):
    return lax.dot_general(a, b, (((1,), (1,)), ((), ())), preferred_element_type=F32)


def _dot_tn(a, b):
    return lax.dot_general(a, b, (((0,), (0,)), ((), ())), preferred_element_type=F32)


def _silu(x):
    return x * (1.0 / (1.0 + jnp.exp(-x)))


def _layer_norm_rows(y, g, b):
    mu = jnp.mean(y, axis=-1, keepdims=True)
    d = y - mu
    var = jnp.mean(d * d, axis=-1, keepdims=True)
    return d * lax.rsqrt(var + LN_EPS) * g + b


def _mm_kernel(x_ref, w_ref, o_ref, *, scale):
    acc = _dot(x_ref[...].astype(BF16), w_ref[0])
    if scale != 1.0:
        acc = acc * scale
    o_ref[0] = acc.astype(o_ref.dtype)


def _matmul(x, w, out_dtype, tm, tn, scale=1.0, name="matmul"):
    M, K = x.shape
    G, _, N = w.shape
    tm = min(tm, M)
    tn = min(tn, N)
    assert M % tm == 0 and N % tn == 0
    return pl.pallas_call(
        functools.partial(_mm_kernel, scale=scale),
        out_shape=jax.ShapeDtypeStruct((G, M, N), out_dtype),
        grid=(G, M // tm, N // tn),
        in_specs=[pl.BlockSpec((tm, K), lambda g, i, j: (i, 0)),
                  pl.BlockSpec((1, K, tn), lambda g, i, j: (g, 0, j))],
        out_specs=pl.BlockSpec((1, tm, tn), lambda g, i, j: (g, i, j)),
        compiler_params=_cparams(("parallel", "parallel", "arbitrary")),
        name=name,
    )(x, w)


def _mm_dn_kernel(*refs, n_in, next_scale):
    xs = refs[:n_in]
    ws = refs[n_in:2 * n_in]
    rest = refs[2 * n_in:]
    if next_scale is None:
        r_ref, g_ref, b_ref, o_ref, ob_ref = rest
    else:
        r_ref, g_ref, b_ref, wn_ref, o_ref, ob_ref = rest
    acc = DN_ALPHA * r_ref[...]
    for x_ref, w_ref in zip(xs, ws):
        acc = acc + _dot(x_ref[...].astype(BF16), w_ref[...])
    y = _layer_norm_rows(acc, g_ref[...], b_ref[...])
    o_ref[...] = y
    if next_scale is None:
        ob_ref[...] = y.astype(BF16)
    else:
        ob_ref[...] = (_dot(y.astype(BF16), wn_ref[...]) * next_scale).astype(BF16)


def _matmul_deepnorm(xs, ws, resid, g, b, tm, w_next=None, next_scale=None, name="matmul_deepnorm"):
    M = resid.shape[0]
    tm = min(tm, M)
    assert M % tm == 0
    n_in = len(xs)
    in_specs = [pl.BlockSpec((tm, x.shape[1]), lambda i: (i, 0)) for x in xs]
    in_specs += [pl.BlockSpec(w.shape, lambda i: (0, 0)) for w in ws]
    in_specs += [pl.BlockSpec((tm, D_MODEL), lambda i: (i, 0)),
                 pl.BlockSpec((1, D_MODEL), lambda i: (0, 0)),
                 pl.BlockSpec((1, D_MODEL), lambda i: (0, 0))]
    args = [*xs, *ws, resid, g.reshape(1, D_MODEL), b.reshape(1, D_MODEL)]
    if w_next is not None:
        in_specs.append(pl.BlockSpec(w_next.shape, lambda i: (0, 0)))
        args.append(w_next)
    return pl.pallas_call(
        functools.partial(_mm_dn_kernel, n_in=n_in, next_scale=next_scale),
        out_shape=[jax.ShapeDtypeStruct((M, D_MODEL), F32), jax.ShapeDtypeStruct((M, D_MODEL), BF16)],
        grid=(M // tm,),
        in_specs=in_specs,
        out_specs=[pl.BlockSpec((tm, D_MODEL), lambda i: (i, 0))] * 2,
        compiler_params=_cparams(("parallel",)),
        name=name,
    )(*args)


def _swap_halves(x, group, half):
    n = x.shape[-1]
    lane = lax.broadcasted_iota(I32, x.shape, x.ndim - 1) % group
    up = pltpu.roll(x, n - half, x.ndim - 1)
    dn = pltpu.roll(x, half, x.ndim - 1)
    return jnp.where(lane < half, up, dn)


def _store_rows(o_ref, y, by_head):
    if not by_head:
        o_ref[...] = y.astype(o_ref.dtype)
        return
    tm = y.shape[0]
    for h in range(H_B):
        o_ref[pl.ds(h, tm, stride=H_B), :] = y[:, h * DV_B:(h + 1) * DV_B].astype(o_ref.dtype)


def _proj0_kernel(*refs, kinds):
    x_ref = refs[0]
    n_out = len(kinds)
    ins, outs = refs[1:len(refs) - n_out], refs[len(refs) - n_out:]
    xb = x_ref[...]
    k = 0
    for (rot, by_head), o_ref in zip(kinds, outs):
        acc = _dot(xb, ins[k][...])
        k += 1
        if rot is not None:
            acc = acc * ins[k][...] + _swap_halves(acc, rot[0], rot[1]) * ins[k + 1][...]
            k += 2
        _store_rows(o_ref, acc, by_head)


def _proj0_call(xb, groups, tm, name):
    T = xb.shape[0]
    tm = min(tm, T)
    for g in groups:
        if g["rot"] is not None:
            tm = min(tm, g["rot"][0].shape[0])
    assert T % tm == 0
    row = lambda i: (i, 0)
    in_specs = [pl.BlockSpec((tm, D_MODEL), row)]
    args = [xb]
    kinds, out_shapes, out_specs = [], [], []
    for g in groups:
        blk = g["w"].shape[1]
        in_specs.append(pl.BlockSpec((D_MODEL, blk), lambda i: (0, 0)))
        args.append(g["w"])
        if g["rot"] is None:
            kinds.append((None, g["by_head"]))
        else:
            cos, sin, lane_group, half = g["rot"]
            assert cos.shape[0] % tm == 0
            npb = cos.shape[0] // tm
            in_specs += [pl.BlockSpec((tm, blk), lambda i, npb=npb: (i % npb, 0))] * 2
            args += [cos, sin]
            kinds.append(((lane_group, half), g["by_head"]))
        if g["by_head"]:
            assert blk == H_B * DV_B
            out_shapes.append(jax.ShapeDtypeStruct((T * H_B, DV_B), g["dtype"]))
            out_specs.append(pl.BlockSpec((tm * H_B, DV_B), row))
        else:
            out_shapes.append(jax.ShapeDtypeStruct((T, blk), g["dtype"]))
            out_specs.append(pl.BlockSpec((tm, blk), row))
    return pl.pallas_call(
        functools.partial(_proj0_kernel, kinds=tuple(kinds)),
        out_shape=out_shapes,
        grid=(T // tm,),
        in_specs=in_specs,
        out_specs=out_specs,
        compiler_params=_cparams(("parallel",)),
        name=name,
    )(*args)


def _rope_tables(pos):
    pos = pos.astype(F32)
    inv = RET_THETA ** (-jnp.arange(0, DK_A, 2, dtype=F32) / DK_A)
    ang = pos[:, None] * inv[None, :]
    c, s = jnp.cos(ang), jnp.sin(ang)
    c64 = jnp.concatenate([c, c], axis=-1)
    s64 = jnp.concatenate([-s, s], axis=-1)
    kscale = DK_A ** -0.5
    ca = jnp.concatenate([jnp.tile(c64, (1, H_A)), jnp.tile(c64, (1, H_A)) * kscale], axis=-1)
    sa = jnp.concatenate([jnp.tile(s64, (1, H_A)), jnp.tile(s64, (1, H_A)) * kscale], axis=-1)
    invb = ROPE_THETA ** (-jnp.arange(0, ROT_B, 2, dtype=F32) / ROT_B)
    angb = pos[:, None] * invb[None, :]
    cb, sb = jnp.cos(angb), jnp.sin(angb)
    rest = DK_B - ROT_B
    c64b = jnp.concatenate([cb, cb, jnp.ones((pos.shape[0], rest), F32)], axis=-1)
    s64b = jnp.concatenate([-sb, sb, jnp.zeros((pos.shape[0], rest), F32)], axis=-1)
    ck = jnp.tile(c64b, (1, 2 * H_B))
    sk = jnp.tile(s64b, (1, 2 * H_B))
    qscale = DK_B ** -0.5
    return ca, sa, ck * qscale, sk * qscale, ck, sk


def _proj0(xb, w_in0_bf, tables, tm):
    ca, sa, cq, sq, ck, sk = tables
    blk = 512
    assert PROJ0 == 6 * blk
    w = [w_in0_bf[:, j * blk:(j + 1) * blk] for j in range(6)]
    rot_a = (DK_A, DK_A // 2)
    rot_b = (DK_B, ROT_B // 2)
    grp = lambda j, dtype, rot=None, by_head=False: dict(w=w[j], dtype=dtype, rot=rot, by_head=by_head)
    qka, va = _proj0_call(xb, [grp(0, BF16, (ca, sa) + rot_a), grp(1, BF16)], tm, "proj0_ret_qkv")
    ga, vb = _proj0_call(xb, [grp(2, F32), grp(5, F32, by_head=True)], tm, "proj0_gate_v")
    qb, kb = _proj0_call(xb, [grp(3, BF16, (cq, sq) + rot_b), grp(4, F32, (ck, sk) + rot_b, by_head=True)],
                         tm, "proj0_diff_qk")
    return qka, va, ga, qb, kb, vb


def _retention_kernel(qk_ref, v_ref, g_ref, dmat_ref, qdec_ref, kdec_ref, sdec_ref, s0_ref, ng_ref,
                      o_ref, sout_ref, s_scr):
    c = pl.program_id(1)

    @pl.when(c == 0)
    def _():
        s_scr[...] = s0_ref[0]

    for h in range(H_A):
        q = qk_ref[:, h * DK_A:(h + 1) * DK_A]
        k = qk_ref[:, A_QK + h * DK_A:A_QK + (h + 1) * DK_A]
        v = v_ref[:, h * DV_A:(h + 1) * DV_A]
        s = s_scr[h]
        scores = _dot_nt(q, k) * dmat_ref[h]
        o = _dot(scores.astype(BF16), v) + _dot(q, s.astype(BF16)) * qdec_ref[h]
        kd = (k.astype(F32) * kdec_ref[h]).astype(BF16)
        s_scr[h] = s * sdec_ref[h] + _dot_tn(kd, v)
        mu = jnp.mean(o, axis=-1, keepdims=True)
        d = o - mu
        var = jnp.mean(d * d, axis=-1, keepdims=True)
        gate = g_ref[:, h * DV_A:(h + 1) * DV_A]
        y = d * lax.rsqrt(var + LN_EPS) * ng_ref[:, h * DV_A:(h + 1) * DV_A] * _silu(gate)
        o_ref[:, h * DV_A:(h + 1) * DV_A] = y.astype(o_ref.dtype)

    @pl.when(c == pl.num_programs(1) - 1)
    def _():
        sout_ref[0] = s_scr[...]


def _retention_tables(lc):
    lg = jnp.log1p(-jnp.exp2(-5.0 - jnp.arange(H_A, dtype=F32)))
    t = jnp.arange(lc, dtype=F32)
    rel = t[:, None] - t[None, :]
    causal = rel >= 0
    dmat = jnp.where(causal, jnp.exp(lg[:, None, None] * jnp.where(causal, rel, 0.0)), 0.0)
    qdec = jnp.exp(lg[:, None] * (t[None, :] + 1.0))
    kdec = jnp.exp(lg[:, None] * (lc - 1.0 - t[None, :]))
    sdec = jnp.exp(lg * lc)
    return (dmat,
            jnp.broadcast_to(qdec[:, :, None], (H_A, lc, DV_A)),
            jnp.broadcast_to(kdec[:, :, None], (H_A, lc, DK_A)),
            jnp.broadcast_to(sdec[:, None, None], (H_A, DK_A, DV_A)))


def _retention(qka, va, ga, state0, ret_norm_g, n_batch, lc):
    T = qka.shape[0]
    nc = T // (n_batch * lc)
    assert nc * n_batch * lc == T
    dmat, qdec, kdec, sdec = _retention_tables(lc)
    row = lambda b, c: (b * nc + c, 0)
    const3 = lambda b, c: (0, 0, 0)
    return pl.pallas_call(
        _retention_kernel,
        out_shape=[jax.ShapeDtypeStruct((T, A_V), BF16),
                   jax.ShapeDtypeStruct((n_batch, H_A, DK_A, DV_A), F32)],
        grid=(n_batch, nc),
        in_specs=[pl.BlockSpec((lc, 2 * A_QK), row),
                  pl.BlockSpec((lc, A_V), row),
                  pl.BlockSpec((lc, A_V), row),
                  pl.BlockSpec((H_A, lc, lc), const3),
                  pl.BlockSpec((H_A, lc, DV_A), const3),
                  pl.BlockSpec((H_A, lc, DK_A), const3),
                  pl.BlockSpec((H_A, DK_A, DV_A), const3),
                  pl.BlockSpec((1, H_A, DK_A, DV_A), lambda b, c: (b, 0, 0, 0)),
                  pl.BlockSpec((1, A_V), lambda b, c: (0, 0))],
        out_specs=[pl.BlockSpec((lc, A_V), row),
                   pl.BlockSpec((1, H_A, DK_A, DV_A), lambda b, c: (b, 0, 0, 0))],
        scratch_shapes=[pltpu.VMEM((H_A, DK_A, DV_A), F32)],
        compiler_params=_cparams(("parallel", "arbitrary")),
        name="retention",
    )(qka, va, ga, dmat, qdec, kdec, sdec, state0, ret_norm_g.reshape(1, A_V))


def _diff_finish(acc1, l1, acc2, l2, lam, ng):
    o = acc1 * (1.0 / l1) - lam * (acc2 * (1.0 / l2))
    ms = jnp.mean(o * o, axis=-1, keepdims=True)
    return o * lax.rsqrt(ms + LN_EPS) * ng * (1.0 - DIFF_LAMBDA_INIT)


def _diff_prompt_kernel(lam_ref, q_ref, k_ref, v_ref, ng_ref, o_ref, kbf, vbf, *, tq):
    qi = pl.program_id(2)

    @pl.when(qi == 0)
    def _():
        h = pl.program_id(1)
        seq = kbf.shape[0]
        kbf[...] = k_ref[pl.ds(h, seq, stride=H_B), :].astype(BF16)
        vbf[...] = v_ref[pl.ds(h, seq, stride=H_B), :].astype(BF16)

    rc = lax.broadcasted_iota(I32, (tq, tq), 0) // CHUNK
    cc = lax.broadcasted_iota(I32, (tq, tq), 1) // CHUNK
    vis = cc <= rc
    nq = kbf.shape[0] // tq

    for j in range(nq):
        @pl.when(qi == j)
        def _(j=j):
            n_full = j * tq
            kd = kbf[n_full:n_full + tq, :]
            vd = vbf[n_full:n_full + tq, :]
            accs = []
            for c in range(2):
                sl = slice(c * DK_B, (c + 1) * DK_B)
                qc = q_ref[:, sl]
                sd = jnp.where(vis, _dot_nt(qc, kd[:, sl]), NEG_BIG)
                m = jnp.max(sd, axis=-1, keepdims=True)
                if j > 0:
                    sf = _dot_nt(qc, kbf[0:n_full, sl])
                    m = jnp.maximum(m, jnp.max(sf, axis=-1, keepdims=True))
                pd = jnp.exp(sd - m)
                l = jnp.sum(pd, axis=-1, keepdims=True)
                acc = _dot(pd.astype(BF16), vd)
                if j > 0:
                    pf = jnp.exp(sf - m)
                    l = l + jnp.sum(pf, axis=-1, keepdims=True)
                    acc = acc + _dot(pf.astype(BF16), vbf[0:n_full, :])
                accs += [acc, l]
            o_ref[...] = _diff_finish(accs[0], accs[1], accs[2], accs[3], lam_ref[0], ng_ref[...]).astype(o_ref.dtype)


def _diff_prompt(qb, kb, vb, lam, diff_norm_g, n_batch, seq, tq):
    T = qb.shape[0]
    assert seq % tq == 0 and tq % CHUNK == 0
    nq = seq // tq
    return pl.pallas_call(
        functools.partial(_diff_prompt_kernel, tq=tq),
        out_shape=jax.ShapeDtypeStruct((T, B_V), BF16),
        grid_spec=pltpu.PrefetchScalarGridSpec(
            num_scalar_prefetch=1,
            grid=(n_batch, H_B, nq),
            in_specs=[pl.BlockSpec((tq, DV_B), lambda b, h, i, lam: (b * nq + i, h)),
                      pl.BlockSpec((seq * H_B, DV_B), lambda b, h, i, lam: (b, 0)),
                      pl.BlockSpec((seq * H_B, DV_B), lambda b, h, i, lam: (b, 0)),
                      pl.BlockSpec((1, DV_B), lambda b, h, i, lam: (0, 0))],
            out_specs=pl.BlockSpec((tq, DV_B), lambda b, h, i, lam: (b * nq + i, h)),
            scratch_shapes=[pltpu.VMEM((seq, DV_B), BF16), pltpu.VMEM((seq, DV_B), BF16)]),
        compiler_params=_cparams(("parallel", "parallel", "arbitrary")),
        name="diff_attn_prompt",
    )(lam, qb, kb, vb, diff_norm_g.reshape(1, DV_B))


def _diff_step_kernel(lam_ref, q_ref, kp_ref, vp_ref, kn_ref, vn_ref, ng_ref, o_ref, *, past, ln):
    qchunk = (past + lax.broadcasted_iota(I32, (ln, 1), 0)) // CHUNK
    vis_p = (lax.broadcasted_iota(I32, (ln, past), 1) // CHUNK) <= qchunk
    vis_n = ((past + lax.broadcasted_iota(I32, (ln, ln), 1)) // CHUNK) <= qchunk
    for h in range(H_B):
        q = q_ref[:, h * DV_B:(h + 1) * DV_B]
        kp = kp_ref[pl.ds(h, past, stride=H_B), :].astype(BF16)
        vp = vp_ref[pl.ds(h, past, stride=H_B), :].astype(BF16)
        kn = kn_ref[pl.ds(h, ln, stride=H_B), :].astype(BF16)
        vn = vn_ref[pl.ds(h, ln, stride=H_B), :].astype(BF16)
        accs = []
        for c in range(2):
            sl = slice(c * DK_B, (c + 1) * DK_B)
            sp = jnp.where(vis_p, _dot_nt(q[:, sl], kp[:, sl]), NEG_BIG)
            sn = jnp.where(vis_n, _dot_nt(q[:, sl], kn[:, sl]), NEG_BIG)
            m = jnp.maximum(jnp.max(sp, axis=-1, keepdims=True), jnp.max(sn, axis=-1, keepdims=True))
            pp = jnp.exp(sp - m)
            pn = jnp.exp(sn - m)
            l = jnp.sum(pp, axis=-1, keepdims=True) + jnp.sum(pn, axis=-1, keepdims=True)
            accs += [_dot(pp.astype(BF16), vp) + _dot(pn.astype(BF16), vn), l]
        y = _diff_finish(accs[0], accs[1], accs[2], accs[3], lam_ref[0], ng_ref[...])
        o_ref[:, h * DV_B:(h + 1) * DV_B] = y.astype(o_ref.dtype)


def _diff_step(qb, kb, vb, past_k, past_v, lam, diff_norm_g, n_batch, ln):
    past = past_k.shape[0] // (n_batch * H_B)
    blk = lambda b, lam: (b, 0)
    return pl.pallas_call(
        functools.partial(_diff_step_kernel, past=past, ln=ln),
        out_shape=jax.ShapeDtypeStruct((n_batch * ln, B_V), BF16),
        grid_spec=pltpu.PrefetchScalarGridSpec(
            num_scalar_prefetch=1,
            grid=(n_batch,),
            in_specs=[pl.BlockSpec((ln, B_QK), blk),
                      pl.BlockSpec((past * H_B, DV_B), blk),
                      pl.BlockSpec((past * H_B, DV_B), blk),
                      pl.BlockSpec((ln * H_B, DV_B), blk),
                      pl.BlockSpec((ln * H_B, DV_B), blk),
                      pl.BlockSpec((1, DV_B), lambda b, lam: (0, 0))],
            out_specs=pl.BlockSpec((ln, B_V), blk)),
        compiler_params=_cparams(("parallel",)),
        name="diff_attn_step",
    )(lam, qb, past_k, past_v, kb, vb, diff_norm_g.reshape(1, DV_B))


X_LANE_BLKS = X_HD // 128


def _xattn_kernel(q_ref, mk_ref, mv_ref, o_ref, *, split):
    stride = X_LANE_BLKS * X_HEADS
    for h in range(X_HEADS):
        if split:
            pieces = [(slice(h * X_HD + j * 128, h * X_HD + (j + 1) * 128), pl.ds(j * X_HEADS + h, N_MEM, stride=stride))
                      for j in range(X_LANE_BLKS)]
            s = sum(_dot_nt(q_ref[:, cs], mk_ref[rs, :].astype(BF16)) for cs, rs in pieces)
        else:
            sl = slice(h * X_HD, (h + 1) * X_HD)
            s = _dot_nt(q_ref[:, sl], mk_ref[:, sl].astype(BF16))
        p = jnp.exp(s - jnp.max(s, axis=-1, keepdims=True))
        inv_l = 1.0 / jnp.sum(p, axis=-1, keepdims=True)
        pb = p.astype(BF16)
        if split:
            for cs, rs in pieces:
                o_ref[:, cs] = (_dot(pb, mv_ref[rs, :].astype(BF16)) * inv_l).astype(o_ref.dtype)
        else:
            o_ref[:, sl] = (_dot(pb, mv_ref[:, sl].astype(BF16)) * inv_l).astype(o_ref.dtype)


def _xattn(q, mk, mv, layer, n_batch, tq, split):
    T = q.shape[0]
    per = T // n_batch
    tq = min(tq, per)
    nt = per // tq
    mem_blk = (None, N_MEM * X_LANE_BLKS * X_HEADS, 128) if split else (None, N_MEM, D_MODEL)
    return pl.pallas_call(
        functools.partial(_xattn_kernel, split=split),
        out_shape=jax.ShapeDtypeStruct((T, D_MODEL), BF16),
        grid=(n_batch, nt),
        in_specs=[pl.BlockSpec((tq, D_MODEL), lambda b, t: (b * nt + t, 0)),
                  pl.BlockSpec(mem_blk, lambda b, t: (layer, b, 0)),
                  pl.BlockSpec(mem_blk, lambda b, t: (layer, b, 0))],
        out_specs=pl.BlockSpec((tq, D_MODEL), lambda b, t: (b * nt + t, 0)),
        compiler_params=_cparams(("parallel", "arbitrary")),
        name="mem_xattn",
    )(q, mk, mv)


def _swiglu_up_kernel(x_ref, wg_ref, wu_ref, o_ref):
    xb = x_ref[...]
    g = _dot(xb, wg_ref[...])
    u = _dot(xb, wu_ref[...])
    o_ref[...] = (_silu(g) * u).astype(o_ref.dtype)


def _swiglu_up(xb, w_gu_bf, tm, tn):
    T = xb.shape[0]
    tm = min(tm, T)
    nj = D_FF // tn
    assert D_FF % tn == 0
    return pl.pallas_call(
        _swiglu_up_kernel,
        out_shape=jax.ShapeDtypeStruct((T, D_FF), BF16),
        grid=(T // tm, nj),
        in_specs=[pl.BlockSpec((tm, D_MODEL), lambda i, j: (i, 0)),
                  pl.BlockSpec((D_MODEL, tn), lambda i, j: (0, j)),
                  pl.BlockSpec((D_MODEL, tn), lambda i, j: (0, nj + j))],
        out_specs=pl.BlockSpec((tm, tn), lambda i, j: (i, j)),
        compiler_params=_cparams(("parallel", "arbitrary")),
        name="swiglu_up",
    )(xb, w_gu_bf, w_gu_bf)


def _conv_kernel(x_ref, prev_ref, st_ref, w_ref, o_ref, buf, *, tm):
    t = pl.program_id(1)
    cb = pl.program_id(2)
    buf[0:8, :] = jnp.where(t == 0, st_ref[0], prev_ref[...])
    buf[8:8 + tm, :] = x_ref[...]
    scale = jnp.where(cb == 0, DK_C ** -0.5, 1.0)
    rc = min(CONV_ROWS, tm)
    for h in range(H_C):
        cs = slice(h * DK_C, (h + 1) * DK_C)
        w = [w_ref[i:i + 1, cs] for i in range(CONV_W)]
        for r0 in range(0, tm, rc):
            y = buf[8 + r0:8 + r0 + rc, cs] * w[CONV_W - 1]
            for i in range(CONV_W - 1):
                y = y + buf[5 + i + r0:5 + i + r0 + rc, cs] * w[i]
            y = _silu(y)
            ss = jnp.sum(y * y, axis=-1, keepdims=True)
            f = jnp.where(cb == 2, 1.0, lax.rsqrt(ss + 1e-6) * scale)
            o_ref[r0:r0 + rc, cs] = (y * f).astype(o_ref.dtype)


def _gdn_conv(proj1, conv_state8, conv_w, n_batch, tm):
    T = proj1.shape[0]
    per = T // n_batch
    tm = min(tm, per)
    nt = per // tm
    assert per % tm == 0 and tm % 8 == 0
    return pl.pallas_call(
        functools.partial(_conv_kernel, tm=tm),
        out_shape=jax.ShapeDtypeStruct((T, 3 * C_W), BF16),
        grid=(n_batch, nt, 3),
        in_specs=[pl.BlockSpec((tm, C_W), lambda b, t, c: (b * nt + t, c)),
                  pl.BlockSpec((8, C_W), lambda b, t, c: (jnp.maximum((b * nt + t) * (tm // 8) - 1, 0), c)),
                  pl.BlockSpec((1, 8, C_W), lambda b, t, c: (b, 0, c)),
                  pl.BlockSpec((CONV_W, C_W), lambda b, t, c: (0, c))],
        out_specs=pl.BlockSpec((tm, C_W), lambda b, t, c: (b * nt + t, c)),
        scratch_shapes=[pltpu.VMEM((8 + tm, C_W), F32)],
        compiler_params=_cparams(("parallel", "arbitrary", "arbitrary")),
        name="gdn_conv",
    )(proj1, proj1, conv_state8, conv_w)


def _gdn_kernel(q_ref, k_ref, v_ref, z_ref, ab_ref, alog_ref, dtb_ref, ng_ref, s0_ref,
                o_ref, sout_ref, s_scr, *, lc, l_real):
    c = pl.program_id(1)

    @pl.when(c == 0)
    def _():
        s_scr[...] = s0_ref[0]

    ri = lax.broadcasted_iota(I32, (lc, lc), 0)
    ci = lax.broadcasted_iota(I32, (lc, lc), 1)
    incl = ci <= ri
    strict = ci < ri
    eye = (ci == ri).astype(F32)
    ab = ab_ref[...]
    sp = jnp.maximum(ab + dtb_ref[...], 0.0) + jnp.log1p(jnp.exp(-jnp.abs(ab + dtb_ref[...])))
    glog = -jnp.exp(alog_ref[...]) * sp
    if l_real < lc:
        live = lax.broadcasted_iota(I32, (lc, 1), 0) < l_real
        glog = jnp.where(live, glog, 0.0)
    gcum = jnp.dot(incl.astype(F32), glog, preferred_element_type=F32, precision=lax.Precision.HIGHEST)
    gcum_t = gcum.T
    beta_all = 1.0 / (1.0 + jnp.exp(-ab))

    levels = []
    s = 1
    while s < lc:
        levels.append(((ri // (2 * s)) == (ci // (2 * s))) & (((ri // s) % 2) == 1) & (((ci // s) % 2) == 0))
        s *= 2

    group = GDN_HEAD_GROUP if lc > 128 else H_C
    for h0 in range(0, H_C, group):
        heads = range(h0, h0 + group)
        sls = [slice(h * DK_C, (h + 1) * DK_C) for h in heads]
        ks, kfs, vs, gcols, decays, egs, kbs, a_lows = [], [], [], [], [], [], [], []
        for h, sl in zip(heads, sls):
            k = k_ref[:, sl]
            v = v_ref[:, sl].astype(F32)
            if l_real < lc:
                k = jnp.where(live, k, jnp.zeros_like(k))
                v = jnp.where(live, v, 0.0)
            kf = k.astype(F32)
            gcol = gcum[:, h:h + 1]
            grow = gcum_t[h:h + 1, :]
            beta = beta_all[:, H_C + h:H_C + h + 1]
            decay = jnp.exp(jnp.where(incl, gcol - grow, NEG_BIG))
            kb = kf * beta
            ks.append(k)
            kfs.append(kf)
            gcols.append(gcol)
            decays.append(decay)
            egs.append(jnp.exp(gcol))
            kbs.append(kb)
            vs.append(v * beta)
            a_lows.append(jnp.where(strict, _dot_nt(kb.astype(BF16), k) * decay, 0.0))
        tinvs = [eye - jnp.where(levels[0], a, 0.0) for a in a_lows]
        for lvl, off in enumerate(levels[1:], start=1):
            s = 2 ** lvl
            tbs = [t.astype(BF16) for t in tinvs]
            if s < 8:
                a_offs = [jnp.where(off, a, 0.0).astype(BF16) for a in a_lows]
                ws = [_dot(a, t).astype(BF16) for a, t in zip(a_offs, tbs)]
                tinvs = [t - _dot(tb, w) for t, tb, w in zip(tinvs, tbs, ws)]
            else:
                odd = [slice((2 * j + 1) * s, (2 * j + 2) * s) for j in range(lc // (2 * s))]
                even = [slice(2 * j * s, (2 * j + 1) * s) for j in range(lc // (2 * s))]
                take = lambda x: jnp.concatenate([x[sl, :] for sl in odd], axis=0)
                rh = lax.broadcasted_iota(I32, (lc // 2, lc), 0)
                ch = lax.broadcasted_iota(I32, (lc // 2, lc), 1)
                off_odd = (ch // s) == 2 * (rh // s)
                zero = jnp.zeros((s, lc), F32)
                new = []
                for a, t, tb in zip(a_lows, tinvs, tbs):
                    w_odd = _dot(jnp.where(off_odd, take(a), 0.0).astype(BF16), tb)
                    w_full = jnp.concatenate(
                        [piece for j in range(len(odd)) for piece in (zero, w_odd[j * s:(j + 1) * s, :])], axis=0)
                    t_odd = take(t)
                    t_odd = t_odd - _dot(t_odd.astype(BF16), w_full.astype(BF16))
                    new.append(jnp.concatenate(
                        [piece for j, ev in enumerate(even) for piece in (t[ev, :], t_odd[j * s:(j + 1) * s, :])],
                        axis=0))
                tinvs = new
        rhss = [jnp.concatenate([vb, kb * eg], axis=-1).astype(BF16) for vb, kb, eg in zip(vs, kbs, egs)]
        sols = [_dot(t.astype(BF16), r) for t, r in zip(tinvs, rhss)]
        sts = [s_scr[h] for h in heads]
        stbs = [st.astype(BF16) for st in sts]
        ubs = [(sol[:, :DV_C] - _dot(sol[:, DV_C:].astype(BF16), stb)).astype(BF16) for sol, stb in zip(sols, stbs)]
        qs = [q_ref[:, sl] for sl in sls]
        qks = [(_dot_nt(q, k) * decay).astype(BF16) for q, k, decay in zip(qs, ks, decays)]
        os_ = [_dot((q.astype(F32) * eg).astype(BF16), stb) + _dot(qk, ub)
               for q, eg, stb, qk, ub in zip(qs, egs, stbs, qks, ubs)]
        for h, st, kf, gcol, ub in zip(heads, sts, kfs, gcols, ubs):
            g_last = gcol[lc - 1:lc, :]
            kdec = (kf * jnp.exp(g_last - gcol)).astype(BF16)
            s_scr[h] = st * jnp.exp(g_last) + _dot_tn(kdec, ub)
        for sl, o in zip(sls, os_):
            ms = jnp.mean(o * o, axis=-1, keepdims=True)
            y = o * lax.rsqrt(ms + LN_EPS) * ng_ref[...] * _silu(z_ref[:, sl])
            o_ref[:, sl] = y.astype(o_ref.dtype)

    @pl.when(c == pl.num_programs(1) - 1)
    def _():
        sout_ref[0] = s_scr[...]


def _gdn(qkvn, proj1, ab, a_log, dt_bias, gdn_norm_g, state0, n_batch, lc, l_real):
    T = qkvn.shape[0]
    nc = T // (n_batch * lc)
    assert nc * n_batch * lc == T and (nc == 1 or l_real == lc)
    pad = lambda v: jnp.pad(v.astype(F32).reshape(1, H_C), ((0, 0), (0, 128 - H_C)))
    return pl.pallas_call(
        functools.partial(_gdn_kernel, lc=lc, l_real=l_real),
        out_shape=[jax.ShapeDtypeStruct((T, C_W), BF16),
                   jax.ShapeDtypeStruct((n_batch, H_C, DK_C, DV_C), F32)],
        grid=(n_batch, nc),
        in_specs=[pl.BlockSpec((lc, C_W), lambda b, c: (b * nc + c, 0)),
                  pl.BlockSpec((lc, C_W), lambda b, c: (b * nc + c, 1)),
                  pl.BlockSpec((lc, C_W), lambda b, c: (b * nc + c, 2)),
                  pl.BlockSpec((lc, C_W), lambda b, c: (b * nc + c, 3)),
                  pl.BlockSpec((lc, 128), lambda b, c: (b * nc + c, 0)),
                  pl.BlockSpec((1, 128), lambda b, c: (0, 0)),
                  pl.BlockSpec((1, 128), lambda b, c: (0, 0)),
                  pl.BlockSpec((1, DV_C), lambda b, c: (0, 0)),
                  pl.BlockSpec((1, H_C, DK_C, DV_C), lambda b, c: (b, 0, 0, 0))],
        out_specs=[pl.BlockSpec((lc, C_W), lambda b, c: (b * nc + c, 0)),
                   pl.BlockSpec((1, H_C, DK_C, DV_C), lambda b, c: (b, 0, 0, 0))],
        scratch_shapes=[pltpu.VMEM((H_C, DK_C, DV_C), F32)],
        compiler_params=_cparams(("parallel", "arbitrary")),
        name="gated_delta",
    )(qkvn, qkvn, qkvn, proj1, ab, pad(a_log), pad(dt_bias), gdn_norm_g.reshape(1, DV_C), state0)


def _router_kernel(x_ref, wt_ref, rt_ref, rtt_ref, cnt_ref):
    tt = x_ref.shape[0]
    logits = lax.dot_general(wt_ref[...], x_ref[...], (((1,), (1,)), ((), ())),
                             preferred_element_type=F32, precision=lax.Precision.HIGHEST)
    eid = lax.broadcasted_iota(I32, (N_EXPERTS, tt), 0).astype(F32)
    m1 = jnp.max(logits, axis=0, keepdims=True)
    e1 = jnp.min(jnp.where(logits == m1, eid, float(N_EXPERTS)), axis=0, keepdims=True)
    rest = jnp.where(eid == e1, -jnp.inf, logits)
    m2 = jnp.max(rest, axis=0, keepdims=True)
    e2 = jnp.min(jnp.where(rest == m2, eid, float(N_EXPERTS)), axis=0, keepdims=True)
    ev = jnp.exp(m2 - m1)
    g1 = 1.0 / (1.0 + ev)
    g2 = ev / (1.0 + ev)
    hit1 = eid == e1
    hit2 = eid == e2
    member = jnp.where(hit1 | hit2, 1.0, 0.0)
    before = (lax.broadcasted_iota(I32, (tt, tt), 0) < lax.broadcasted_iota(I32, (tt, tt), 1))
    rank = _dot(member.astype(BF16), jnp.where(before, 1.0, 0.0).astype(BF16))
    r1 = jnp.sum(jnp.where(hit1, rank, 0.0), axis=0, keepdims=True)
    r2 = jnp.sum(jnp.where(hit2, rank, 0.0), axis=0, keepdims=True)
    rows = jnp.concatenate([e1, e2, r1, r2, g1, g2, jnp.zeros((2, tt), F32)], axis=0)
    rt_ref[0] = rows
    rtt_ref[0] = jnp.concatenate([rows, jnp.zeros((128 - 8, tt), F32)], axis=0).T
    cnt = jnp.sum(member, axis=1, keepdims=True).astype(I32)
    cnt_ref[0] = jnp.broadcast_to(cnt, (N_EXPERTS, 128))


def _router(x, w_router):
    T = x.shape[0]
    tt = MOE_TOK_TILE
    nt = T // tt
    assert T % tt == 0
    return pl.pallas_call(
        _router_kernel,
        out_shape=[jax.ShapeDtypeStruct((nt, 8, tt), F32),
                   jax.ShapeDtypeStruct((nt, tt, 128), F32),
                   jax.ShapeDtypeStruct((nt, N_EXPERTS, 128), I32)],
        grid=(nt,),
        in_specs=[pl.BlockSpec((tt, D_MODEL), lambda i: (i, 0)),
                  pl.BlockSpec((N_EXPERTS, D_MODEL), lambda i: (0, 0))],
        out_specs=[pl.BlockSpec((1, 8, tt), lambda i: (i, 0, 0)),
                   pl.BlockSpec((1, tt, 128), lambda i: (i, 0, 0)),
                   pl.BlockSpec((1, N_EXPERTS, 128), lambda i: (i, 0, 0))],
        compiler_params=_cparams(("parallel",)),
        name="moe_router",
    )(x, w_router.T)


def _moe_tables(cnt, n_tiles, group):
    E = N_EXPERTS
    nb_max = (2 * n_tiles * MOE_TOK_TILE) // MOE_SUB + n_tiles * E
    nb_tot = -(-(nb_max + E * (group - 1)) // group) * group
    nblk = (cnt + MOE_SUB - 1) // MOE_SUB
    nbe = jnp.sum(nblk, axis=0)
    nbe_pad = (nbe + group - 1) // group * group
    ends_e = jnp.cumsum(nbe_pad)
    base = ends_e - nbe_pad
    dest0 = base[None, :] + jnp.cumsum(nblk, axis=0) - nblk
    ends_t = jnp.cumsum(nblk, axis=1)
    kk = jnp.arange(MOE_SLOTS, dtype=I32)
    slot_e = jnp.minimum(jnp.sum(ends_t[:, None, :] <= kk[None, :, None], axis=-1), E - 1).astype(I32)
    slot_valid = kk[None, :] < ends_t[:, -1:]
    start_t = jnp.take_along_axis(ends_t - nblk, slot_e, axis=1)
    slot_j = jnp.where(slot_valid, kk[None, :] - start_t, 0)
    slot_dest = jnp.where(slot_valid, jnp.take_along_axis(dest0, slot_e, axis=1) + slot_j, 0)
    npad = nbe_pad - nbe
    pm = jnp.arange(E * (group - 1), dtype=I32)
    pe, pt = pm // max(group - 1, 1), pm % max(group - 1, 1)
    pad_valid = pt < npad[pe]
    pad_dest = jnp.where(pad_valid, base[pe] + nbe[pe] + pt, 0)
    n_rt = nb_tot // group
    r0 = jnp.arange(n_rt, dtype=I32) * group
    rt_valid = r0 < ends_e[-1]
    rt_e = jnp.minimum(jnp.sum(ends_e[None, :] <= r0[:, None], axis=-1), E - 1)
    last_e = jnp.minimum(jnp.sum(ends_e <= ends_e[-1] - 1), E - 1)
    rt_e = jnp.where(rt_valid, rt_e, last_e)
    i32 = lambda a: a.astype(I32).reshape(-1)
    return dict(nb_tot=nb_tot, n_rt=n_rt, slot_e=i32(slot_e), slot_j=i32(slot_j), slot_dest=i32(slot_dest),
                slot_valid=i32(slot_valid), pad_dest=i32(pad_dest), pad_valid=i32(pad_valid),
                rt_e=i32(rt_e), rt_valid=i32(rt_valid), n_valid_rt=i32(ends_e[-1] // group))


def _moe_gather_kernel(se_ref, sj_ref, sd_ref, sv_ref, pd_ref, pv_ref, nv_ref, x_ref, rt_ref, xg_hbm,
                       buf, zbuf, sem, zsem, *, n_pad, n_rt, rows):
    i = pl.program_id(0)
    rt = rt_ref[0]
    height = MOE_SLOTS * MOE_SUB
    row = lax.broadcasted_iota(I32, (height, 1), 0)
    row_e = jnp.full((height, 1), -1.0, F32)
    row_r = (row % MOE_SUB).astype(F32)
    for k in range(MOE_SLOTS):
        idx = i * MOE_SLOTS + k
        here = (row // MOE_SUB) == k
        row_e = jnp.where(here, jnp.where(sv_ref[idx] == 1, se_ref[idx], -1).astype(F32), row_e)
        row_r = row_r + jnp.where(here, (sj_ref[idx] * MOE_SUB).astype(F32), 0.0)
    sel = ((rt[0:1, :] == row_e) & (rt[2:3, :] == row_r)) | ((rt[1:2, :] == row_e) & (rt[3:4, :] == row_r))
    buf[...] = _dot(jnp.where(sel, 1.0, 0.0).astype(BF16), x_ref[...]).astype(BF16)

    def slot_copy(k):
        dst = pl.multiple_of(sd_ref[i * MOE_SLOTS + k] * MOE_SUB, MOE_SUB)
        return pltpu.make_async_copy(buf.at[pl.ds(k * MOE_SUB, MOE_SUB)], xg_hbm.at[pl.ds(dst, MOE_SUB)], sem.at[k])

    for k in range(MOE_SLOTS):
        @pl.when(sv_ref[i * MOE_SLOTS + k] == 1)
        def _(k=k):
            slot_copy(k).start()

    for k in range(MOE_SLOTS):
        @pl.when(sv_ref[i * MOE_SLOTS + k] == 1)
        def _(k=k):
            slot_copy(k).wait()

    def pad_copy(m):
        dst = pl.multiple_of(pd_ref[m] * MOE_SUB, MOE_SUB)
        return pltpu.make_async_copy(zbuf.at[pl.ds(0, MOE_SUB)], xg_hbm.at[pl.ds(dst, MOE_SUB)], zsem.at[0])

    def tail_copy(r):
        dst = pl.multiple_of(r * rows, rows)
        return pltpu.make_async_copy(zbuf, xg_hbm.at[pl.ds(dst, rows)], zsem.at[1])

    @pl.when(i == pl.num_programs(0) - 1)
    def _():
        zbuf[...] = jnp.zeros_like(zbuf)

        def pad_start(m, carry):
            @pl.when(pv_ref[m] == 1)
            def _():
                pad_copy(m).start()
            return carry

        def pad_wait(m, carry):
            @pl.when(pv_ref[m] == 1)
            def _():
                pad_copy(m).wait()
            return carry

        def tail_start(r, carry):
            tail_copy(r).start()
            return carry

        def tail_wait(r, carry):
            tail_copy(r).wait()
            return carry

        lax.fori_loop(0, n_pad, pad_start, 0)
        lax.fori_loop(nv_ref[0], n_rt, tail_start, 0)
        lax.fori_loop(0, n_pad, pad_wait, 0)
        lax.fori_loop(nv_ref[0], n_rt, tail_wait, 0)


def _moe_gather(xb, rt, tb, rows):
    nb = tb["nb_tot"]
    tt = MOE_TOK_TILE
    nt = xb.shape[0] // tt
    n_pad = tb["pad_dest"].shape[0]
    return pl.pallas_call(
        functools.partial(_moe_gather_kernel, n_pad=n_pad, n_rt=tb["n_rt"], rows=rows),
        out_shape=jax.ShapeDtypeStruct((nb * MOE_SUB, D_MODEL), BF16),
        grid_spec=pltpu.PrefetchScalarGridSpec(
            num_scalar_prefetch=7,
            grid=(nt,),
            in_specs=[pl.BlockSpec((tt, D_MODEL), lambda i, *_: (i, 0)),
                      pl.BlockSpec((1, 8, tt), lambda i, *_: (i, 0, 0))],
            out_specs=pl.BlockSpec(memory_space=pl.ANY),
            scratch_shapes=[pltpu.VMEM((MOE_SLOTS * MOE_SUB, D_MODEL), BF16),
                            pltpu.VMEM((rows, D_MODEL), BF16),
                            pltpu.SemaphoreType.DMA((MOE_SLOTS,)),
                            pltpu.SemaphoreType.DMA((2,))]),
        compiler_params=_cparams(("arbitrary",)),
        name="moe_gather",
    )(tb["slot_e"], tb["slot_j"], tb["slot_dest"], tb["slot_valid"], tb["pad_dest"], tb["pad_valid"],
      tb["n_valid_rt"], xb, rt)


def _moe_ffn_kernel(e_ref, valid_ref, x_ref, wg_ref, wu_ref, wd_ref, y_ref, acc, *, rows):
    r = pl.program_id(0)
    f = pl.program_id(1)
    nf = pl.num_programs(1)
    ok = valid_ref[r] == 1

    def partial_out():
        xb = x_ref[...]
        g = _dot(xb, wg_ref[0].astype(BF16))
        u = _dot(xb, wu_ref[0].astype(BF16))
        return _dot((_silu(g) * u).astype(BF16), wd_ref[0].astype(BF16))

    @pl.when(jnp.logical_and(ok, f == 0))
    def _():
        acc[...] = partial_out()

    @pl.when(jnp.logical_and(ok, jnp.logical_and(f > 0, f < nf - 1)))
    def _():
        acc[...] += partial_out()

    @pl.when(jnp.logical_and(ok, f == nf - 1))
    def _():
        y_ref[...] = (acc[...] + partial_out()).astype(y_ref.dtype)

    @pl.when(jnp.logical_and(jnp.logical_not(ok), f == nf - 1))
    def _():
        y_ref[...] = jnp.zeros_like(y_ref)


def _moe_ffn(xg, w_gu_bf, w_d_bf, tb, rows):
    n_rt = tb["n_rt"]
    nf = D_FF_E // MOE_FF_BLK
    fb = MOE_FF_BLK

    def fsel(r, f, v):
        return jnp.where(v[r] == 1, f, nf - 1)

    return pl.pallas_call(
        functools.partial(_moe_ffn_kernel, rows=rows),
        out_shape=jax.ShapeDtypeStruct(xg.shape, BF16),
        grid_spec=pltpu.PrefetchScalarGridSpec(
            num_scalar_prefetch=2,
            grid=(n_rt, nf),
            in_specs=[pl.BlockSpec((rows, D_MODEL), lambda r, f, e, v: (r, 0)),
                      pl.BlockSpec((1, D_MODEL, fb), lambda r, f, e, v: (e[r], 0, fsel(r, f, v))),
                      pl.BlockSpec((1, D_MODEL, fb), lambda r, f, e, v: (e[r], 0, nf + fsel(r, f, v))),
                      pl.BlockSpec((1, fb, D_MODEL), lambda r, f, e, v: (e[r], fsel(r, f, v), 0))],
            out_specs=pl.BlockSpec((rows, D_MODEL), lambda r, f, e, v: (r, 0)),
            scratch_shapes=[pltpu.VMEM((rows, D_MODEL), F32)]),
        compiler_params=_cparams(("arbitrary", "arbitrary")),
        name="moe_grouped_ffn",
    )(tb["rt_e"], tb["rt_valid"], xg, w_gu_bf, w_gu_bf, w_d_bf)


def _moe_combine_kernel(se_ref, sj_ref, sd_ref, sv_ref, *refs):
    y_refs = refs[:MOE_SLOTS]
    rtt_ref, x_ref, g_ref, b_ref, o_ref, ybuf = refs[MOE_SLOTS:]
    i = pl.program_id(0)
    width = MOE_SLOTS * MOE_SUB
    lane = lax.broadcasted_iota(I32, (1, width), 1)
    lane_e = jnp.full((1, width), -1.0, F32)
    lane_r = (lane % MOE_SUB).astype(F32)
    for k in range(MOE_SLOTS):
        idx = i * MOE_SLOTS + k
        valid = sv_ref[idx] == 1
        yk = y_refs[k][...]
        ybuf[k * MOE_SUB:(k + 1) * MOE_SUB, :] = jnp.where(valid, yk, jnp.zeros_like(yk))
        here = (lane // MOE_SUB) == k
        lane_e = jnp.where(here, jnp.where(valid, se_ref[idx], -1).astype(F32), lane_e)
        lane_r = lane_r + jnp.where(here, (sj_ref[idx] * MOE_SUB).astype(F32), 0.0)
    rtt = rtt_ref[0]
    a1 = jnp.where((rtt[:, 0:1] == lane_e) & (rtt[:, 2:3] == lane_r), 1.0, 0.0).astype(BF16)
    a2 = jnp.where((rtt[:, 1:2] == lane_e) & (rtt[:, 3:4] == lane_r), 1.0, 0.0).astype(BF16)
    yb = ybuf[...]
    ff = rtt[:, 4:5] * _dot(a1, yb) + rtt[:, 5:6] * _dot(a2, yb)
    o_ref[...] = _layer_norm_rows(DN_ALPHA * x_ref[...] + ff, g_ref[...], b_ref[...])


def _moe_combine(yg, rtt, x, g, b, tb):
    tt = MOE_TOK_TILE
    T = x.shape[0]

    def y_map(i, se, sj, sd, sv, *, k):
        return (sd[i * MOE_SLOTS + k], 0)

    y_specs = [pl.BlockSpec((MOE_SUB, D_MODEL), functools.partial(y_map, k=k)) for k in range(MOE_SLOTS)]
    return pl.pallas_call(
        _moe_combine_kernel,
        out_shape=jax.ShapeDtypeStruct((T, D_MODEL), F32),
        grid_spec=pltpu.PrefetchScalarGridSpec(
            num_scalar_prefetch=4,
            grid=(T // tt,),
            in_specs=y_specs + [pl.BlockSpec((1, tt, 128), lambda i, *_: (i, 0, 0)),
                                pl.BlockSpec((tt, D_MODEL), lambda i, *_: (i, 0)),
                                pl.BlockSpec((1, D_MODEL), lambda i, *_: (0, 0)),
                                pl.BlockSpec((1, D_MODEL), lambda i, *_: (0, 0))],
            out_specs=pl.BlockSpec((tt, D_MODEL), lambda i, *_: (i, 0)),
            scratch_shapes=[pltpu.VMEM((MOE_SLOTS * MOE_SUB, D_MODEL), BF16)]),
        compiler_params=_cparams(("arbitrary",)),
        name="moe_combine",
    )(tb["slot_e"], tb["slot_j"], tb["slot_dest"], tb["slot_valid"], *([yg] * MOE_SLOTS), rtt, x,
      g.reshape(1, D_MODEL), b.reshape(1, D_MODEL))


def _moe(x, xb, w_router, w_gu_bf, w_d_bf, g, b, rows):
    T = x.shape[0]
    nt = T // MOE_TOK_TILE
    rt, rtt, cnt = _router(x, w_router)
    tb = _moe_tables(cnt[:, :, 0], nt, rows // MOE_SUB)
    xg = _moe_gather(xb, rt, tb, rows)
    yg = _moe_ffn(xg, w_gu_bf, w_d_bf, tb, rows)
    return _moe_combine(yg, rtt, x, g, b, tb)


def _trunk(x, pos_rows, n_batch, mem_k, mem_v, ret_state, past_k, past_v, conv_state, gdn_state, p, cfg):
    T = x.shape[0]
    per = T // n_batch
    tm = cfg["tm"]
    tm_mm = cfg["tm_mm"]
    ln_g, ln_b = p["ln_g"], p["ln_b"]

    qka, va, ga, qb, kb, vb = _proj0(x.astype(BF16), p["w_in0"], _rope_tables(pos_rows), tm_mm)
    oa, ret_new = _retention(qka, va, ga, ret_state, p["ret_norm_g"], n_batch, cfg["ret_lc"])
    lp = p["diff_lambda"].astype(F32)
    lam = (jnp.exp(jnp.sum(lp[0] * lp[1])) - jnp.exp(jnp.sum(lp[2] * lp[3])) + DIFF_LAMBDA_INIT).reshape(1)
    if past_k is None:
        ob = _diff_prompt(qb, kb, vb, lam, p["diff_norm_g"], n_batch, per, cfg["diff_tq"])
    else:
        ob = _diff_step(qb, kb, vb, past_k, past_v, lam, p["diff_norm_g"], n_batch, per)
    tm_dn = cfg["tm_dn"]
    x, q = _matmul_deepnorm([oa, ob], [p["w_out0"][:A_V], p["w_out0"][A_V:]], x, ln_g[0, 0], ln_b[0, 0], tm_dn,
                            w_next=p["w_xq"][0], next_scale=X_HD ** -0.5, name="out0_deepnorm_xq")
    xa = _xattn(q, mem_k, mem_v, 0, n_batch, cfg["x_tq"], cfg["mem_split"])
    x, xb = _matmul_deepnorm([xa], [p["w_xo"][0]], x, ln_g[0, 1], ln_b[0, 1], tm_dn, name="xo0_deepnorm")
    hmid = _swiglu_up(xb, p["w_ffn_gu"], cfg["ffn_tm"], cfg["ffn_tn"])
    x, xb = _matmul_deepnorm([hmid], [p["w_ffn_d"]], x, ln_g[0, 2], ln_b[0, 2], tm, name="ffn_down_deepnorm")

    proj1 = _matmul(xb, p["w_in1_main"][None], F32, tm_mm, 1024, name="proj1")[0]
    ab = _matmul(xb, p["w_in1_ab"][None], F32, tm_mm, 128, name="proj1_gates")[0]
    qkv3 = proj1.reshape(n_batch, per, 4 * C_W)[:, :, :3 * C_W]
    conv_new = qkv3[:, per - (CONV_W - 1):, :]
    state8 = jnp.pad(conv_state.astype(F32), ((0, 0), (8 - (CONV_W - 1), 0), (0, 0)))
    qkvn = _gdn_conv(proj1, state8, p["conv_w"], n_batch, cfg["conv_tm"])
    lc = cfg["gdn_lc"]
    if per < lc:
        padrows = lambda a: jnp.pad(a.reshape(n_batch, per, a.shape[-1]),
                                    ((0, 0), (0, lc - per), (0, 0))).reshape(n_batch * lc, a.shape[-1])
        og, gdn_new = _gdn(padrows(qkvn), padrows(proj1), padrows(ab), p["a_log"], p["dt_bias"],
                           p["gdn_norm_g"], gdn_state, n_batch, lc, per)
        og = og.reshape(n_batch, lc, C_W)[:, :per].reshape(T, C_W)
    else:
        og, gdn_new = _gdn(qkvn, proj1, ab, p["a_log"], p["dt_bias"], p["gdn_norm_g"], gdn_state,
                           n_batch, lc, lc)
    x, q = _matmul_deepnorm([og], [p["w_out1"]], x, ln_g[1, 0], ln_b[1, 0], tm_dn,
                            w_next=p["w_xq"][1], next_scale=X_HD ** -0.5, name="out1_deepnorm_xq")
    xa = _xattn(q, mem_k, mem_v, 1, n_batch, cfg["x_tq"], cfg["mem_split"])
    x, xb = _matmul_deepnorm([xa], [p["w_xo"][1]], x, ln_g[1, 1], ln_b[1, 1], tm_dn, name="xo1_deepnorm")
    x = _moe(x, xb, p["w_router"], p["w_moe_gu"], p["w_moe_d"], ln_g[1, 2], ln_b[1, 2], cfg["moe_rows"])
    return x, kb, vb, ret_new, conv_new, gdn_new


PROMPT_CFG = dict(tm=512, tm_mm=1024, tm_dn=1024, ret_lc=256, diff_tq=512, x_tq=512, ffn_tm=256, ffn_tn=2816, conv_tm=512,
                  gdn_lc=256, moe_rows=1024, mem_split=False)
STEP_CFG = dict(tm=512, tm_mm=512, tm_dn=512, ret_lc=16, diff_tq=16, x_tq=16, ffn_tm=256, ffn_tn=2816, conv_tm=16,
                gdn_lc=64, moe_rows=256, mem_split=True)


def kernel(x_prompt, x_sample, cache_diff_k, cache_diff_v, state_ret, state_gdn_conv, state_gdn, cache_mem_k, cache_mem_v, mem_prompt, w_in0, ret_norm_g, diff_lambda, diff_norm_g, w_out0, w_in1, conv_w, a_log, dt_bias, gdn_norm_g, w_out1, w_xq, w_xkv, w_xo, w_ffn_gu, w_ffn_d, w_router, w_moe_gu, w_moe_d, ln_g, ln_b):
    B, S, _ = x_prompt.shape
    DB, L, _ = x_sample.shape
    P = cache_diff_k.shape[1]
    bf = lambda w: w.astype(BF16)
    w_ab = jnp.pad(w_in1[:, 4 * C_W:], ((0, 0), (0, 128 - 2 * H_C)))
    p = dict(w_in0=bf(w_in0), ret_norm_g=ret_norm_g, diff_lambda=diff_lambda, diff_norm_g=diff_norm_g,
             w_out0=bf(w_out0), w_in1_main=bf(w_in1[:, :4 * C_W]), w_in1_ab=bf(w_ab), conv_w=conv_w,
             a_log=a_log, dt_bias=dt_bias, gdn_norm_g=gdn_norm_g, w_out1=bf(w_out1), w_xq=bf(w_xq),
             w_xo=bf(w_xo), w_ffn_gu=bf(w_ffn_gu), w_ffn_d=bf(w_ffn_d), w_router=w_router,
             w_moe_gu=w_moe_gu, w_moe_d=w_moe_d, ln_g=ln_g, ln_b=ln_b)

    mem = mem_prompt.reshape(B * N_MEM, D_MODEL)
    mk_p = _matmul(mem, bf(w_xkv[:, :, :D_MODEL]), F32, 1024, 1024, name="mem_k")
    mv_p = _matmul(mem, bf(w_xkv[:, :, D_MODEL:]), F32, 1024, 1024, name="mem_v")

    y_p, dk_p, dv_p, ret_p, conv_p, gdn_p = _trunk(
        x_prompt.reshape(B * S, D_MODEL), jnp.arange(S), B, mk_p, mv_p,
        jnp.zeros((B, H_A, DK_A, DV_A), F32), None, None,
        jnp.zeros((B, CONV_W - 1, 3 * C_W), F32), jnp.zeros((B, H_C, DK_C, DV_C), F32), p, PROMPT_CFG)

    pos_s = jnp.tile(P + jnp.arange(L), DB)

    def mem_rows(c):
        c = c.reshape(DEPTH, DB * N_MEM, X_HEADS, X_LANE_BLKS, 128).transpose(0, 1, 3, 2, 4)
        return c.reshape(DEPTH, DB * N_MEM * X_LANE_BLKS * X_HEADS, 128)

    y_s, dk_s, dv_s, ret_s, conv_s, gdn_s = _trunk(
        x_sample.reshape(DB * L, D_MODEL), pos_s, DB, mem_rows(cache_mem_k), mem_rows(cache_mem_v),
        state_ret.astype(F32), cache_diff_k.reshape(DB * P * H_B, DV_B), cache_diff_v.reshape(DB * P * H_B, DV_B),
        state_gdn_conv, state_gdn.astype(F32), p, STEP_CFG)

    shape5 = (DEPTH, B, N_MEM, X_HEADS, X_HD)
    return (y_p.reshape(B, S, D_MODEL), y_s.reshape(DB, L, D_MODEL),
            mk_p.reshape(shape5), mv_p.reshape(shape5),
            dk_p.reshape(B, S, H_B, 2 * DK_B), dv_p.reshape(B, S, H_B, DV_B),
            ret_p, conv_p, gdn_p,
            dk_s.reshape(DB, L, H_B, 2 * DK_B), dv_s.reshape(DB, L, H_B, DV_B),
            ret_s, conv_s, gdn_s)
```

```python
import functools
import math

import jax
import jax.numpy as jnp
from jax import lax
from jax.experimental import pallas as pl
from jax.experimental.pallas import tpu as pltpu

F32 = jnp.float32
BF16 = jnp.bfloat16
I32 = jnp.int32

D_MODEL = 1024
DEPTH = 2
CHUNK = 64
H_A, DK_A, DV_A = 4, 64, 128
RET_THETA = 10000.0
H_B, DK_B = 4, 64
DV_B = 2 * DK_B
ROT_B = DK_B // 4
ROPE_THETA = 500000.0
DIFF_LAMBDA_INIT = 0.8 - 0.6 * math.exp(-0.3 * 0)
H_C, DK_C, DV_C = 8, 128, 128
C_W = H_C * DV_C
CONV_W = 4
N_MEM = 256
X_HEADS = 4
X_HD = D_MODEL // X_HEADS
D_FF = 2816
N_EXPERTS = 8
D_FF_E = 3584
DN_ALPHA = (2 * DEPTH) ** 0.25
LN_EPS = 1e-5
A_QK = H_A * DK_A
A_V = H_A * DV_A
B_QK = H_B * 2 * DK_B
B_V = H_B * DV_B
PROJ0 = 2 * A_QK + 2 * A_V + 2 * B_QK + B_V

VMEM_LIMIT_V7X = 52 * 1024 * 1024
NEG_BIG = -1e30

MOE_TOK_TILE = 512
MOE_SUB = 32
MOE_FF_BLK = 512
MOE_SLOTS = 2 * MOE_TOK_TILE // MOE_SUB + N_EXPERTS
GDN_HEAD_GROUP = 8
CONV_ROWS = 32


def _cparams(sem):
    return pltpu.CompilerParams(dimension_semantics=sem, vmem_limit_bytes=VMEM_LIMIT_V7X)


def _dot(a, b):
    return jnp.dot(a, b, preferred_element_type=F32)


def _dot_nt(a, b):
    return lax.dot_general(a, b, (((1,), (1,)), ((), ())), preferred_element_type=F32)


def _dot_tn(a, b):
    return lax.dot_general(a, b, (((0,), (0,)), ((), ())), preferred_element_type=F32)


def _silu(x):
    return x * (1.0 / (1.0 + jnp.exp(-x)))


def _layer_norm_rows(y, g, b):
    mu = jnp.mean(y, axis=-1, keepdims=True)
    d = y - mu
    var = jnp.mean(d * d, axis=-1, keepdims=True)
    return d * lax.rsqrt(var + LN_EPS) * g + b


def _mm_kernel(x_ref, w_ref, o_ref, *, scale):
    acc = _dot(x_ref[...].astype(BF16), w_ref[0])
    if scale != 1.0:
        acc = acc * scale
    o_ref[0] = acc.astype(o_ref.dtype)


def _matmul(x, w, out_dtype, tm, tn, scale=1.0, name="matmul"):
    M, K = x.shape
    G, _, N = w.shape
    tm = min(tm, M)
    tn = min(tn, N)
    assert M % tm == 0 and N % tn == 0
    return pl.pallas_call(
        functools.partial(_mm_kernel, scale=scale),
        out_shape=jax.ShapeDtypeStruct((G, M, N), out_dtype),
        grid=(G, M // tm, N // tn),
        in_specs=[pl.BlockSpec((tm, K), lambda g, i, j: (i, 0)),
                  pl.BlockSpec((1, K, tn), lambda g, i, j: (g, 0, j))],
        out_specs=pl.BlockSpec((1, tm, tn), lambda g, i, j: (g, i, j)),
        compiler_params=_cparams(("parallel", "parallel", "arbitrary")),
        name=name,
    )(x, w)


def _mm_dn_kernel(*refs, n_in, next_scale):
    xs = refs[:n_in]
    ws = refs[n_in:2 * n_in]
    rest = refs[2 * n_in:]
    if next_scale is None:
        r_ref, g_ref, b_ref, o_ref, ob_ref = rest
    else:
        r_ref, g_ref, b_ref, wn_ref, o_ref, ob_ref = rest
    acc = DN_ALPHA * r_ref[...]
    for x_ref, w_ref in zip(xs, ws):
        acc = acc + _dot(x_ref[...].astype(BF16), w_ref[...])
    y = _layer_norm_rows(acc, g_ref[...], b_ref[...])
    o_ref[...] = y
    if next_scale is None:
        ob_ref[...] = y.astype(BF16)
    else:
        ob_ref[...] = (_dot(y.astype(BF16), wn_ref[...]) * next_scale).astype(BF16)


def _matmul_deepnorm(xs, ws, resid, g, b, tm, w_next=None, next_scale=None, name="matmul_deepnorm"):
    M = resid.shape[0]
    tm = min(tm, M)
    assert M % tm == 0
    n_in = len(xs)
    in_specs = [pl.BlockSpec((tm, x.shape[1]), lambda i: (i, 0)) for x in xs]
    in_specs += [pl.BlockSpec(w.shape, lambda i: (0, 0)) for w in ws]
    in_specs += [pl.BlockSpec((tm, D_MODEL), lambda i: (i, 0)),
                 pl.BlockSpec((1, D_MODEL), lambda i: (0, 0)),
                 pl.BlockSpec((1, D_MODEL), lambda i: (0, 0))]
    args = [*xs, *ws, resid, g.reshape(1, D_MODEL), b.reshape(1, D_MODEL)]
    if w_next is not None:
        in_specs.append(pl.BlockSpec(w_next.shape, lambda i: (0, 0)))
        args.append(w_next)
    return pl.pallas_call(
        functools.partial(_mm_dn_kernel, n_in=n_in, next_scale=next_scale),
        out_shape=[jax.ShapeDtypeStruct((M, D_MODEL), F32), jax.ShapeDtypeStruct((M, D_MODEL), BF16)],
        grid=(M // tm,),
        in_specs=in_specs,
        out_specs=[pl.BlockSpec((tm, D_MODEL), lambda i: (i, 0))] * 2,
        compiler_params=_cparams(("parallel",)),
        name=name,
    )(*args)


def _swap_halves(x, group, half):
    n = x.shape[-1]
    lane = lax.broadcasted_iota(I32, x.shape, x.ndim - 1) % group
    up = pltpu.roll(x, n - half, x.ndim - 1)
    dn = pltpu.roll(x, half, x.ndim - 1)
    return jnp.where(lane < half, up, dn)


def _store_rows(o_ref, y, by_head):
    if not by_head:
        o_ref[...] = y.astype(o_ref.dtype)
        return
    tm = y.shape[0]
    for h in range(H_B):
        o_ref[pl.ds(h, tm, stride=H_B), :] = y[:, h * DV_B:(h + 1) * DV_B].astype(o_ref.dtype)


def _proj0_kernel(*refs, kinds):
    x_ref = refs[0]
    n_out = len(kinds)
    ins, outs = refs[1:len(refs) - n_out], refs[len(refs) - n_out:]
    xb = x_ref[...]
    k = 0
    for (rot, by_head), o_ref in zip(kinds, outs):
        acc = _dot(xb, ins[k][...])
        k += 1
        if rot is not None:
            acc = acc * ins[k][...] + _swap_halves(acc, rot[0], rot[1]) * ins[k + 1][...]
            k += 2
        _store_rows(o_ref, acc, by_head)


def _proj0_call(xb, groups, tm, name):
    T = xb.shape[0]
    tm = min(tm, T)
    for g in groups:
        if g["rot"] is not None:
            tm = min(tm, g["rot"][0].shape[0])
    assert T % tm == 0
    row = lambda i: (i, 0)
    in_specs = [pl.BlockSpec((tm, D_MODEL), row)]
    args = [xb]
    kinds, out_shapes, out_specs = [], [], []
    for g in groups:
        blk = g["w"].shape[1]
        in_specs.append(pl.BlockSpec((D_MODEL, blk), lambda i: (0, 0)))
        args.append(g["w"])
        if g["rot"] is None:
            kinds.append((None, g["by_head"]))
        else:
            cos, sin, lane_group, half = g["rot"]
            assert cos.shape[0] % tm == 0
            npb = cos.shape[0] // tm
            in_specs += [pl.BlockSpec((tm, blk), lambda i, npb=npb: (i % npb, 0))] * 2
            args += [cos, sin]
            kinds.append(((lane_group, half), g["by_head"]))
        if g["by_head"]:
            assert blk == H_B * DV_B
            out_shapes.append(jax.ShapeDtypeStruct((T * H_B, DV_B), g["dtype"]))
            out_specs.append(pl.BlockSpec((tm * H_B, DV_B), row))
        else:
            out_shapes.append(jax.ShapeDtypeStruct((T, blk), g["dtype"]))
            out_specs.append(pl.BlockSpec((tm, blk), row))
    return pl.pallas_call(
        functools.partial(_proj0_kernel, kinds=tuple(kinds)),
        out_shape=out_shapes,
        grid=(T // tm,),
        in_specs=in_specs,
        out_specs=out_specs,
        compiler_params=_cparams(("parallel",)),
        name=name,
    )(*args)


def _rope_tables(pos):
    pos = pos.astype(F32)
    inv = RET_THETA ** (-jnp.arange(0, DK_A, 2, dtype=F32) / DK_A)
    ang = pos[:, None] * inv[None, :]
    c, s = jnp.cos(ang), jnp.sin(ang)
    c64 = jnp.concatenate([c, c], axis=-1)
    s64 = jnp.concatenate([-s, s], axis=-1)
    kscale = DK_A ** -0.5
    ca = jnp.concatenate([jnp.tile(c64, (1, H_A)), jnp.tile(c64, (1, H_A)) * kscale], axis=-1)
    sa = jnp.concatenate([jnp.tile(s64, (1, H_A)), jnp.tile(s64, (1, H_A)) * kscale], axis=-1)
    invb = ROPE_THETA ** (-jnp.arange(0, ROT_B, 2, dtype=F32) / ROT_B)
    angb = pos[:, None] * invb[None, :]
    cb, sb = jnp.cos(angb), jnp.sin(angb)
    rest = DK_B - ROT_B
    c64b = jnp.concatenate([cb, cb, jnp.ones((pos.shape[0], rest), F32)], axis=-1)
    s64b = jnp.concatenate([-sb, sb, jnp.zeros((pos.shape[0], rest), F32)], axis=-1)
    ck = jnp.tile(c64b, (1, 2 * H_B))
    sk = jnp.tile(s64b, (1, 2 * H_B))
    qscale = DK_B ** -0.5
    return ca, sa, ck * qscale, sk * qscale, ck, sk


def _proj0(xb, w_in0_bf, tables, tm):
    ca, sa, cq, sq, ck, sk = tables
    blk = 512
    assert PROJ0 == 6 * blk
    w = [w_in0_bf[:, j * blk:(j + 1) * blk] for j in range(6)]
    rot_a = (DK_A, DK_A // 2)
    rot_b = (DK_B, ROT_B // 2)
    grp = lambda j, dtype, rot=None, by_head=False: dict(w=w[j], dtype=dtype, rot=rot, by_head=by_head)
    qka, va = _proj0_call(xb, [grp(0, BF16, (ca, sa) + rot_a), grp(1, BF16)], tm, "proj0_ret_qkv")
    ga, vb = _proj0_call(xb, [grp(2, F32), grp(5, F32, by_head=True)], tm, "proj0_gate_v")
    qb, kb = _proj0_call(xb, [grp(3, BF16, (cq, sq) + rot_b), grp(4, F32, (ck, sk) + rot_b, by_head=True)],
                         tm, "proj0_diff_qk")
    return qka, va, ga, qb, kb, vb


def _retention_kernel(qk_ref, v_ref, g_ref, dmat_ref, qdec_ref, kdec_ref, sdec_ref, s0_ref, ng_ref,
                      o_ref, sout_ref, s_scr):
    c = pl.program_id(1)

    @pl.when(c == 0)
    def _():
        s_scr[...] = s0_ref[0]

    for h in range(H_A):
        q = qk_ref[:, h * DK_A:(h + 1) * DK_A]
        k = qk_ref[:, A_QK + h * DK_A:A_QK + (h + 1) * DK_A]
        v = v_ref[:, h * DV_A:(h + 1) * DV_A]
        s = s_scr[h]
        scores = _dot_nt(q, k) * dmat_ref[h]
        o = _dot(scores.astype(BF16), v) + _dot(q, s.astype(BF16)) * qdec_ref[h]
        kd = (k.astype(F32) * kdec_ref[h]).astype(BF16)
        s_scr[h] = s * sdec_ref[h] + _dot_tn(kd, v)
        mu = jnp.mean(o, axis=-1, keepdims=True)
        d = o - mu
        var = jnp.mean(d * d, axis=-1, keepdims=True)
        gate = g_ref[:, h * DV_A:(h + 1) * DV_A]
        y = d * lax.rsqrt(var + LN_EPS) * ng_ref[:, h * DV_A:(h + 1) * DV_A] * _silu(gate)
        o_ref[:, h * DV_A:(h + 1) * DV_A] = y.astype(o_ref.dtype)

    @pl.when(c == pl.num_programs(1) - 1)
    def _():
        sout_ref[0] = s_scr[...]


def _retention_tables(lc):
    lg = jnp.log1p(-jnp.exp2(-5.0 - jnp.arange(H_A, dtype=F32)))
    t = jnp.arange(lc, dtype=F32)
    rel = t[:, None] - t[None, :]
    causal = rel >= 0
    dmat = jnp.where(causal, jnp.exp(lg[:, None, None] * jnp.where(causal, rel, 0.0)), 0.0)
    qdec = jnp.exp(lg[:, None] * (t[None, :] + 1.0))
    kdec = jnp.exp(lg[:, None] * (lc - 1.0 - t[None, :]))
    sdec = jnp.exp(lg * lc)
    return (dmat,
            jnp.broadcast_to(qdec[:, :, None], (H_A, lc, DV_A)),
            jnp.broadcast_to(kdec[:, :, None], (H_A, lc, DK_A)),
            jnp.broadcast_to(sdec[:, None, None], (H_A, DK_A, DV_A)))


def _retention(qka, va, ga, state0, ret_norm_g, n_batch, lc):
    T = qka.shape[0]
    nc = T // (n_batch * lc)
    assert nc * n_batch * lc == T
    dmat, qdec, kdec, sdec = _retention_tables(lc)
    row = lambda b, c: (b * nc + c, 0)
    const3 = lambda b, c: (0, 0, 0)
    return pl.pallas_call(
        _retention_kernel,
        out_shape=[jax.ShapeDtypeStruct((T, A_V), BF16),
                   jax.ShapeDtypeStruct((n_batch, H_A, DK_A, DV_A), F32)],
        grid=(n_batch, nc),
        in_specs=[pl.BlockSpec((lc, 2 * A_QK), row),
                  pl.BlockSpec((lc, A_V), row),
                  pl.BlockSpec((lc, A_V), row),
                  pl.BlockSpec((H_A, lc, lc), const3),
                  pl.BlockSpec((H_A, lc, DV_A), const3),
                  pl.BlockSpec((H_A, lc, DK_A), const3),
                  pl.BlockSpec((H_A, DK_A, DV_A), const3),
                  pl.BlockSpec((1, H_A, DK_A, DV_A), lambda b, c: (b, 0, 0, 0)),
                  pl.BlockSpec((1, A_V), lambda b, c: (0, 0))],
        out_specs=[pl.BlockSpec((lc, A_V), row),
                   pl.BlockSpec((1, H_A, DK_A, DV_A), lambda b, c: (b, 0, 0, 0))],
        scratch_shapes=[pltpu.VMEM((H_A, DK_A, DV_A), F32)],
        compiler_params=_cparams(("parallel", "arbitrary")),
        name="retention",
    )(qka, va, ga, dmat, qdec, kdec, sdec, state0, ret_norm_g.reshape(1, A_V))


def _diff_finish(acc1, l1, acc2, l2, lam, ng):
    o = acc1 * (1.0 / l1) - lam * (acc2 * (1.0 / l2))
    ms = jnp.mean(o * o, axis=-1, keepdims=True)
    return o * lax.rsqrt(ms + LN_EPS) * ng * (1.0 - DIFF_LAMBDA_INIT)


def _diff_prompt_kernel(lam_ref, q_ref, k_ref, v_ref, ng_ref, o_ref, kbf, vbf, *, tq):
    qi = pl.program_id(2)

    @pl.when(qi == 0)
    def _():
        h = pl.program_id(1)
        seq = kbf.shape[0]
        kbf[...] = k_ref[pl.ds(h, seq, stride=H_B), :].astype(BF16)
        vbf[...] = v_ref[pl.ds(h, seq, stride=H_B), :].astype(BF16)

    rc = lax.broadcasted_iota(I32, (tq, tq), 0) // CHUNK
    cc = lax.broadcasted_iota(I32, (tq, tq), 1) // CHUNK
    vis = cc <= rc
    nq = kbf.shape[0] // tq

    for j in range(nq):
        @pl.when(qi == j)
        def _(j=j):
            n_full = j * tq
            kd = kbf[n_full:n_full + tq, :]
            vd = vbf[n_full:n_full + tq, :]
            sls = [slice(c * DK_B, (c + 1) * DK_B) for c in range(2)]
            qs = [q_ref[:, sl] for sl in sls]
            sds = [jnp.where(vis, _dot_nt(q, kd[:, sl]), NEG_BIG) for q, sl in zip(qs, sls)]
            ms = [jnp.max(sd, axis=-1, keepdims=True) for sd in sds]
            if j > 0:
                sfs = [_dot_nt(q, kbf[0:n_full, sl]) for q, sl in zip(qs, sls)]
                ms = [jnp.maximum(m, jnp.max(sf, axis=-1, keepdims=True)) for m, sf in zip(ms, sfs)]
            pds = [jnp.exp(sd - m) for sd, m in zip(sds, ms)]
            ls = [jnp.sum(pd, axis=-1, keepdims=True) for pd in pds]
            accs = [_dot(pd.astype(BF16), vd) for pd in pds]
            if j > 0:
                pfs = [jnp.exp(sf - m) for sf, m in zip(sfs, ms)]
                ls = [l + jnp.sum(pf, axis=-1, keepdims=True) for l, pf in zip(ls, pfs)]
                accs = [a + _dot(pf.astype(BF16), vbf[0:n_full, :]) for a, pf in zip(accs, pfs)]
            o_ref[...] = _diff_finish(accs[0], ls[0], accs[1], ls[1], lam_ref[0], ng_ref[...]).astype(o_ref.dtype)


def _diff_prompt(qb, kb, vb, lam, diff_norm_g, n_batch, seq, tq):
    T = qb.shape[0]
    assert seq % tq == 0 and tq % CHUNK == 0
    nq = seq // tq
    return pl.pallas_call(
        functools.partial(_diff_prompt_kernel, tq=tq),
        out_shape=jax.ShapeDtypeStruct((T, B_V), BF16),
        grid_spec=pltpu.PrefetchScalarGridSpec(
            num_scalar_prefetch=1,
            grid=(n_batch, H_B, nq),
            in_specs=[pl.BlockSpec((tq, DV_B), lambda b, h, i, lam: (b * nq + i, h)),
                      pl.BlockSpec((seq * H_B, DV_B), lambda b, h, i, lam: (b, 0)),
                      pl.BlockSpec((seq * H_B, DV_B), lambda b, h, i, lam: (b, 0)),
                      pl.BlockSpec((1, DV_B), lambda b, h, i, lam: (0, 0))],
            out_specs=pl.BlockSpec((tq, DV_B), lambda b, h, i, lam: (b * nq + i, h)),
            scratch_shapes=[pltpu.VMEM((seq, DV_B), BF16), pltpu.VMEM((seq, DV_B), BF16)]),
        compiler_params=_cparams(("parallel", "parallel", "arbitrary")),
        name="diff_attn_prompt",
    )(lam, qb, kb, vb, diff_norm_g.reshape(1, DV_B))


def _diff_step_kernel(lam_ref, q_ref, kp_ref, vp_ref, kn_ref, vn_ref, ng_ref, o_ref, *, past, ln):
    qchunk = (past + lax.broadcasted_iota(I32, (ln, 1), 0)) // CHUNK
    vis_p = (lax.broadcasted_iota(I32, (ln, past), 1) // CHUNK) <= qchunk
    vis_n = ((past + lax.broadcasted_iota(I32, (ln, ln), 1)) // CHUNK) <= qchunk
    for h in range(H_B):
        q = q_ref[:, h * DV_B:(h + 1) * DV_B]
        kp = kp_ref[pl.ds(h, past, stride=H_B), :].astype(BF16)
        vp = vp_ref[pl.ds(h, past, stride=H_B), :].astype(BF16)
        kn = kn_ref[pl.ds(h, ln, stride=H_B), :].astype(BF16)
        vn = vn_ref[pl.ds(h, ln, stride=H_B), :].astype(BF16)
        accs = []
        for c in range(2):
            sl = slice(c * DK_B, (c + 1) * DK_B)
            sp = jnp.where(vis_p, _dot_nt(q[:, sl], kp[:, sl]), NEG_BIG)
            sn = jnp.where(vis_n, _dot_nt(q[:, sl], kn[:, sl]), NEG_BIG)
            m = jnp.maximum(jnp.max(sp, axis=-1, keepdims=True), jnp.max(sn, axis=-1, keepdims=True))
            pp = jnp.exp(sp - m)
            pn = jnp.exp(sn - m)
            l = jnp.sum(pp, axis=-1, keepdims=True) + jnp.sum(pn, axis=-1, keepdims=True)
            accs += [_dot(pp.astype(BF16), vp) + _dot(pn.astype(BF16), vn), l]
        y = _diff_finish(accs[0], accs[1], accs[2], accs[3], lam_ref[0], ng_ref[...])
        o_ref[:, h * DV_B:(h + 1) * DV_B] = y.astype(o_ref.dtype)


def _diff_step(qb, kb, vb, past_k, past_v, lam, diff_norm_g, n_batch, ln):
    past = past_k.shape[0] // (n_batch * H_B)
    blk = lambda b, lam: (b, 0)
    return pl.pallas_call(
        functools.partial(_diff_step_kernel, past=past, ln=ln),
        out_shape=jax.ShapeDtypeStruct((n_batch * ln, B_V), BF16),
        grid_spec=pltpu.PrefetchScalarGridSpec(
            num_scalar_prefetch=1,
            grid=(n_batch,),
            in_specs=[pl.BlockSpec((ln, B_QK), blk),
                      pl.BlockSpec((past * H_B, DV_B), blk),
                      pl.BlockSpec((past * H_B, DV_B), blk),
                      pl.BlockSpec((ln * H_B, DV_B), blk),
                      pl.BlockSpec((ln * H_B, DV_B), blk),
                      pl.BlockSpec((1, DV_B), lambda b, lam: (0, 0))],
            out_specs=pl.BlockSpec((ln, B_V), blk)),
        compiler_params=_cparams(("parallel",)),
        name="diff_attn_step",
    )(lam, qb, past_k, past_v, kb, vb, diff_norm_g.reshape(1, DV_B))


X_LANE_BLKS = X_HD // 128


def _xattn_kernel(q_ref, mk_ref, mv_ref, o_ref, *, split):
    stride = X_LANE_BLKS * X_HEADS
    for h in range(X_HEADS):
        if split:
            pieces = [(slice(h * X_HD + j * 128, h * X_HD + (j + 1) * 128), pl.ds(j * X_HEADS + h, N_MEM, stride=stride))
                      for j in range(X_LANE_BLKS)]
            s = sum(_dot_nt(q_ref[:, cs], mk_ref[rs, :].astype(BF16)) for cs, rs in pieces)
        else:
            sl = slice(h * X_HD, (h + 1) * X_HD)
            s = _dot_nt(q_ref[:, sl], mk_ref[:, sl].astype(BF16))
        p = jnp.exp(s - jnp.max(s, axis=-1, keepdims=True))
        inv_l = 1.0 / jnp.sum(p, axis=-1, keepdims=True)
        pb = p.astype(BF16)
        if split:
            for cs, rs in pieces:
                o_ref[:, cs] = (_dot(pb, mv_ref[rs, :].astype(BF16)) * inv_l).astype(o_ref.dtype)
        else:
            o_ref[:, sl] = (_dot(pb, mv_ref[:, sl].astype(BF16)) * inv_l).astype(o_ref.dtype)


def _xattn(q, mk, mv, layer, n_batch, tq, split):
    T = q.shape[0]
    per = T // n_batch
    tq = min(tq, per)
    nt = per // tq
    mem_blk = (None, N_MEM * X_LANE_BLKS * X_HEADS, 128) if split else (None, N_MEM, D_MODEL)
    return pl.pallas_call(
        functools.partial(_xattn_kernel, split=split),
        out_shape=jax.ShapeDtypeStruct((T, D_MODEL), BF16),
        grid=(n_batch, nt),
        in_specs=[pl.BlockSpec((tq, D_MODEL), lambda b, t: (b * nt + t, 0)),
                  pl.BlockSpec(mem_blk, lambda b, t: (layer, b, 0)),
                  pl.BlockSpec(mem_blk, lambda b, t: (layer, b, 0))],
        out_specs=pl.BlockSpec((tq, D_MODEL), lambda b, t: (b * nt + t, 0)),
        compiler_params=_cparams(("parallel", "arbitrary")),
        name="mem_xattn",
    )(q, mk, mv)


def _swiglu_up_kernel(x_ref, wg_ref, wu_ref, o_ref):
    xb = x_ref[...]
    g = _dot(xb, wg_ref[...])
    u = _dot(xb, wu_ref[...])
    o_ref[...] = (_silu(g) * u).astype(o_ref.dtype)


def _swiglu_up(xb, w_gu_bf, tm, tn):
    T = xb.shape[0]
    tm = min(tm, T)
    nj = D_FF // tn
    assert D_FF % tn == 0
    return pl.pallas_call(
        _swiglu_up_kernel,
        out_shape=jax.ShapeDtypeStruct((T, D_FF), BF16),
        grid=(T // tm, nj),
        in_specs=[pl.BlockSpec((tm, D_MODEL), lambda i, j: (i, 0)),
                  pl.BlockSpec((D_MODEL, tn), lambda i, j: (0, j)),
                  pl.BlockSpec((D_MODEL, tn), lambda i, j: (0, nj + j))],
        out_specs=pl.BlockSpec((tm, tn), lambda i, j: (i, j)),
        compiler_params=_cparams(("parallel", "arbitrary")),
        name="swiglu_up",
    )(xb, w_gu_bf, w_gu_bf)


def _conv_kernel(x_ref, prev_ref, st_ref, w_ref, o_ref, buf, *, tm):
    t = pl.program_id(1)
    cb = pl.program_id(2)
    buf[0:8, :] = jnp.where(t == 0, st_ref[0], prev_ref[...])
    buf[8:8 + tm, :] = x_ref[...]
    scale = jnp.where(cb == 0, DK_C ** -0.5, 1.0)
    rc = min(CONV_ROWS, tm)
    for h in range(H_C):
        cs = slice(h * DK_C, (h + 1) * DK_C)
        w = [w_ref[i:i + 1, cs] for i in range(CONV_W)]
        for r0 in range(0, tm, rc):
            y = buf[8 + r0:8 + r0 + rc, cs] * w[CONV_W - 1]
            for i in range(CONV_W - 1):
                y = y + buf[5 + i + r0:5 + i + r0 + rc, cs] * w[i]
            y = _silu(y)
            ss = jnp.sum(y * y, axis=-1, keepdims=True)
            f = jnp.where(cb == 2, 1.0, lax.rsqrt(ss + 1e-6) * scale)
            o_ref[r0:r0 + rc, cs] = (y * f).astype(o_ref.dtype)


def _gdn_conv(proj1, conv_state8, conv_w, n_batch, tm):
    T = proj1.shape[0]
    per = T // n_batch
    tm = min(tm, per)
    nt = per // tm
    assert per % tm == 0 and tm % 8 == 0
    return pl.pallas_call(
        functools.partial(_conv_kernel, tm=tm),
        out_shape=jax.ShapeDtypeStruct((T, 3 * C_W), BF16),
        grid=(n_batch, nt, 3),
        in_specs=[pl.BlockSpec((tm, C_W), lambda b, t, c: (b * nt + t, c)),
                  pl.BlockSpec((8, C_W), lambda b, t, c: (jnp.maximum((b * nt + t) * (tm // 8) - 1, 0), c)),
                  pl.BlockSpec((1, 8, C_W), lambda b, t, c: (b, 0, c)),
                  pl.BlockSpec((CONV_W, C_W), lambda b, t, c: (0, c))],
        out_specs=pl.BlockSpec((tm, C_W), lambda b, t, c: (b * nt + t, c)),
        scratch_shapes=[pltpu.VMEM((8 + tm, C_W), F32)],
        compiler_params=_cparams(("parallel", "arbitrary", "arbitrary")),
        name="gdn_conv",
    )(proj1, proj1, conv_state8, conv_w)


def _gdn_kernel(q_ref, k_ref, v_ref, z_ref, ab_ref, alog_ref, dtb_ref, ng_ref, s0_ref,
                o_ref, sout_ref, s_scr, *, lc, l_real):
    c = pl.program_id(1)

    @pl.when(c == 0)
    def _():
        s_scr[...] = s0_ref[0]

    ri = lax.broadcasted_iota(I32, (lc, lc), 0)
    ci = lax.broadcasted_iota(I32, (lc, lc), 1)
    incl = ci <= ri
    strict = ci < ri
    eye = (ci == ri).astype(F32)
    ab = ab_ref[...]
    sp = jnp.maximum(ab + dtb_ref[...], 0.0) + jnp.log1p(jnp.exp(-jnp.abs(ab + dtb_ref[...])))
    glog = -jnp.exp(alog_ref[...]) * sp
    if l_real < lc:
        live = lax.broadcasted_iota(I32, (lc, 1), 0) < l_real
        glog = jnp.where(live, glog, 0.0)
    gcum = jnp.dot(incl.astype(F32), glog, preferred_element_type=F32, precision=lax.Precision.HIGHEST)
    gcum_t = gcum.T
    beta_all = 1.0 / (1.0 + jnp.exp(-ab))

    levels = []
    s = 1
    while s < lc:
        levels.append(((ri // (2 * s)) == (ci // (2 * s))) & (((ri // s) % 2) == 1) & (((ci // s) % 2) == 0))
        s *= 2

    group = GDN_HEAD_GROUP if lc > 128 else H_C
    for h0 in range(0, H_C, group):
        heads = range(h0, h0 + group)
        sls = [slice(h * DK_C, (h + 1) * DK_C) for h in heads]
        ks, kfs, vs, gcols, decays, egs, kbs, a_lows = [], [], [], [], [], [], [], []
        for h, sl in zip(heads, sls):
            k = k_ref[:, sl]
            v = v_ref[:, sl].astype(F32)
            if l_real < lc:
                k = jnp.where(live, k, jnp.zeros_like(k))
                v = jnp.where(live, v, 0.0)
            kf = k.astype(F32)
            gcol = gcum[:, h:h + 1]
            grow = gcum_t[h:h + 1, :]
            beta = beta_all[:, H_C + h:H_C + h + 1]
            decay = jnp.exp(jnp.where(incl, gcol - grow, NEG_BIG))
            kb = kf * beta
            ks.append(k)
            kfs.append(kf)
            gcols.append(gcol)
            decays.append(decay)
            egs.append(jnp.exp(gcol))
            kbs.append(kb)
            vs.append(v * beta)
            a_lows.append(jnp.where(strict, _dot_nt(kb.astype(BF16), k) * decay, 0.0))
        tinvs = [eye - jnp.where(levels[0], a, 0.0) for a in a_lows]
        for lvl, off in enumerate(levels[1:], start=1):
            s = 2 ** lvl
            tbs = [t.astype(BF16) for t in tinvs]
            if s < 8:
                a_offs = [jnp.where(off, a, 0.0).astype(BF16) for a in a_lows]
                ws = [_dot(a, t).astype(BF16) for a, t in zip(a_offs, tbs)]
                tinvs = [t - _dot(tb, w) for t, tb, w in zip(tinvs, tbs, ws)]
            else:
                odd = [slice((2 * j + 1) * s, (2 * j + 2) * s) for j in range(lc // (2 * s))]
                even = [slice(2 * j * s, (2 * j + 1) * s) for j in range(lc // (2 * s))]
                take = lambda x: jnp.concatenate([x[sl, :] for sl in odd], axis=0)
                rh = lax.broadcasted_iota(I32, (lc // 2, lc), 0)
                ch = lax.broadcasted_iota(I32, (lc // 2, lc), 1)
                off_odd = (ch // s) == 2 * (rh // s)
                zero = jnp.zeros((s, lc), F32)
                new = []
                for a, t, tb in zip(a_lows, tinvs, tbs):
                    w_odd = _dot(jnp.where(off_odd, take(a), 0.0).astype(BF16), tb)
                    w_full = jnp.concatenate(
                        [piece for j in range(len(odd)) for piece in (zero, w_odd[j * s:(j + 1) * s, :])], axis=0)
                    t_odd = take(t)
                    t_odd = t_odd - _dot(t_odd.astype(BF16), w_full.astype(BF16))
                    new.append(jnp.concatenate(
                        [piece for j, ev in enumerate(even) for piece in (t[ev, :], t_odd[j * s:(j + 1) * s, :])],
                        axis=0))
                tinvs = new
        rhss = [jnp.concatenate([vb, kb * eg], axis=-1).astype(BF16) for vb, kb, eg in zip(vs, kbs, egs)]
        sols = [_dot(t.astype(BF16), r) for t, r in zip(tinvs, rhss)]
        sts = [s_scr[h] for h in heads]
        stbs = [st.astype(BF16) for st in sts]
        ubs = [(sol[:, :DV_C] - _dot(sol[:, DV_C:].astype(BF16), stb)).astype(BF16) for sol, stb in zip(sols, stbs)]
        qs = [q_ref[:, sl] for sl in sls]
        qks = [(_dot_nt(q, k) * decay).astype(BF16) for q, k, decay in zip(qs, ks, decays)]
        os_ = [_dot((q.astype(F32) * eg).astype(BF16), stb) + _dot(qk, ub)
               for q, eg, stb, qk, ub in zip(qs, egs, stbs, qks, ubs)]
        for h, st, kf, gcol, ub in zip(heads, sts, kfs, gcols, ubs):
            g_last = gcol[lc - 1:lc, :]
            kdec = (kf * jnp.exp(g_last - gcol)).astype(BF16)
            s_scr[h] = st * jnp.exp(g_last) + _dot_tn(kdec, ub)
        for sl, o in zip(sls, os_):
            ms = jnp.mean(o * o, axis=-1, keepdims=True)
            y = o * lax.rsqrt(ms + LN_EPS) * ng_ref[...] * _silu(z_ref[:, sl])
            o_ref[:, sl] = y.astype(o_ref.dtype)

    @pl.when(c == pl.num_programs(1) - 1)
    def _():
        sout_ref[0] = s_scr[...]


def _gdn(qkvn, proj1, ab, a_log, dt_bias, gdn_norm_g, state0, n_batch, lc, l_real):
    T = qkvn.shape[0]
    nc = T // (n_batch * lc)
    assert nc * n_batch * lc == T and (nc == 1 or l_real == lc)
    pad = lambda v: jnp.pad(v.astype(F32).reshape(1, H_C), ((0, 0), (0, 128 - H_C)))
    return pl.pallas_call(
        functools.partial(_gdn_kernel, lc=lc, l_real=l_real),
        out_shape=[jax.ShapeDtypeStruct((T, C_W), BF16),
                   jax.ShapeDtypeStruct((n_batch, H_C, DK_C, DV_C), F32)],
        grid=(n_batch, nc),
        in_specs=[pl.BlockSpec((lc, C_W), lambda b, c: (b * nc + c, 0)),
                  pl.BlockSpec((lc, C_W), lambda b, c: (b * nc + c, 1)),
                  pl.BlockSpec((lc, C_W), lambda b, c: (b * nc + c, 2)),
                  pl.BlockSpec((lc, C_W), lambda b, c: (b * nc + c, 3)),
                  pl.BlockSpec((lc, 128), lambda b, c: (b * nc + c, 0)),
                  pl.BlockSpec((1, 128), lambda b, c: (0, 0)),
                  pl.BlockSpec((1, 128), lambda b, c: (0, 0)),
                  pl.BlockSpec((1, DV_C), lambda b, c: (0, 0)),
                  pl.BlockSpec((1, H_C, DK_C, DV_C), lambda b, c: (b, 0, 0, 0))],
        out_specs=[pl.BlockSpec((lc, C_W), lambda b, c: (b * nc + c, 0)),
                   pl.BlockSpec((1, H_C, DK_C, DV_C), lambda b, c: (b, 0, 0, 0))],
        scratch_shapes=[pltpu.VMEM((H_C, DK_C, DV_C), F32)],
        compiler_params=_cparams(("parallel", "arbitrary")),
        name="gated_delta",
    )(qkvn, qkvn, qkvn, proj1, ab, pad(a_log), pad(dt_bias), gdn_norm_g.reshape(1, DV_C), state0)


def _router_kernel(x_ref, wt_ref, rt_ref, rtt_ref, cnt_ref):
    tt = x_ref.shape[0]
    logits = lax.dot_general(wt_ref[...], x_ref[...], (((1,), (1,)), ((), ())),
                             preferred_element_type=F32, precision=lax.Precision.HIGHEST)
    eid = lax.broadcasted_iota(I32, (N_EXPERTS, tt), 0).astype(F32)
    m1 = jnp.max(logits, axis=0, keepdims=True)
    e1 = jnp.min(jnp.where(logits == m1, eid, float(N_EXPERTS)), axis=0, keepdims=True)
    rest = jnp.where(eid == e1, -jnp.inf, logits)
    m2 = jnp.max(rest, axis=0, keepdims=True)
    e2 = jnp.min(jnp.where(rest == m2, eid, float(N_EXPERTS)), axis=0, keepdims=True)
    ev = jnp.exp(m2 - m1)
    g1 = 1.0 / (1.0 + ev)
    g2 = ev / (1.0 + ev)
    hit1 = eid == e1
    hit2 = eid == e2
    member = jnp.where(hit1 | hit2, 1.0, 0.0)
    before = (lax.broadcasted_iota(I32, (tt, tt), 0) < lax.broadcasted_iota(I32, (tt, tt), 1))
    rank = _dot(member.astype(BF16), jnp.where(before, 1.0, 0.0).astype(BF16))
    r1 = jnp.sum(jnp.where(hit1, rank, 0.0), axis=0, keepdims=True)
    r2 = jnp.sum(jnp.where(hit2, rank, 0.0), axis=0, keepdims=True)
    rows = jnp.concatenate([e1, e2, r1, r2, g1, g2, jnp.zeros((2, tt), F32)], axis=0)
    rt_ref[0] = rows
    rtt_ref[0] = jnp.concatenate([rows, jnp.zeros((128 - 8, tt), F32)], axis=0).T
    cnt = jnp.sum(member, axis=1, keepdims=True).astype(I32)
    cnt_ref[0] = jnp.broadcast_to(cnt, (N_EXPERTS, 128))


def _router(x, w_router):
    T = x.shape[0]
    tt = MOE_TOK_TILE
    nt = T // tt
    assert T % tt == 0
    return pl.pallas_call(
        _router_kernel,
        out_shape=[jax.ShapeDtypeStruct((nt, 8, tt), F32),
                   jax.ShapeDtypeStruct((nt, tt, 128), F32),
                   jax.ShapeDtypeStruct((nt, N_EXPERTS, 128), I32)],
        grid=(nt,),
        in_specs=[pl.BlockSpec((tt, D_MODEL), lambda i: (i, 0)),
                  pl.BlockSpec((N_EXPERTS, D_MODEL), lambda i: (0, 0))],
        out_specs=[pl.BlockSpec((1, 8, tt), lambda i: (i, 0, 0)),
                   pl.BlockSpec((1, tt, 128), lambda i: (i, 0, 0)),
                   pl.BlockSpec((1, N_EXPERTS, 128), lambda i: (i, 0, 0))],
        compiler_params=_cparams(("parallel",)),
        name="moe_router",
    )(x, w_router.T)


def _moe_tables(cnt, n_tiles, group):
    E = N_EXPERTS
    nb_max = (2 * n_tiles * MOE_TOK_TILE) // MOE_SUB + n_tiles * E
    nb_tot = -(-(nb_max + E * (group - 1)) // group) * group
    nblk = (cnt + MOE_SUB - 1) // MOE_SUB
    nbe = jnp.sum(nblk, axis=0)
    nbe_pad = (nbe + group - 1) // group * group
    ends_e = jnp.cumsum(nbe_pad)
    base = ends_e - nbe_pad
    dest0 = base[None, :] + jnp.cumsum(nblk, axis=0) - nblk
    ends_t = jnp.cumsum(nblk, axis=1)
    kk = jnp.arange(MOE_SLOTS, dtype=I32)
    slot_e = jnp.minimum(jnp.sum(ends_t[:, None, :] <= kk[None, :, None], axis=-1), E - 1).astype(I32)
    slot_valid = kk[None, :] < ends_t[:, -1:]
    start_t = jnp.take_along_axis(ends_t - nblk, slot_e, axis=1)
    slot_j = jnp.where(slot_valid, kk[None, :] - start_t, 0)
    slot_dest = jnp.where(slot_valid, jnp.take_along_axis(dest0, slot_e, axis=1) + slot_j, 0)
    npad = nbe_pad - nbe
    pm = jnp.arange(E * (group - 1), dtype=I32)
    pe, pt = pm // max(group - 1, 1), pm % max(group - 1, 1)
    pad_valid = pt < npad[pe]
    pad_dest = jnp.where(pad_valid, base[pe] + nbe[pe] + pt, 0)
    n_rt = nb_tot // group
    r0 = jnp.arange(n_rt, dtype=I32) * group
    rt_valid = r0 < ends_e[-1]
    rt_e = jnp.minimum(jnp.sum(ends_e[None, :] <= r0[:, None], axis=-1), E - 1)
    last_e = jnp.minimum(jnp.sum(ends_e <= ends_e[-1] - 1), E - 1)
    rt_e = jnp.where(rt_valid, rt_e, last_e)
    i32 = lambda a: a.astype(I32).reshape(-1)
    return dict(nb_tot=nb_tot, n_rt=n_rt, slot_e=i32(slot_e), slot_j=i32(slot_j), slot_dest=i32(slot_dest),
                slot_valid=i32(slot_valid), pad_dest=i32(pad_dest), pad_valid=i32(pad_valid),
                rt_e=i32(rt_e), rt_valid=i32(rt_valid), n_valid_rt=i32(ends_e[-1] // group))


def _moe_gather_kernel(se_ref, sj_ref, sd_ref, sv_ref, pd_ref, pv_ref, nv_ref, x_ref, rt_ref, xg_hbm,
                       buf, zbuf, sem, zsem, *, n_pad, n_rt, rows):
    i = pl.program_id(0)
    rt = rt_ref[0]
    height = MOE_SLOTS * MOE_SUB
    row = lax.broadcasted_iota(I32, (height, 1), 0)
    row_e = jnp.full((height, 1), -1.0, F32)
    row_r = (row % MOE_SUB).astype(F32)
    for k in range(MOE_SLOTS):
        idx = i * MOE_SLOTS + k
        here = (row // MOE_SUB) == k
        row_e = jnp.where(here, jnp.where(sv_ref[idx] == 1, se_ref[idx], -1).astype(F32), row_e)
        row_r = row_r + jnp.where(here, (sj_ref[idx] * MOE_SUB).astype(F32), 0.0)
    sel = ((rt[0:1, :] == row_e) & (rt[2:3, :] == row_r)) | ((rt[1:2, :] == row_e) & (rt[3:4, :] == row_r))
    buf[...] = _dot(jnp.where(sel, 1.0, 0.0).astype(BF16), x_ref[...]).astype(BF16)

    def slot_copy(k):
        dst = pl.multiple_of(sd_ref[i * MOE_SLOTS + k] * MOE_SUB, MOE_SUB)
        return pltpu.make_async_copy(buf.at[pl.ds(k * MOE_SUB, MOE_SUB)], xg_hbm.at[pl.ds(dst, MOE_SUB)], sem.at[k])

    for k in range(MOE_SLOTS):
        @pl.when(sv_ref[i * MOE_SLOTS + k] == 1)
        def _(k=k):
            slot_copy(k).start()

    for k in range(MOE_SLOTS):
        @pl.when(sv_ref[i * MOE_SLOTS + k] == 1)
        def _(k=k):
            slot_copy(k).wait()

    def pad_copy(m):
        dst = pl.multiple_of(pd_ref[m] * MOE_SUB, MOE_SUB)
        return pltpu.make_async_copy(zbuf.at[pl.ds(0, MOE_SUB)], xg_hbm.at[pl.ds(dst, MOE_SUB)], zsem.at[0])

    def tail_copy(r):
        dst = pl.multiple_of(r * rows, rows)
        return pltpu.make_async_copy(zbuf, xg_hbm.at[pl.ds(dst, rows)], zsem.at[1])

    @pl.when(i == pl.num_programs(0) - 1)
    def _():
        zbuf[...] = jnp.zeros_like(zbuf)

        def pad_start(m, carry):
            @pl.when(pv_ref[m] == 1)
            def _():
                pad_copy(m).start()
            return carry

        def pad_wait(m, carry):
            @pl.when(pv_ref[m] == 1)
            def _():
                pad_copy(m).wait()
            return carry

        def tail_start(r, carry):
            tail_copy(r).start()
            return carry

        def tail_wait(r, carry):
            tail_copy(r).wait()
            return carry

        lax.fori_loop(0, n_pad, pad_start, 0)
        lax.fori_loop(nv_ref[0], n_rt, tail_start, 0)
        lax.fori_loop(0, n_pad, pad_wait, 0)
        lax.fori_loop(nv_ref[0], n_rt, tail_wait, 0)


def _moe_gather(xb, rt, tb, rows):
    nb = tb["nb_tot"]
    tt = MOE_TOK_TILE
    nt = xb.shape[0] // tt
    n_pad = tb["pad_dest"].shape[0]
    return pl.pallas_call(
        functools.partial(_moe_gather_kernel, n_pad=n_pad, n_rt=tb["n_rt"], rows=rows),
        out_shape=jax.ShapeDtypeStruct((nb * MOE_SUB, D_MODEL), BF16),
        grid_spec=pltpu.PrefetchScalarGridSpec(
            num_scalar_prefetch=7,
            grid=(nt,),
            in_specs=[pl.BlockSpec((tt, D_MODEL), lambda i, *_: (i, 0)),
                      pl.BlockSpec((1, 8, tt), lambda i, *_: (i, 0, 0))],
            out_specs=pl.BlockSpec(memory_space=pl.ANY),
            scratch_shapes=[pltpu.VMEM((MOE_SLOTS * MOE_SUB, D_MODEL), BF16),
                            pltpu.VMEM((rows, D_MODEL), BF16),
                            pltpu.SemaphoreType.DMA((MOE_SLOTS,)),
                            pltpu.SemaphoreType.DMA((2,))]),
        compiler_params=_cparams(("arbitrary",)),
        name="moe_gather",
    )(tb["slot_e"], tb["slot_j"], tb["slot_dest"], tb["slot_valid"], tb["pad_dest"], tb["pad_valid"],
      tb["n_valid_rt"], xb, rt)


def _moe_ffn_kernel(e_ref, valid_ref, x_ref, wg_ref, wu_ref, wd_ref, y_ref, acc, *, rows):
    r = pl.program_id(0)
    f = pl.program_id(1)
    nf = pl.num_programs(1)
    ok = valid_ref[r] == 1

    def partial_out():
        xb = x_ref[...]
        g = _dot(xb, wg_ref[0].astype(BF16))
        u = _dot(xb, wu_ref[0].astype(BF16))
        return _dot((_silu(g) * u).astype(BF16), wd_ref[0].astype(BF16))

    @pl.when(jnp.logical_and(ok, f == 0))
    def _():
        acc[...] = partial_out()

    @pl.when(jnp.logical_and(ok, jnp.logical_and(f > 0, f < nf - 1)))
    def _():
        acc[...] += partial_out()

    @pl.when(jnp.logical_and(ok, f == nf - 1))
    def _():
        y_ref[...] = (acc[...] + partial_out()).astype(y_ref.dtype)

    @pl.when(jnp.logical_and(jnp.logical_not(ok), f == nf - 1))
    def _():
        y_ref[...] = jnp.zeros_like(y_ref)


def _moe_ffn(xg, w_gu_bf, w_d_bf, tb, rows):
    n_rt = tb["n_rt"]
    nf = D_FF_E // MOE_FF_BLK
    fb = MOE_FF_BLK

    def fsel(r, f, v):
        return jnp.where(v[r] == 1, f, nf - 1)

    return pl.pallas_call(
        functools.partial(_moe_ffn_kernel, rows=rows),
        out_shape=jax.ShapeDtypeStruct(xg.shape, BF16),
        grid_spec=pltpu.PrefetchScalarGridSpec(
            num_scalar_prefetch=2,
            grid=(n_rt, nf),
            in_specs=[pl.BlockSpec((rows, D_MODEL), lambda r, f, e, v: (r, 0)),
                      pl.BlockSpec((1, D_MODEL, fb), lambda r, f, e, v: (e[r], 0, fsel(r, f, v))),
                      pl.BlockSpec((1, D_MODEL, fb), lambda r, f, e, v: (e[r], 0, nf + fsel(r, f, v))),
                      pl.BlockSpec((1, fb, D_MODEL), lambda r, f, e, v: (e[r], fsel(r, f, v), 0))],
            out_specs=pl.BlockSpec((rows, D_MODEL), lambda r, f, e, v: (r, 0)),
            scratch_shapes=[pltpu.VMEM((rows, D_MODEL), F32)]),
        compiler_params=_cparams(("arbitrary", "arbitrary")),
        name="moe_grouped_ffn",
    )(tb["rt_e"], tb["rt_valid"], xg, w_gu_bf, w_gu_bf, w_d_bf)


def _moe_combine_kernel(se_ref, sj_ref, sd_ref, sv_ref, *refs):
    y_refs = refs[:MOE_SLOTS]
    rtt_ref, x_ref, g_ref, b_ref, o_ref, ybuf = refs[MOE_SLOTS:]
    i = pl.program_id(0)
    width = MOE_SLOTS * MOE_SUB
    lane = lax.broadcasted_iota(I32, (1, width), 1)
    lane_e = jnp.full((1, width), -1.0, F32)
    lane_r = (lane % MOE_SUB).astype(F32)
    for k in range(MOE_SLOTS):
        idx = i * MOE_SLOTS + k
        valid = sv_ref[idx] == 1
        yk = y_refs[k][...]
        ybuf[k * MOE_SUB:(k + 1) * MOE_SUB, :] = jnp.where(valid, yk, jnp.zeros_like(yk))
        here = (lane // MOE_SUB) == k
        lane_e = jnp.where(here, jnp.where(valid, se_ref[idx], -1).astype(F32), lane_e)
        lane_r = lane_r + jnp.where(here, (sj_ref[idx] * MOE_SUB).astype(F32), 0.0)
    rtt = rtt_ref[0]
    a1 = jnp.where((rtt[:, 0:1] == lane_e) & (rtt[:, 2:3] == lane_r), 1.0, 0.0).astype(BF16)
    a2 = jnp.where((rtt[:, 1:2] == lane_e) & (rtt[:, 3:4] == lane_r), 1.0, 0.0).astype(BF16)
    yb = ybuf[...]
    ff = rtt[:, 4:5] * _dot(a1, yb) + rtt[:, 5:6] * _dot(a2, yb)
    o_ref[...] = _layer_norm_rows(DN_ALPHA * x_ref[...] + ff, g_ref[...], b_ref[...])


def _moe_combine(yg, rtt, x, g, b, tb):
    tt = MOE_TOK_TILE
    T = x.shape[0]

    def y_map(i, se, sj, sd, sv, *, k):
        return (sd[i * MOE_SLOTS + k], 0)

    y_specs = [pl.BlockSpec((MOE_SUB, D_MODEL), functools.partial(y_map, k=k)) for k in range(MOE_SLOTS)]
    return pl.pallas_call(
        _moe_combine_kernel,
        out_shape=jax.ShapeDtypeStruct((T, D_MODEL), F32),
        grid_spec=pltpu.PrefetchScalarGridSpec(
            num_scalar_prefetch=4,
            grid=(T // tt,),
            in_specs=y_specs + [pl.BlockSpec((1, tt, 128), lambda i, *_: (i, 0, 0)),
                                pl.BlockSpec((tt, D_MODEL), lambda i, *_: (i, 0)),
                                pl.BlockSpec((1, D_MODEL), lambda i, *_: (0, 0)),
                                pl.BlockSpec((1, D_MODEL), lambda i, *_: (0, 0))],
            out_specs=pl.BlockSpec((tt, D_MODEL), lambda i, *_: (i, 0)),
            scratch_shapes=[pltpu.VMEM((MOE_SLOTS * MOE_SUB, D_MODEL), BF16)]),
        compiler_params=_cparams(("arbitrary",)),
        name="moe_combine",
    )(tb["slot_e"], tb["slot_j"], tb["slot_dest"], tb["slot_valid"], *([yg] * MOE_SLOTS), rtt, x,
      g.reshape(1, D_MODEL), b.reshape(1, D_MODEL))


def _moe(x, xb, w_router, w_gu_bf, w_d_bf, g, b, rows):
    T = x.shape[0]
    nt = T // MOE_TOK_TILE
    rt, rtt, cnt = _router(x, w_router)
    tb = _moe_tables(cnt[:, :, 0], nt, rows // MOE_SUB)
    xg = _moe_gather(xb, rt, tb, rows)
    yg = _moe_ffn(xg, w_gu_bf, w_d_bf, tb, rows)
    return _moe_combine(yg, rtt, x, g, b, tb)


def _trunk(x, pos_rows, n_batch, mem_k, mem_v, ret_state, past_k, past_v, conv_state, gdn_state, p, cfg):
    T = x.shape[0]
    per = T // n_batch
    tm = cfg["tm"]
    tm_mm = cfg["tm_mm"]
    ln_g, ln_b = p["ln_g"], p["ln_b"]

    qka, va, ga, qb, kb, vb = _proj0(x.astype(BF16), p["w_in0"], _rope_tables(pos_rows), tm_mm)
    oa, ret_new = _retention(qka, va, ga, ret_state, p["ret_norm_g"], n_batch, cfg["ret_lc"])
    lp = p["diff_lambda"].astype(F32)
    lam = (jnp.exp(jnp.sum(lp[0] * lp[1])) - jnp.exp(jnp.sum(lp[2] * lp[3])) + DIFF_LAMBDA_INIT).reshape(1)
    if past_k is None:
        ob = _diff_prompt(qb, kb, vb, lam, p["diff_norm_g"], n_batch, per, cfg["diff_tq"])
    else:
        ob = _diff_step(qb, kb, vb, past_k, past_v, lam, p["diff_norm_g"], n_batch, per)
    tm_dn = cfg["tm_dn"]
    x, q = _matmul_deepnorm([oa, ob], [p["w_out0"][:A_V], p["w_out0"][A_V:]], x, ln_g[0, 0], ln_b[0, 0], tm_dn,
                            w_next=p["w_xq"][0], next_scale=X_HD ** -0.5, name="out0_deepnorm_xq")
    xa = _xattn(q, mem_k, mem_v, 0, n_batch, cfg["x_tq"], cfg["mem_split"])
    x, xb = _matmul_deepnorm([xa], [p["w_xo"][0]], x, ln_g[0, 1], ln_b[0, 1], tm_dn, name="xo0_deepnorm")
    hmid = _swiglu_up(xb, p["w_ffn_gu"], cfg["ffn_tm"], cfg["ffn_tn"])
    x, xb = _matmul_deepnorm([hmid], [p["w_ffn_d"]], x, ln_g[0, 2], ln_b[0, 2], tm, name="ffn_down_deepnorm")

    proj1 = _matmul(xb, p["w_in1_main"][None], F32, tm_mm, 1024, name="proj1")[0]
    ab = _matmul(xb, p["w_in1_ab"][None], F32, tm_mm, 128, name="proj1_gates")[0]
    qkv3 = proj1.reshape(n_batch, per, 4 * C_W)[:, :, :3 * C_W]
    conv_new = qkv3[:, per - (CONV_W - 1):, :]
    state8 = jnp.pad(conv_state.astype(F32), ((0, 0), (8 - (CONV_W - 1), 0), (0, 0)))
    qkvn = _gdn_conv(proj1, state8, p["conv_w"], n_batch, cfg["conv_tm"])
    lc = cfg["gdn_lc"]
    if per < lc:
        padrows = lambda a: jnp.pad(a.reshape(n_batch, per, a.shape[-1]),
                                    ((0, 0), (0, lc - per), (0, 0))).reshape(n_batch * lc, a.shape[-1])
        og, gdn_new = _gdn(padrows(qkvn), padrows(proj1), padrows(ab), p["a_log"], p["dt_bias"],
                           p["gdn_norm_g"], gdn_state, n_batch, lc, per)
        og = og.reshape(n_batch, lc, C_W)[:, :per].reshape(T, C_W)
    else:
        og, gdn_new = _gdn(qkvn, proj1, ab, p["a_log"], p["dt_bias"], p["gdn_norm_g"], gdn_state,
                           n_batch, lc, lc)
    x, q = _matmul_deepnorm([og], [p["w_out1"]], x, ln_g[1, 0], ln_b[1, 0], tm_dn,
                            w_next=p["w_xq"][1], next_scale=X_HD ** -0.5, name="out1_deepnorm_xq")
    xa = _xattn(q, mem_k, mem_v, 1, n_batch, cfg["x_tq"], cfg["mem_split"])
    x, xb = _matmul_deepnorm([xa], [p["w_xo"][1]], x, ln_g[1, 1], ln_b[1, 1], tm_dn, name="xo1_deepnorm")
    x = _moe(x, xb, p["w_router"], p["w_moe_gu"], p["w_moe_d"], ln_g[1, 2], ln_b[1, 2], cfg["moe_rows"])
    return x, kb, vb, ret_new, conv_new, gdn_new


PROMPT_CFG = dict(tm=512, tm_mm=1024, tm_dn=1024, ret_lc=256, diff_tq=512, x_tq=512, ffn_tm=256, ffn_tn=2816, conv_tm=512,
                  gdn_lc=256, moe_rows=1024, mem_split=False)
STEP_CFG = dict(tm=512, tm_mm=512, tm_dn=512, ret_lc=16, diff_tq=16, x_tq=16, ffn_tm=256, ffn_tn=2816, conv_tm=16,
                gdn_lc=64, moe_rows=256, mem_split=True)


def kernel(x_prompt, x_sample, cache_diff_k, cache_diff_v, state_ret, state_gdn_conv, state_gdn, cache_mem_k, cache_mem_v, mem_prompt, w_in0, ret_norm_g, diff_lambda, diff_norm_g, w_out0, w_in1, conv_w, a_log, dt_bias, gdn_norm_g, w_out1, w_xq, w_xkv, w_xo, w_ffn_gu, w_ffn_d, w_router, w_moe_gu, w_moe_d, ln_g, ln_b):
    B, S, _ = x_prompt.shape
    DB, L, _ = x_sample.shape
    P = cache_diff_k.shape[1]
    bf = lambda w: w.astype(BF16)
    w_ab = jnp.pad(w_in1[:, 4 * C_W:], ((0, 0), (0, 128 - 2 * H_C)))
    p = dict(w_in0=bf(w_in0), ret_norm_g=ret_norm_g, diff_lambda=diff_lambda, diff_norm_g=diff_norm_g,
             w_out0=bf(w_out0), w_in1_main=bf(w_in1[:, :4 * C_W]), w_in1_ab=bf(w_ab), conv_w=conv_w,
             a_log=a_log, dt_bias=dt_bias, gdn_norm_g=gdn_norm_g, w_out1=bf(w_out1), w_xq=bf(w_xq),
             w_xo=bf(w_xo), w_ffn_gu=bf(w_ffn_gu), w_ffn_d=bf(w_ffn_d), w_router=w_router,
             w_moe_gu=w_moe_gu, w_moe_d=w_moe_d, ln_g=ln_g, ln_b=ln_b)

    mem = mem_prompt.reshape(B * N_MEM, D_MODEL)
    mk_p = _matmul(mem, bf(w_xkv[:, :, :D_MODEL]), F32, 1024, 1024, name="mem_k")
    mv_p = _matmul(mem, bf(w_xkv[:, :, D_MODEL:]), F32, 1024, 1024, name="mem_v")

    y_p, dk_p, dv_p, ret_p, conv_p, gdn_p = _trunk(
        x_prompt.reshape(B * S, D_MODEL), jnp.arange(S), B, mk_p, mv_p,
        jnp.zeros((B, H_A, DK_A, DV_A), F32), None, None,
        jnp.zeros((B, CONV_W - 1, 3 * C_W), F32), jnp.zeros((B, H_C, DK_C, DV_C), F32), p, PROMPT_CFG)

    pos_s = jnp.tile(P + jnp.arange(L), DB)

    def mem_rows(c):
        c = c.reshape(DEPTH, DB * N_MEM, X_HEADS, X_LANE_BLKS, 128).transpose(0, 1, 3, 2, 4)
        return c.reshape(DEPTH, DB * N_MEM * X_LANE_BLKS * X_HEADS, 128)

    y_s, dk_s, dv_s, ret_s, conv_s, gdn_s = _trunk(
        x_sample.reshape(DB * L, D_MODEL), pos_s, DB, mem_rows(cache_mem_k), mem_rows(cache_mem_v),
        state_ret.astype(F32), cache_diff_k.reshape(DB * P * H_B, DV_B), cache_diff_v.reshape(DB * P * H_B, DV_B),
        state_gdn_conv, state_gdn.astype(F32), p, STEP_CFG)

    shape5 = (DEPTH, B, N_MEM, X_HEADS, X_HD)
    return (y_p.reshape(B, S, D_MODEL), y_s.reshape(DB, L, D_MODEL),
            mk_p.reshape(shape5), mv_p.reshape(shape5),
            dk_p.reshape(B, S, H_B, 2 * DK_B), dv_p.reshape(B, S, H_B, DV_B),
            ret_p, conv_p, gdn_p,
            dk_s.reshape(DB, L, H_B, 2 * DK_B), dv_s.reshape(DB, L, H_B, DV_B),
            ret_s, conv_s, gdn_s)
```
